```python
import jax, jax.numpy as jnp
from jax import lax
import numpy as np

D_MODEL = 1024
BATCH = 4
SEQ = 8192
DEPTH = 1
DEC_BATCH = 32
DEC_SEQ = 4
PAST_LEN = 16384
PAGE_SIZE = 128

D_MIX = D_MODEL
D_LRU = D_MIX // 2
N_LRU_BLOCKS = 8
LRU_BLOCK = D_LRU // N_LRU_BLOCKS
CONV_WIDTH = 4
LRU_C = 8.0
D_ATTN = D_MIX - D_LRU
N_HEADS = 8
HEAD_DIM = D_ATTN // N_HEADS
DILATED_PATTERNS = ((128, 1), (512, 4), (2048, 16))
MAX_WINDOW = max(w for w, _ in DILATED_PATTERNS)
Q_BLOCK = 128
D_FF = -(-8 * D_MODEL // (3 * 256)) * 256
D_IN = 2 * D_LRU + 3 * D_ATTN
RMS_EPS = 1e-6

kernel_name = 'hymba_rglru_dilated_swa_step'


def rmsnorm(x, g):
    xf = x.astype(jnp.float32)
    y = xf * lax.rsqrt(jnp.mean(xf * xf, axis=-1, keepdims=True) + RMS_EPS)
    return (y * g.astype(jnp.float32)).astype(x.dtype)


def project_in(xn, w_in):
    b, t, _ = xn.shape
    proj = xn @ w_in
    u = proj[..., :D_LRU]
    gate = proj[..., D_LRU:2 * D_LRU]
    o = 2 * D_LRU
    q = proj[..., o:o + D_ATTN].reshape(b, t, N_HEADS, HEAD_DIM)
    k = proj[..., o + D_ATTN:o + 2 * D_ATTN].reshape(b, t, N_HEADS, HEAD_DIM)
    v = proj[..., o + 2 * D_ATTN:].reshape(b, t, N_HEADS, HEAD_DIM)
    return u, gate, q, k, v


def causal_depthwise_conv(u, prev, w, bias):
    t = u.shape[1]
    full = jnp.concatenate([prev.astype(u.dtype), u], axis=1)
    y = bias.astype(u.dtype)
    for j in range(CONV_WIDTH):
        y = y + full[:, j:j + t] * w[j]
    return y, full[:, -(CONV_WIDTH - 1):]


def block_diag(u, w, bias):
    b, t, _ = u.shape
    ub = u.reshape(b, t, N_LRU_BLOCKS, LRU_BLOCK)
    y = jnp.einsum('btni,nij->btnj', ub, w.astype(jnp.float32))
    return y.reshape(b, t, D_LRU) + bias.astype(jnp.float32)


def rg_lru(u, h0, w_a, b_a, w_x, b_x, lam):
    uf = u.astype(jnp.float32)
    r = jax.nn.sigmoid(block_diag(uf, w_a, b_a))
    i = jax.nn.sigmoid(block_diag(uf, w_x, b_x))
    log_a = -LRU_C * r * jax.nn.softplus(-lam.astype(jnp.float32))
    a = jnp.exp(log_a)
    bx = jnp.sqrt(-jnp.expm1(2.0 * log_a)) * (i * uf)
    bx = bx.at[:, 0].add(a[:, 0] * h0.astype(jnp.float32))

    def combine(left, right):
        a1, b1 = left
        a2, b2 = right
        return a1 * a2, a2 * b1 + b2

    _, h = lax.associative_scan(combine, (a, bx), axis=1)
    return h, h[:, -1]


def lru_group(u, gate, conv_prev, h0, conv_w, conv_b, w_a, b_a, w_x, b_x, lam):
    uc, new_conv = causal_depthwise_conv(u, conv_prev, conv_w, conv_b)
    h, h_last = rg_lru(uc, h0, w_a, b_a, w_x, b_x, lam)
    y = h * jax.nn.gelu(gate.astype(jnp.float32))
    return y.astype(u.dtype), new_conv, h_last


def dilated_attention_block(q, k_ext, v_ext, q_pos):
    scale = HEAD_DIM ** -0.5
    qf = q.astype(jnp.float32)
    outs, lses = [], []
    for window, dil in DILATED_PATTERNS:
        n_keys = window // dil + 1
        dist = jnp.arange(n_keys, dtype=jnp.int32) * dil
        idx = q_pos[:, None] - dist[None, :]
        valid = idx >= 0
        idx_c = jnp.maximum(idx, 0)
        kg = k_ext[:, idx_c].astype(jnp.float32)
        vg = v_ext[:, idx_c].astype(jnp.float32)
        s = jnp.einsum('bqhd,bqkhd->bhqk', qf, kg) * scale
        s = jnp.where(valid[None, None], s, -jnp.inf)
        lse = jax.nn.logsumexp(s, axis=-1)
        p = jnp.exp(s - lse[..., None])
        outs.append(jnp.einsum('bhqk,bqkhd->bqhd', p, vg))
        lses.append(lse)
    wts = jax.nn.softmax(jnp.stack(lses, axis=0), axis=0)
    wts = jnp.transpose(wts, (0, 1, 3, 2))[..., None]
    out = wts[0] * outs[0]
    for pi in range(1, len(DILATED_PATTERNS)):
        out = out + wts[pi] * outs[pi]
    return out


def prompt_attention(q, k, v):
    b, s, h, dh = q.shape
    n_blk = s // Q_BLOCK
    q_blocks = q.reshape(b, n_blk, Q_BLOCK, h, dh).transpose(1, 0, 2, 3, 4)
    starts = jnp.arange(n_blk, dtype=jnp.int32) * Q_BLOCK

    def one_block(args):
        qb, s0 = args
        return dilated_attention_block(qb, k, v, s0 + jnp.arange(Q_BLOCK, dtype=jnp.int32))

    o = lax.map(one_block, (q_blocks, starts))
    return o.transpose(1, 0, 2, 3, 4).reshape(b, s, h * dh)


def merge_groups(y_lru, y_attn, g_lru, g_attn, w_out):
    y = jnp.concatenate([rmsnorm(y_lru, g_lru), rmsnorm(y_attn, g_attn)], axis=-1)
    return y @ w_out


def swiglu(x, w_gate, w_up, w_down):
    return (jax.nn.silu(x @ w_gate) * (x @ w_up)) @ w_down


def setup_inputs(seed: int = 0) -> dict:
    key = jax.random.key(seed)
    ks = jax.random.split(key, 24)
    f32 = jnp.float32
    w_buf = min(MAX_WINDOW, PAST_LEN)
    nrm = lambda k, shape, s: jax.random.normal(k, shape, f32) * s
    u_a = jax.random.uniform(ks[10], (DEPTH, D_LRU), f32, minval=0.9, maxval=0.999)
    a_base = u_a ** (1.0 / LRU_C)
    lam = jnp.log(a_base) - jnp.log1p(-a_base)
    return {
        'x_prompt': nrm(ks[0], (BATCH, SEQ, D_MODEL), 1.0),
        'x_sample': nrm(ks[1], (DEC_BATCH, DEC_SEQ, D_MODEL), 1.0),
        'state_conv': nrm(ks[2], (DEPTH, DEC_BATCH, CONV_WIDTH - 1, D_LRU), 1.0),
        'state_lru': nrm(ks[3], (DEPTH, DEC_BATCH, D_LRU), 0.5),
        'cache_k': nrm(ks[4], (DEPTH, DEC_BATCH, w_buf, N_HEADS, HEAD_DIM), 1.0),
        'cache_v': nrm(ks[5], (DEPTH, DEC_BATCH, w_buf, N_HEADS, HEAD_DIM), 1.0),
        'norm_mix': 1.0 + nrm(ks[6], (DEPTH, D_MODEL), 0.01),
        'w_in': nrm(ks[7], (DEPTH, D_MODEL, D_IN), D_MODEL ** -0.5),
        'conv_w': nrm(ks[8], (DEPTH, CONV_WIDTH, D_LRU), CONV_WIDTH ** -0.5),
        'conv_b': nrm(ks[9], (DEPTH, D_LRU), 0.01),
        'lru_w_a': nrm(ks[11], (DEPTH, N_LRU_BLOCKS, LRU_BLOCK, LRU_BLOCK), LRU_BLOCK ** -0.5),
        'lru_b_a': nrm(ks[12], (DEPTH, D_LRU), 0.01),
        'lru_w_x': nrm(ks[13], (DEPTH, N_LRU_BLOCKS, LRU_BLOCK, LRU_BLOCK), LRU_BLOCK ** -0.5),
        'lru_b_x': nrm(ks[14], (DEPTH, D_LRU), 0.01),
        'lru_lambda': lam,
        'out_norm_lru': 1.0 + nrm(ks[15], (DEPTH, D_LRU), 0.01),
        'out_norm_attn': 1.0 + nrm(ks[16], (DEPTH, D_ATTN), 0.01),
        'w_out': nrm(ks[17], (DEPTH, D_MIX, D_MODEL), D_MIX ** -0.5),
        'norm_ffn': 1.0 + nrm(ks[18], (DEPTH, D_MODEL), 0.01),
        'w_gate': nrm(ks[19], (DEPTH, D_MODEL, D_FF), D_MODEL ** -0.5),
        'w_up': nrm(ks[20], (DEPTH, D_MODEL, D_FF), D_MODEL ** -0.5),
        'w_down': nrm(ks[21], (DEPTH, D_FF, D_MODEL), D_FF ** -0.5),
        'norm_final': 1.0 + nrm(ks[22], (D_MODEL,), 0.01),
    }


def reference(x_prompt, x_sample, state_conv, state_lru, cache_k, cache_v, norm_mix, w_in,
              conv_w, conv_b, lru_w_a, lru_b_a, lru_w_x, lru_b_x, lru_lambda, out_norm_lru,
              out_norm_attn, w_out, norm_ffn, w_gate, w_up, w_down, norm_final):
    xp, xs = x_prompt, x_sample
    bp, sp = xp.shape[0], xp.shape[1]
    bs, ts = xs.shape[0], xs.shape[1]
    w_p = min(MAX_WINDOW, sp)
    w_buf = cache_k.shape[2]
    conv_p, lru_p, kw_p, vw_p = [], [], [], []
    conv_s, lru_s, kn_s, vn_s = [], [], [], []
    for l in range(DEPTH):
        lru_args = (conv_w[l], conv_b[l], lru_w_a[l], lru_b_a[l], lru_w_x[l], lru_b_x[l], lru_lambda[l])
        xn = rmsnorm(xp, norm_mix[l])
        u, gate, q, k, v = project_in(xn, w_in[l])
        y_lru, c_new, h_last = lru_group(u, gate, jnp.zeros((bp, CONV_WIDTH - 1, D_LRU), xp.dtype),
                                         jnp.zeros((bp, D_LRU), jnp.float32), *lru_args)
        y_att = prompt_attention(q, k, v).astype(xp.dtype)
        xp = xp + merge_groups(y_lru, y_att, out_norm_lru[l], out_norm_attn[l], w_out[l])
        xp = xp + swiglu(rmsnorm(xp, norm_ffn[l]), w_gate[l], w_up[l], w_down[l])
        conv_p.append(c_new)
        lru_p.append(h_last.astype(xp.dtype))
        kw_p.append(k[:, sp - w_p:])
        vw_p.append(v[:, sp - w_p:])
        xn = rmsnorm(xs, norm_mix[l])
        u, gate, q, k, v = project_in(xn, w_in[l])
        y_lru, c_new, h_last = lru_group(u, gate, state_conv[l], state_lru[l], *lru_args)
        k_ext = jnp.concatenate([cache_k[l].astype(k.dtype), k], axis=1)
        v_ext = jnp.concatenate([cache_v[l].astype(v.dtype), v], axis=1)
        q_pos = w_buf + jnp.arange(ts, dtype=jnp.int32)
        y_att = dilated_attention_block(q, k_ext, v_ext, q_pos).reshape(bs, ts, D_ATTN).astype(xs.dtype)
        xs = xs + merge_groups(y_lru, y_att, out_norm_lru[l], out_norm_attn[l], w_out[l])
        xs = xs + swiglu(rmsnorm(xs, norm_ffn[l]), w_gate[l], w_up[l], w_down[l])
        conv_s.append(c_new)
        lru_s.append(h_last.astype(xs.dtype))
        kn_s.append(k)
        vn_s.append(v)
    y_prompt = rmsnorm(xp, norm_final)
    y_sample = rmsnorm(xs, norm_final)
    return (y_prompt, y_sample,
            jnp.stack(conv_p, 0), jnp.stack(lru_p, 0), jnp.stack(kw_p, 0), jnp.stack(vw_p, 0),
            jnp.stack(conv_s, 0), jnp.stack(lru_s, 0), jnp.stack(kn_s, 0), jnp.stack(vn_s, 0))
```

```python
import functools
import math

import jax
import jax.numpy as jnp
from jax import lax
from jax.experimental import pallas as pl
from jax.experimental.pallas import tpu as pltpu

F32 = jnp.float32
BF16 = jnp.bfloat16

D_MODEL = 1024
D_LRU = 512
D_ATTN = 512
N_HEADS = 8
HEAD_DIM = 64
N_LRU_BLOCKS = 8
LRU_BLOCK = D_LRU // N_LRU_BLOCKS
CONV_WIDTH = 4
LRU_C = 8.0
DILATED_PATTERNS = ((128, 1), (512, 4), (2048, 16))
MAX_WINDOW = 2048
D_FF = 2816
D_IN = 2 * D_LRU + 3 * D_ATTN
RMS_EPS = 1e-6
Q_SCALE = HEAD_DIM ** -0.5

LANES = 128
SUBLANES = 8
VMEM_LIMIT_BYTES = 56 * 1024 * 1024

KEYS_PER_PATTERN = 129
QBLK = 128
SUPER = 2048
HEADS_PER_STEP = LANES // HEAD_DIM
NEG_BIG = -1e30
HEAD_SHIFT = HEAD_DIM.bit_length() - 1
HEADS_SHIFT = N_HEADS.bit_length() - 1
assert all(d & (d - 1) == 0 for _, d in DILATED_PATTERNS) and 1 << HEAD_SHIFT == HEAD_DIM
assert all(w // d + 1 == KEYS_PER_PATTERN for w, d in DILATED_PATTERNS)


def _rms_scale(x):
    return lax.rsqrt(jnp.mean(x * x, axis=-1, keepdims=True) + RMS_EPS)


def _inproj_kernel(x_ref, g_ref, w_ref, u_ref, gate_ref, q_ref, k_ref, v_ref):
    x = x_ref[...]
    xn = (x * _rms_scale(x) * g_ref[...]).astype(BF16)
    proj = jnp.dot(xn, w_ref[...], preferred_element_type=F32)
    u_ref[...] = proj[:, 0:D_LRU]
    gate_ref[...] = proj[:, D_LRU:2 * D_LRU]
    o = 2 * D_LRU
    q_ref[...] = proj[:, o:o + D_ATTN] * Q_SCALE
    k_ref[...] = proj[:, o + D_ATTN:o + 2 * D_ATTN]
    v_ref[...] = proj[:, o + 2 * D_ATTN:o + 3 * D_ATTN]


def _inproj(x2d, g, w_bf, tm):
    n = x2d.shape[0]
    const = lambda i: (0, 0)
    row = lambda i: (i, 0)
    out_sds = jax.ShapeDtypeStruct((n, D_LRU), F32)
    return pl.pallas_call(
        _inproj_kernel,
        grid=(n // tm,),
        in_specs=[
            pl.BlockSpec((tm, D_MODEL), row),
            pl.BlockSpec((1, D_MODEL), const),
            pl.BlockSpec((D_MODEL, D_IN), const, pipeline_mode=pl.Buffered(1)),
        ],
        out_specs=[pl.BlockSpec((tm, D_LRU), row)] * 5,
        out_shape=[out_sds] * 5,
        compiler_params=pltpu.CompilerParams(
            dimension_semantics=("arbitrary",), vmem_limit_bytes=VMEM_LIMIT_BYTES),
        name="inproj",
    )(x2d, g, w_bf)


def _gelu_tanh(x):
    c = math.sqrt(2.0 / math.pi)
    return 0.5 * x * (1.0 + jnp.tanh(c * (x + 0.044715 * (x * x * x))))


def _sigmoid(x):
    return jax.nn.sigmoid(x)


def _lru_gates(uc, wg_ref, ba, bx_bias, neg_c_softplus):
    half = D_LRU // 2
    ucb = uc.astype(BF16)
    pre0 = jnp.dot(ucb[:, 0:half], wg_ref[0], preferred_element_type=F32)
    pre1 = jnp.dot(ucb[:, half:D_LRU], wg_ref[1], preferred_element_type=F32)
    r_pre = jnp.concatenate([pre0[:, 0:half], pre1[:, 0:half]], axis=1) + ba
    i_pre = jnp.concatenate([pre0[:, half:], pre1[:, half:]], axis=1) + bx_bias
    r = _sigmoid(r_pre)
    i = _sigmoid(i_pre)
    log_a = r * neg_c_softplus
    a = jnp.exp(log_a)
    b = jnp.sqrt(-jnp.tanh(log_a) * (a * a + 1.0)) * (i * uc)
    return a, b


def _shift_rows(x, s, fill):
    t = x.shape[0]
    if s % SUBLANES == 0:
        return jnp.concatenate([jnp.full((s, x.shape[1]), fill, x.dtype), x[0:t - s]], axis=0)
    rolled = pltpu.roll(x, s, axis=0)
    row = lax.broadcasted_iota(jnp.int32, x.shape, 0)
    return jnp.where(row < s, fill, rolled)


def _scan_rows(a, b):
    t = a.shape[0]
    s = 1
    while s < t:
        b = a * _shift_rows(b, s, 0.0) + b
        if 2 * s < t:
            a = a * _shift_rows(a, s, 1.0)
        s *= 2
    return b


LRU_CHUNK = 256


def _lru_prompt_kernel(u_ref, gate_ref, cprev_ref, h0_ref, cw_ref, cb_ref, wg_ref, ba_ref, bx_ref,
                       ncs_ref, gl_ref, y_ref, conv_ref, hlast_ref, tail_ref, h_ref, *, n_chunks):
    ti = pl.program_id(1)
    n_t = pl.num_programs(1)

    @pl.when(ti == 0)
    def _():
        tail_ref[...] = jnp.zeros_like(tail_ref)
        tail_ref[SUBLANES - (CONV_WIDTH - 1):SUBLANES, :] = cprev_ref[0]
        h_ref[...] = jnp.broadcast_to(h0_ref[0], h_ref.shape)

    cw = cw_ref[...]
    cb = cb_ref[...]
    ba = ba_ref[...]
    bxb = bx_ref[...]
    ncs = ncs_ref[...]
    gl = gl_ref[...]
    row8 = lax.broadcasted_iota(jnp.int32, (SUBLANES, D_LRU), 0)
    row_c = lax.broadcasted_iota(jnp.int32, (LRU_CHUNK, D_LRU), 0)

    def chunk(ci, carry):
        r0 = pl.multiple_of(ci * LRU_CHUNK, LRU_CHUNK)
        u = u_ref[0, pl.ds(r0, LRU_CHUNK), :]
        tail = tail_ref[...]
        uc = cb + cw[CONV_WIDTH - 1:CONV_WIDTH, :] * u
        for s in range(1, CONV_WIDTH):
            ru = pltpu.roll(u, s, axis=0)
            rt = pltpu.roll(tail, s, axis=0)
            first = jnp.where(row8 < s, rt, ru[0:SUBLANES])
            us = jnp.concatenate([first, ru[SUBLANES:]], axis=0)
            uc = uc + cw[CONV_WIDTH - 1 - s:CONV_WIDTH - s, :] * us
        tail_ref[...] = u[LRU_CHUNK - SUBLANES:LRU_CHUNK]
        a, b = _lru_gates(uc, wg_ref, ba, bxb, ncs)
        h_prev = h_ref[0:1, :]
        b = jnp.where(row_c == 0, b + a * h_prev, b)
        h = _scan_rows(a, b)
        h_ref[...] = jnp.broadcast_to(h[LRU_CHUNK - 1:LRU_CHUNK], h_ref.shape)
        y = h * _gelu_tanh(gate_ref[0, pl.ds(r0, LRU_CHUNK), :])
        y_ref[0, pl.ds(r0, LRU_CHUNK), :] = (y * _rms_scale(y) * gl).astype(BF16)
        return carry

    lax.fori_loop(0, n_chunks, chunk, 0)

    @pl.when(ti == n_t - 1)
    def _():
        conv_ref[0] = tail_ref[SUBLANES - (CONV_WIDTH - 1):SUBLANES, :]
        hlast_ref[0] = h_ref[0:1, :]


def _lru_prompt(u, gate, conv_prev, h0, lw, tt):
    b, s, _ = u.shape
    assert s % tt == 0 and tt % LRU_CHUNK == 0
    seq = lambda bi, ti: (bi, ti, 0)
    per_b = lambda bi, ti: (bi, 0, 0)
    c2 = lambda bi, ti: (0, 0)
    c3 = lambda bi, ti: (0, 0, 0)
    vec = pl.BlockSpec((1, D_LRU), c2)
    return pl.pallas_call(
        functools.partial(_lru_prompt_kernel, n_chunks=tt // LRU_CHUNK),
        grid=(b, s // tt),
        in_specs=[
            pl.BlockSpec((1, tt, D_LRU), seq),
            pl.BlockSpec((1, tt, D_LRU), seq),
            pl.BlockSpec((1, CONV_WIDTH - 1, D_LRU), per_b),
            pl.BlockSpec((1, 1, D_LRU), per_b),
            pl.BlockSpec((CONV_WIDTH, D_LRU), c2),
            vec,
            pl.BlockSpec((2, D_LRU // 2, D_LRU), c3),
            vec, vec, vec, vec,
        ],
        out_specs=[
            pl.BlockSpec((1, tt, D_LRU), seq),
            pl.BlockSpec((1, CONV_WIDTH - 1, D_LRU), per_b),
            pl.BlockSpec((1, 1, D_LRU), per_b),
        ],
        out_shape=[
            jax.ShapeDtypeStruct((b, s, D_LRU), BF16),
            jax.ShapeDtypeStruct((b, CONV_WIDTH - 1, D_LRU), F32),
            jax.ShapeDtypeStruct((b, 1, D_LRU), F32),
        ],
        scratch_shapes=[pltpu.VMEM((SUBLANES, D_LRU), F32), pltpu.VMEM((SUBLANES, D_LRU), F32)],
        compiler_params=pltpu.CompilerParams(
            dimension_semantics=("arbitrary", "arbitrary"), vmem_limit_bytes=VMEM_LIMIT_BYTES),
        name="lru_prompt",
    )(u, gate, conv_prev, h0, lw["conv_w"], lw["conv_b"], lw["w_gates"], lw["b_a"], lw["b_x"],
      lw["neg_c_softplus"], lw["g_lru"])


def _lru_sample_kernel(u_ref, gate_ref, cprev_ref, h0_ref, cw_ref, cb_ref, wg_ref, ba_ref, bx_ref,
                       ncs_ref, gl_ref, y_ref, conv_ref, hlast_ref):
    t_len, nb, _ = u_ref.shape
    cw = cw_ref[...]
    hist = [cprev_ref[j] for j in range(CONV_WIDTH - 1)] + [u_ref[t] for t in range(t_len)]
    ucs = []
    for t in range(t_len):
        uc = cb_ref[...]
        for j in range(CONV_WIDTH):
            uc = uc + cw[j:j + 1, :] * hist[t + j]
        ucs.append(uc)
    uc_all = jnp.concatenate(ucs, axis=0)
    a, b = _lru_gates(uc_all, wg_ref, ba_ref[...], bx_ref[...], ncs_ref[...])
    h = h0_ref[...]
    gl = gl_ref[...]
    for t in range(t_len):
        h = a[t * nb:(t + 1) * nb] * h + b[t * nb:(t + 1) * nb]
        y = h * _gelu_tanh(gate_ref[t])
        y_ref[t] = (y * _rms_scale(y) * gl).astype(BF16)
    hlast_ref[...] = h
    for j in range(CONV_WIDTH - 1):
        conv_ref[j] = hist[t_len + j]


def _lru_sample(u_t, gate_t, conv_prev_t, h0, lw):
    t_len, nb, _ = u_t.shape
    return pl.pallas_call(
        _lru_sample_kernel,
        out_shape=[
            jax.ShapeDtypeStruct((t_len, nb, D_LRU), BF16),
            jax.ShapeDtypeStruct((CONV_WIDTH - 1, nb, D_LRU), F32),
            jax.ShapeDtypeStruct((nb, D_LRU), F32),
        ],
        name="lru_sample",
    )(u_t, gate_t, conv_prev_t, h0, lw["conv_w"], lw["conv_b"], lw["w_gates"], lw["b_a"], lw["b_x"],
      lw["neg_c_softplus"], lw["g_lru"])


def _attn_prompt_kernel(q_ref, k_ref, v_ref, o_ref,
                        qp1, qp4, qp16, kb1, kb4, kb16, vb1, vb4, vb16, res_o, res_l, bias_ref):
    sb = pl.program_id(2)
    qps = (qp1, qp4, qp16)
    kbs = (kb1, kb4, kb16)
    vbs = (vb1, vb4, vb16)
    dils = tuple(d for _, d in DILATED_PATTERNS)

    qi = lax.broadcasted_iota(jnp.int32, (QBLK, 2 * QBLK), 0)
    ki = lax.broadcasted_iota(jnp.int32, (QBLK, 2 * QBLK), 1)
    delta = qi + QBLK - ki
    band = (delta >= 0) & (delta <= KEYS_PER_PATTERN - 1)
    bias_ref[0] = jnp.where(band, 0.0, NEG_BIG)
    bias_ref[1] = jnp.where(band & (ki >= QBLK), 0.0, NEG_BIG)

    for kb, vb, d in zip(kbs, vbs, dils):
        cls_len = SUPER // d

        @pl.when(sb == 0)
        def _():
            kb[:, 0:QBLK, :] = jnp.zeros((d, QBLK, LANES), BF16)
            vb[:, 0:QBLK, :] = jnp.zeros((d, QBLK, LANES), BF16)

        @pl.when(sb > 0)
        def _():
            kb[:, 0:QBLK, :] = kb[:, cls_len:cls_len + QBLK, :]
            vb[:, 0:QBLK, :] = vb[:, cls_len:cls_len + QBLK, :]

    for qp, kb, vb, d in zip(qps, kbs, vbs, dils):
        cls_len = SUPER // d
        for c in range(d):
            if d == 1:
                rows = pl.ds(0, SUPER)
            else:
                rows = pl.ds(c, cls_len, stride=d)
            qp[c] = q_ref[0, rows, :].astype(BF16)
            kb[c, QBLK:QBLK + cls_len, :] = k_ref[0, rows, :].astype(BF16)
            vb[c, QBLK:QBLK + cls_len, :] = v_ref[0, rows, :].astype(BF16)

    lane = lax.broadcasted_iota(jnp.int32, (QBLK, LANES), 1)
    head0 = lane < HEAD_DIM

    for p, (qp, kb, vb, d) in enumerate(zip(qps, kbs, vbs, dils)):
        blocks_per_class = SUPER // d // QBLK

        def unit(n, carry, p=p, qp=qp, kb=kb, vb=vb, d=d, blocks_per_class=blocks_per_class):
            c = n // blocks_per_class
            i = n % blocks_per_class
            r0 = pl.multiple_of(i * QBLK, QBLK)
            q_t = qp[c, pl.ds(r0, QBLK), :]
            k_t = kb[c, pl.ds(r0, 2 * QBLK), :]
            v_t = vb[c, pl.ds(r0, 2 * QBLK), :]
            first = jnp.logical_and(sb == 0, i == 0).astype(jnp.int32)
            bias = bias_ref[first]
            outs, lses = [], []
            for h in range(HEADS_PER_STEP):
                hm = head0 if h == 0 else jnp.logical_not(head0)
                qh = jnp.where(hm, q_t, jnp.zeros_like(q_t))
                s = lax.dot_general(qh, k_t, (((1,), (1,)), ((), ())), preferred_element_type=F32)
                s = s + bias
                m = jnp.max(s, axis=-1, keepdims=True)
                pr = jnp.exp(s - m)
                l = jnp.sum(pr, axis=-1, keepdims=True)
                o = jnp.dot(pr.astype(BF16), v_t, preferred_element_type=F32)
                outs.append(o * (1.0 / l))
                lses.append(m + jnp.log(l))
            o2 = jnp.where(head0, outs[0], outs[1])
            l2 = jnp.where(head0, lses[0], lses[1])
            start = c + d * r0
            if d == 1:
                dst = pl.ds(pl.multiple_of(start, QBLK), QBLK)
            else:
                dst = pl.ds(start, QBLK, stride=d)
            res_o[p, dst, :] = o2
            res_l[p, dst, :] = l2
            return carry

        lax.fori_loop(0, SUPER // QBLK, unit, 0)

    def merge(ci, carry):
        rows = pl.ds(pl.multiple_of(ci * 256, 256), 256)
        ls = [res_l[p, rows, :] for p in range(3)]
        m = jnp.maximum(jnp.maximum(ls[0], ls[1]), ls[2])
        es = [jnp.exp(x - m) for x in ls]
        den = es[0] + es[1] + es[2]
        num = es[0] * res_o[0, rows, :] + es[1] * res_o[1, rows, :] + es[2] * res_o[2, rows, :]
        o_ref[0, rows, :] = num * (1.0 / den)
        return carry

    lax.fori_loop(0, SUPER // 256, merge, 0)


def _attn_prompt(q, k, v):
    b, s, _ = q.shape
    assert s % SUPER == 0
    blk = pl.BlockSpec((1, SUPER, LANES), lambda bi, hp, sb: (bi, sb, hp))
    scratch = []
    for _, d in DILATED_PATTERNS:
        scratch.append(pltpu.VMEM((d, SUPER // d, LANES), BF16))
    for _ in range(2):
        for _, d in DILATED_PATTERNS:
            scratch.append(pltpu.VMEM((d, QBLK + SUPER // d, LANES), BF16))
    scratch += [pltpu.VMEM((3, SUPER, LANES), F32), pltpu.VMEM((3, SUPER, LANES), F32),
                pltpu.VMEM((2, QBLK, 2 * QBLK), F32)]
    return pl.pallas_call(
        _attn_prompt_kernel,
        grid=(b, D_ATTN // LANES, s // SUPER),
        in_specs=[blk, blk, blk],
        out_specs=blk,
        out_shape=jax.ShapeDtypeStruct((b, s, D_ATTN), F32),
        scratch_shapes=scratch,
        compiler_params=pltpu.CompilerParams(
            dimension_semantics=("arbitrary", "arbitrary", "arbitrary"),
            vmem_limit_bytes=VMEM_LIMIT_BYTES),
        name="attn_prompt",
    )(q, k, v)


def _attn_sample_kernel(q_ref, kn_ref, vn_ref, ck_ref, cv_ref, o_ref, *, t_len, w_buf):
    rows = t_len * N_HEADS
    q = q_ref[0]
    qrep = jnp.broadcast_to(q[:, None, :], (t_len, N_HEADS, D_ATTN)).reshape(rows, D_ATTN)
    rid = lax.broadcasted_iota(jnp.int32, (rows, D_ATTN), 0)
    lid = lax.broadcasted_iota(jnp.int32, (rows, D_ATTN), 1)
    head_mask = (rid & (N_HEADS - 1)) == (lid >> HEAD_SHIFT)
    qbd = jnp.where(head_mask, qrep, 0.0).astype(BF16)

    pad = jnp.zeros((SUBLANES - t_len, D_ATTN), F32)
    kn = jnp.concatenate([kn_ref[0], pad], axis=0).astype(BF16)
    vn = jnp.concatenate([vn_ref[0], pad], axis=0).astype(BF16)
    kc = ck_ref[0].astype(BF16)
    vc = cv_ref[0].astype(BF16)
    nt = (((1,), (1,)), ((), ()))
    s_c = lax.dot_general(qbd, kc, nt, preferred_element_type=F32)
    s_n = lax.dot_general(qbd, kn, nt, preferred_element_type=F32)

    t_c = lax.broadcasted_iota(jnp.int32, (rows, w_buf), 0) >> HEADS_SHIFT
    d_c = w_buf + t_c - lax.broadcasted_iota(jnp.int32, (rows, w_buf), 1)
    t_n = lax.broadcasted_iota(jnp.int32, (rows, SUBLANES), 0) >> HEADS_SHIFT
    j_n = lax.broadcasted_iota(jnp.int32, (rows, SUBLANES), 1)
    d_n = t_n - j_n
    new_ok = (d_n >= 0) & (j_n < t_len)

    outs, lses = [], []
    for window, dil in DILATED_PATTERNS:
        ok_c = ((d_c & (dil - 1)) == 0) & (d_c <= window)
        ok_n = new_ok & ((d_n & (dil - 1)) == 0) & (d_n <= window)
        sc = jnp.where(ok_c, s_c, NEG_BIG)
        sn = jnp.where(ok_n, s_n, NEG_BIG)
        m = jnp.maximum(jnp.max(sc, axis=-1, keepdims=True), jnp.max(sn, axis=-1, keepdims=True))
        pc = jnp.exp(sc - m)
        pn = jnp.exp(sn - m)
        l = jnp.sum(pc, axis=-1, keepdims=True) + jnp.sum(pn, axis=-1, keepdims=True)
        o = (jnp.dot(pc.astype(BF16), vc, preferred_element_type=F32)
             + jnp.dot(pn.astype(BF16), vn, preferred_element_type=F32))
        outs.append(o * (1.0 / l))
        lses.append(m + jnp.log(l))
    m = jnp.maximum(jnp.maximum(lses[0], lses[1]), lses[2])
    es = [jnp.exp(x - m) for x in lses]
    o = (es[0] * outs[0] + es[1] * outs[1] + es[2] * outs[2]) * (1.0 / (es[0] + es[1] + es[2]))
    o = jnp.where(head_mask, o, 0.0).reshape(t_len, N_HEADS, D_ATTN)
    o_ref[0] = jnp.sum(o, axis=1)


def _attn_sample(q, k_new, v_new, cache_k, cache_v):
    b, t_len, _ = q.shape
    w_buf = cache_k.shape[1]
    assert w_buf >= MAX_WINDOW and t_len <= SUBLANES
    new = pl.BlockSpec((1, t_len, D_ATTN), lambda bi: (bi, 0, 0))
    cache = pl.BlockSpec((1, w_buf, D_ATTN), lambda bi: (bi, 0, 0))
    return pl.pallas_call(
        functools.partial(_attn_sample_kernel, t_len=t_len, w_buf=w_buf),
        grid=(b,),
        in_specs=[new, new, new, cache, cache],
        out_specs=new,
        out_shape=jax.ShapeDtypeStruct((b, t_len, D_ATTN), F32),
        compiler_params=pltpu.CompilerParams(
            dimension_semantics=("arbitrary",), vmem_limit_bytes=VMEM_LIMIT_BYTES),
        name="attn_sample",
    )(q, k_new, v_new, cache_k, cache_v)


def _mix_ffn_kernel(x_ref, yl_ref, ya_ref, ga_ref, wo_ref, gf_ref, wg_ref, wu_ref, wd_ref, gn_ref,
                    y_ref, *, final_norm):
    ya = ya_ref[...]
    ya_n = (ya * _rms_scale(ya) * ga_ref[...]).astype(BF16)
    mix = (jnp.dot(yl_ref[...], wo_ref[0:D_LRU, :], preferred_element_type=F32)
           + jnp.dot(ya_n, wo_ref[D_LRU:D_LRU + D_ATTN, :], preferred_element_type=F32))
    x1 = x_ref[...] + mix
    xn = (x1 * _rms_scale(x1) * gf_ref[...]).astype(BF16)
    hg = jnp.dot(xn, wg_ref[...], preferred_element_type=F32)
    hu = jnp.dot(xn, wu_ref[...], preferred_element_type=F32)
    act = (hg * _sigmoid(hg) * hu).astype(BF16)
    x2 = x1 + jnp.dot(act, wd_ref[...], preferred_element_type=F32)
    if final_norm:
        x2 = x2 * _rms_scale(x2) * gn_ref[...]
    y_ref[...] = x2


def _mix_ffn(x2d, yl, ya, pw, g_final, tm, final_norm):
    n = x2d.shape[0]
    row = lambda i: (i, 0)
    const = lambda i: (0, 0)
    once = pl.Buffered(1)
    return pl.pallas_call(
        functools.partial(_mix_ffn_kernel, final_norm=final_norm),
        grid=(n // tm,),
        in_specs=[
            pl.BlockSpec((tm, D_MODEL), row),
            pl.BlockSpec((tm, D_LRU), row),
            pl.BlockSpec((tm, D_ATTN), row),
            pl.BlockSpec((1, D_ATTN), const),
            pl.BlockSpec((D_LRU + D_ATTN, D_MODEL), const, pipeline_mode=once),
            pl.BlockSpec((1, D_MODEL), const),
            pl.BlockSpec((D_MODEL, D_FF), const, pipeline_mode=once),
            pl.BlockSpec((D_MODEL, D_FF), const, pipeline_mode=once),
            pl.BlockSpec((D_FF, D_MODEL), const, pipeline_mode=once),
            pl.BlockSpec((1, D_MODEL), const),
        ],
        out_specs=pl.BlockSpec((tm, D_MODEL), row),
        out_shape=jax.ShapeDtypeStruct((n, D_MODEL), F32),
        compiler_params=pltpu.CompilerParams(
            dimension_semantics=("arbitrary",), vmem_limit_bytes=VMEM_LIMIT_BYTES),
        name="mix_ffn",
    )(x2d, yl, ya, pw["g_attn"], pw["w_out"], pw["g_ffn"], pw["w_gate"], pw["w_up"], pw["w_down"],
      g_final)


def _block_diag_gate_weights(w_a, w_x):
    half_blocks = N_LRU_BLOCKS // 2
    half = D_LRU // 2

    def bd(w):
        eye = jnp.eye(half_blocks, dtype=w.dtype)
        return jnp.einsum("nij,nm->nimj", w, eye).reshape(half, half)

    halves = [jnp.concatenate([bd(w_a[c * half_blocks:(c + 1) * half_blocks]),
                               bd(w_x[c * half_blocks:(c + 1) * half_blocks])], axis=1)
              for c in range(2)]
    return jnp.stack(halves, axis=0).astype(BF16)


def _row(v):
    return v.reshape(1, -1).astype(F32)


def kernel(x_prompt, x_sample, state_conv, state_lru, cache_k, cache_v, norm_mix, w_in, conv_w, conv_b,
           lru_w_a, lru_b_a, lru_w_x, lru_b_x, lru_lambda, out_norm_lru, out_norm_attn, w_out, norm_ffn,
           w_gate, w_up, w_down, norm_final):
    depth = w_in.shape[0]
    bp, sp, _ = x_prompt.shape
    bs, ts, _ = x_sample.shape
    w_p = min(MAX_WINDOW, sp)
    assert ts >= CONV_WIDTH - 1
    xp = x_prompt.reshape(bp * sp, D_MODEL)
    xs = x_sample.reshape(bs * ts, D_MODEL)
    g_final = _row(norm_final)
    conv_p, lru_p, kw_p, vw_p = [], [], [], []
    conv_s, lru_s, kn_s, vn_s = [], [], [], []
    for l in range(depth):
        last = l == depth - 1
        w_in_bf = w_in[l].astype(BF16)
        g_mix = _row(norm_mix[l])
        lw = dict(
            conv_w=conv_w[l].astype(F32), conv_b=_row(conv_b[l]),
            w_gates=_block_diag_gate_weights(lru_w_a[l], lru_w_x[l]),
            b_a=_row(lru_b_a[l]), b_x=_row(lru_b_x[l]),
            neg_c_softplus=_row(-LRU_C * jax.nn.softplus(-lru_lambda[l].astype(F32))),
            g_lru=_row(out_norm_lru[l]))
        pw = dict(
            g_attn=_row(out_norm_attn[l]), w_out=w_out[l].astype(BF16), g_ffn=_row(norm_ffn[l]),
            w_gate=w_gate[l].astype(BF16), w_up=w_up[l].astype(BF16), w_down=w_down[l].astype(BF16))

        u, gate, q, k, v = _inproj(xp, g_mix, w_in_bf, tm=512)
        shp = lambda a: a.reshape(bp, sp, -1)
        yl, c_new, h_last = _lru_prompt(
            shp(u), shp(gate), jnp.zeros((bp, CONV_WIDTH - 1, D_LRU), F32),
            jnp.zeros((bp, 1, D_LRU), F32), lw, tt=1024)
        ya = _attn_prompt(shp(q), shp(k), shp(v))
        xp = _mix_ffn(xp, yl.reshape(bp * sp, D_LRU), ya.reshape(bp * sp, D_ATTN), pw, g_final,
                      tm=512, final_norm=last)
        conv_p.append(c_new)
        lru_p.append(h_last.reshape(bp, D_LRU))
        kw_p.append(shp(k)[:, sp - w_p:].reshape(bp, w_p, N_HEADS, HEAD_DIM))
        vw_p.append(shp(v)[:, sp - w_p:].reshape(bp, w_p, N_HEADS, HEAD_DIM))

        u, gate, q, k, v = _inproj(xs, g_mix, w_in_bf, tm=bs * ts)
        tmaj = lambda a: a.reshape(bs, ts, -1).transpose(1, 0, 2)
        yl_t, c_new_t, h_last = _lru_sample(
            tmaj(u), tmaj(gate), state_conv[l].astype(F32).transpose(1, 0, 2),
            state_lru[l].astype(F32), lw)
        w_buf = cache_k.shape[2]
        ya = _attn_sample(q.reshape(bs, ts, D_ATTN), k.reshape(bs, ts, D_ATTN), v.reshape(bs, ts, D_ATTN),
                          cache_k[l].reshape(bs, w_buf, D_ATTN), cache_v[l].reshape(bs, w_buf, D_ATTN))
        xs = _mix_ffn(xs, yl_t.transpose(1, 0, 2).reshape(bs * ts, D_LRU), ya.reshape(bs * ts, D_ATTN),
                      pw, g_final, tm=bs * ts, final_norm=last)
        conv_s.append(c_new_t.transpose(1, 0, 2))
        lru_s.append(h_last)
        kn_s.append(k.reshape(bs, ts, N_HEADS, HEAD_DIM))
        vn_s.append(v.reshape(bs, ts, N_HEADS, HEAD_DIM))
    return (xp.reshape(bp, sp, D_MODEL), xs.reshape(bs, ts, D_MODEL),
            jnp.stack(conv_p, 0), jnp.stack(lru_p, 0), jnp.stack(kw_p, 0), jnp.stack(vw_p, 0),
            jnp.stack(conv_s, 0), jnp.stack(lru_s, 0), jnp.stack(kn_s, 0), jnp.stack(vn_s, 0))
```

```python
import functools
import math

import jax
import jax.numpy as jnp
from jax import lax
from jax.experimental import pallas as pl
from jax.experimental.pallas import tpu as pltpu

F32 = jnp.float32
BF16 = jnp.bfloat16

D_MODEL = 1024
D_LRU = 512
D_ATTN = 512
N_HEADS = 8
HEAD_DIM = 64
N_LRU_BLOCKS = 8
LRU_BLOCK = D_LRU // N_LRU_BLOCKS
CONV_WIDTH = 4
LRU_C = 8.0
DILATED_PATTERNS = ((128, 1), (512, 4), (2048, 16))
MAX_WINDOW = 2048
D_FF = 2816
D_IN = 2 * D_LRU + 3 * D_ATTN
RMS_EPS = 1e-6
Q_SCALE = HEAD_DIM ** -0.5

LANES = 128
SUBLANES = 8
VMEM_LIMIT_BYTES = 56 * 1024 * 1024

KEYS_PER_PATTERN = 129
QBLK = 128
SUPER = 2048
HEADS_PER_STEP = LANES // HEAD_DIM
NEG_BIG = -1e30
LOG2_E = math.log2(math.e)
HEAD_SHIFT = HEAD_DIM.bit_length() - 1
HEADS_SHIFT = N_HEADS.bit_length() - 1
assert all(d & (d - 1) == 0 for _, d in DILATED_PATTERNS) and 1 << HEAD_SHIFT == HEAD_DIM
assert all(w // d + 1 == KEYS_PER_PATTERN for w, d in DILATED_PATTERNS)


def _rms_scale(x):
    return lax.rsqrt(jnp.mean(x * x, axis=-1, keepdims=True) + RMS_EPS)


def _inproj_kernel(x_ref, g_ref, w_ref, u_ref, gate_ref, q_ref, k_ref, v_ref):
    x = x_ref[...]
    xn = (x * _rms_scale(x) * g_ref[...]).astype(BF16)
    proj = jnp.dot(xn, w_ref[...], preferred_element_type=F32)
    u_ref[...] = proj[:, 0:D_LRU]
    gate_ref[...] = proj[:, D_LRU:2 * D_LRU]
    o = 2 * D_LRU
    q_ref[...] = proj[:, o:o + D_ATTN] * Q_SCALE
    k_ref[...] = proj[:, o + D_ATTN:o + 2 * D_ATTN]
    v_ref[...] = proj[:, o + 2 * D_ATTN:o + 3 * D_ATTN]


def _inproj(x2d, g, w_bf, tm):
    n = x2d.shape[0]
    const = lambda i: (0, 0)
    row = lambda i: (i, 0)
    out_sds = jax.ShapeDtypeStruct((n, D_LRU), F32)
    return pl.pallas_call(
        _inproj_kernel,
        grid=(n // tm,),
        in_specs=[
            pl.BlockSpec((tm, D_MODEL), row),
            pl.BlockSpec((1, D_MODEL), const),
            pl.BlockSpec((D_MODEL, D_IN), const, pipeline_mode=pl.Buffered(1)),
        ],
        out_specs=[pl.BlockSpec((tm, D_LRU), row)] * 5,
        out_shape=[out_sds] * 5,
        compiler_params=pltpu.CompilerParams(
            dimension_semantics=("arbitrary",), vmem_limit_bytes=VMEM_LIMIT_BYTES),
        name="inproj",
    )(x2d, g, w_bf)


def _gelu_tanh(x):
    c = math.sqrt(2.0 / math.pi)
    return 0.5 * x * (1.0 + jnp.tanh(c * (x + 0.044715 * (x * x * x))))


def _sigmoid(x):
    return jax.nn.sigmoid(x)


def _lru_gates(uc, wg_ref, ba, bx_bias, neg_c_softplus):
    half = D_LRU // 2
    ucb = uc.astype(BF16)
    pre0 = jnp.dot(ucb[:, 0:half], wg_ref[0], preferred_element_type=F32)
    pre1 = jnp.dot(ucb[:, half:D_LRU], wg_ref[1], preferred_element_type=F32)
    r_pre = jnp.concatenate([pre0[:, 0:half], pre1[:, 0:half]], axis=1) + ba
    i_pre = jnp.concatenate([pre0[:, half:], pre1[:, half:]], axis=1) + bx_bias
    r = _sigmoid(r_pre)
    i = _sigmoid(i_pre)
    log_a = r * neg_c_softplus
    a = jnp.exp(log_a)
    b = jnp.sqrt(-jnp.tanh(log_a) * (a * a + 1.0)) * (i * uc)
    return a, b


def _shift_rows(x, s, fill):
    t = x.shape[0]
    if s % SUBLANES == 0:
        return jnp.concatenate([jnp.full((s, x.shape[1]), fill, x.dtype), x[0:t - s]], axis=0)
    rolled = pltpu.roll(x, s, axis=0)
    row = lax.broadcasted_iota(jnp.int32, x.shape, 0)
    return jnp.where(row < s, fill, rolled)


def _scan_rows(a, b):
    t = a.shape[0]
    s = 1
    while s < t:
        b = a * _shift_rows(b, s, 0.0) + b
        if 2 * s < t:
            a = a * _shift_rows(a, s, 1.0)
        s *= 2
    return b


LRU_CHUNK = 256


def _lru_prompt_kernel(u_ref, gate_ref, cprev_ref, h0_ref, cw_ref, cb_ref, wg_ref, ba_ref, bx_ref,
                       ncs_ref, gl_ref, y_ref, conv_ref, hlast_ref, tail_ref, h_ref, *, n_chunks):
    ti = pl.program_id(1)
    n_t = pl.num_programs(1)

    @pl.when(ti == 0)
    def _():
        tail_ref[...] = jnp.zeros_like(tail_ref)
        tail_ref[SUBLANES - (CONV_WIDTH - 1):SUBLANES, :] = cprev_ref[0]
        h_ref[...] = jnp.broadcast_to(h0_ref[0], h_ref.shape)

    cw = cw_ref[...]
    cb = cb_ref[...]
    ba = ba_ref[...]
    bxb = bx_ref[...]
    ncs = ncs_ref[...]
    gl = gl_ref[...]
    row8 = lax.broadcasted_iota(jnp.int32, (SUBLANES, D_LRU), 0)
    row_c = lax.broadcasted_iota(jnp.int32, (LRU_CHUNK, D_LRU), 0)

    def chunk(ci, carry):
        r0 = pl.multiple_of(ci * LRU_CHUNK, LRU_CHUNK)
        u = u_ref[0, pl.ds(r0, LRU_CHUNK), :]
        tail = tail_ref[...]
        uc = cb + cw[CONV_WIDTH - 1:CONV_WIDTH, :] * u
        for s in range(1, CONV_WIDTH):
            ru = pltpu.roll(u, s, axis=0)
            rt = pltpu.roll(tail, s, axis=0)
            first = jnp.where(row8 < s, rt, ru[0:SUBLANES])
            us = jnp.concatenate([first, ru[SUBLANES:]], axis=0)
            uc = uc + cw[CONV_WIDTH - 1 - s:CONV_WIDTH - s, :] * us
        tail_ref[...] = u[LRU_CHUNK - SUBLANES:LRU_CHUNK]
        a, b = _lru_gates(uc, wg_ref, ba, bxb, ncs)
        h_prev = h_ref[0:1, :]
        b = jnp.where(row_c == 0, b + a * h_prev, b)
        h = _scan_rows(a, b)
        h_ref[...] = jnp.broadcast_to(h[LRU_CHUNK - 1:LRU_CHUNK], h_ref.shape)
        y = h * _gelu_tanh(gate_ref[0, pl.ds(r0, LRU_CHUNK), :])
        y_ref[0, pl.ds(r0, LRU_CHUNK), :] = (y * _rms_scale(y) * gl).astype(BF16)
        return carry

    lax.fori_loop(0, n_chunks, chunk, 0)

    @pl.when(ti == n_t - 1)
    def _():
        conv_ref[0] = tail_ref[SUBLANES - (CONV_WIDTH - 1):SUBLANES, :]
        hlast_ref[0] = h_ref[0:1, :]


def _lru_prompt(u, gate, conv_prev, h0, lw, tt):
    b, s, _ = u.shape
    assert s % tt == 0 and tt % LRU_CHUNK == 0
    seq = lambda bi, ti: (bi, ti, 0)
    per_b = lambda bi, ti: (bi, 0, 0)
    c2 = lambda bi, ti: (0, 0)
    c3 = lambda bi, ti: (0, 0, 0)
    vec = pl.BlockSpec((1, D_LRU), c2)
    return pl.pallas_call(
        functools.partial(_lru_prompt_kernel, n_chunks=tt // LRU_CHUNK),
        grid=(b, s // tt),
        in_specs=[
            pl.BlockSpec((1, tt, D_LRU), seq),
            pl.BlockSpec((1, tt, D_LRU), seq),
            pl.BlockSpec((1, CONV_WIDTH - 1, D_LRU), per_b),
            pl.BlockSpec((1, 1, D_LRU), per_b),
            pl.BlockSpec((CONV_WIDTH, D_LRU), c2),
            vec,
            pl.BlockSpec((2, D_LRU // 2, D_LRU), c3),
            vec, vec, vec, vec,
        ],
        out_specs=[
            pl.BlockSpec((1, tt, D_LRU), seq),
            pl.BlockSpec((1, CONV_WIDTH - 1, D_LRU), per_b),
            pl.BlockSpec((1, 1, D_LRU), per_b),
        ],
        out_shape=[
            jax.ShapeDtypeStruct((b, s, D_LRU), BF16),
            jax.ShapeDtypeStruct((b, CONV_WIDTH - 1, D_LRU), F32),
            jax.ShapeDtypeStruct((b, 1, D_LRU), F32),
        ],
        scratch_shapes=[pltpu.VMEM((SUBLANES, D_LRU), F32), pltpu.VMEM((SUBLANES, D_LRU), F32)],
        compiler_params=pltpu.CompilerParams(
            dimension_semantics=("arbitrary", "arbitrary"), vmem_limit_bytes=VMEM_LIMIT_BYTES),
        name="lru_prompt",
    )(u, gate, conv_prev, h0, lw["conv_w"], lw["conv_b"], lw["w_gates"], lw["b_a"], lw["b_x"],
      lw["neg_c_softplus"], lw["g_lru"])


def _lru_sample_kernel(u_ref, gate_ref, cprev_ref, h0_ref, cw_ref, cb_ref, wg_ref, ba_ref, bx_ref,
                       ncs_ref, gl_ref, y_ref, conv_ref, hlast_ref):
    t_len, nb, _ = u_ref.shape
    cw = cw_ref[...]
    hist = [cprev_ref[j] for j in range(CONV_WIDTH - 1)] + [u_ref[t] for t in range(t_len)]
    ucs = []
    for t in range(t_len):
        uc = cb_ref[...]
        for j in range(CONV_WIDTH):
            uc = uc + cw[j:j + 1, :] * hist[t + j]
        ucs.append(uc)
    uc_all = jnp.concatenate(ucs, axis=0)
    a, b = _lru_gates(uc_all, wg_ref, ba_ref[...], bx_ref[...], ncs_ref[...])
    h = h0_ref[...]
    gl = gl_ref[...]
    for t in range(t_len):
        h = a[t * nb:(t + 1) * nb] * h + b[t * nb:(t + 1) * nb]
        y = h * _gelu_tanh(gate_ref[t])
        y_ref[t] = (y * _rms_scale(y) * gl).astype(BF16)
    hlast_ref[...] = h
    for j in range(CONV_WIDTH - 1):
        conv_ref[j] = hist[t_len + j]


def _lru_sample(u_t, gate_t, conv_prev_t, h0, lw):
    t_len, nb, _ = u_t.shape
    return pl.pallas_call(
        _lru_sample_kernel,
        out_shape=[
            jax.ShapeDtypeStruct((t_len, nb, D_LRU), BF16),
            jax.ShapeDtypeStruct((CONV_WIDTH - 1, nb, D_LRU), F32),
            jax.ShapeDtypeStruct((nb, D_LRU), F32),
        ],
        name="lru_sample",
    )(u_t, gate_t, conv_prev_t, h0, lw["conv_w"], lw["conv_b"], lw["w_gates"], lw["b_a"], lw["b_x"],
      lw["neg_c_softplus"], lw["g_lru"])


def _attn_prompt_kernel(q_ref, k_ref, v_ref, o_ref,
                        qp1, qp4, qp16, kb1, kb4, kb16, vb1, vb4, vb16, res_a, res_l, res_m, bias_ref,
                        stage_ref):
    sb = pl.program_id(2)
    qps = (qp1, qp4, qp16)
    kbs = (kb1, kb4, kb16)
    vbs = (vb1, vb4, vb16)
    dils = tuple(d for _, d in DILATED_PATTERNS)
    assert dils == (1, 4, 16)
    lane = lax.broadcasted_iota(jnp.int32, (QBLK, LANES), 1)
    head0 = lane < HEAD_DIM

    qi = lax.broadcasted_iota(jnp.int32, (QBLK, 2 * QBLK), 0)
    ki = lax.broadcasted_iota(jnp.int32, (QBLK, 2 * QBLK), 1)
    delta = qi + QBLK - ki
    band = (delta >= 0) & (delta <= KEYS_PER_PATTERN - 1)
    bias_ref[0] = jnp.where(band, 0.0, NEG_BIG)
    bias_ref[1] = jnp.where(band & (ki >= QBLK), 0.0, NEG_BIG)

    for kb, vb, d in zip(kbs, vbs, dils):
        cls_len = SUPER // d

        @pl.when(sb == 0)
        def _():
            kb[:, 0:QBLK, :] = jnp.zeros((d, QBLK, LANES), BF16)
            vb[:, :, 0:QBLK, 0:LANES] = jnp.zeros((HEADS_PER_STEP, d, QBLK, LANES), BF16)
            h0_all = lax.broadcasted_iota(jnp.int32, (d, QBLK + cls_len, LANES), 2) < HEAD_DIM
            vb[0, :, :, LANES:2 * LANES] = jnp.where(h0_all, 1.0, 0.0).astype(BF16)
            vb[1, :, :, LANES:2 * LANES] = jnp.where(h0_all, 0.0, 1.0).astype(BF16)

        @pl.when(sb > 0)
        def _():
            kb[:, 0:QBLK, :] = kb[:, cls_len:cls_len + QBLK, :]
            vb[:, :, 0:QBLK, 0:LANES] = vb[:, :, cls_len:cls_len + QBLK, 0:LANES]

    def put(x, p, c, which):
        n_rows = x.shape[0]
        if which == "q":
            qps[p][c] = (x * LOG2_E).astype(BF16)
        elif which == "k":
            kbs[p][c, QBLK:QBLK + n_rows, :] = x.astype(BF16)
        else:
            h0 = lax.broadcasted_iota(jnp.int32, x.shape, 1) < HEAD_DIM
            h0 = lax.broadcasted_iota(jnp.int32, x.shape, 1) < HEAD_DIM
            rows = slice(QBLK, QBLK + n_rows)
            vbs[p][0, c, rows, 0:LANES] = jnp.where(h0, x, 0.0).astype(BF16)
            vbs[p][1, c, rows, 0:LANES] = jnp.where(h0, 0.0, x).astype(BF16)

    for which, ref in (("q", q_ref), ("k", k_ref), ("v", v_ref)):
        put(ref[0], 0, 0, which)
        for c in range(4):
            x = ref[0, pl.ds(c, SUPER // 4, stride=4), :]
            stage_ref[c] = x
            put(x, 1, c, which)
        for r in range(16):
            put(stage_ref[r % 4, pl.ds(r // 4, QBLK, stride=4), :], 2, r, which)

    for p, (qp, kb, vb, d) in enumerate(zip(qps, kbs, vbs, dils)):
        blocks_per_class = SUPER // d // QBLK

        def unit(n, carry, p=p, qp=qp, kb=kb, vb=vb, d=d, blocks_per_class=blocks_per_class):
            c = n // blocks_per_class
            i = n % blocks_per_class
            r0 = pl.multiple_of(i * QBLK, QBLK)
            q_t = qp[c, pl.ds(r0, QBLK), :]
            k_t = kb[c, pl.ds(r0, 2 * QBLK), :]
            first = jnp.logical_and(sb == 0, i == 0).astype(jnp.int32)
            bias = bias_ref[first]
            acc = None
            maxes = []
            for h in range(HEADS_PER_STEP):
                hm = head0 if h == 0 else jnp.logical_not(head0)
                qh = jnp.where(hm, q_t, jnp.zeros_like(q_t))
                s = lax.dot_general(qh, k_t, (((1,), (1,)), ((), ())), preferred_element_type=F32)
                s = s + bias
                m = jnp.max(s, axis=-1, keepdims=True)
                pr = jnp.exp2(s - m).astype(BF16)
                v_t = vb[h, c, pl.ds(r0, 2 * QBLK), :]
                pv = jnp.dot(pr, v_t, preferred_element_type=F32)
                acc = pv if acc is None else acc + pv
                maxes.append(jnp.broadcast_to(m, (QBLK, LANES)))
            start = c + d * r0
            if d == 1:
                dst = pl.ds(pl.multiple_of(start, QBLK), QBLK)
            else:
                dst = pl.ds(start, QBLK, stride=d)
            res_a[p, dst, :] = acc[:, 0:LANES]
            res_l[p, dst, :] = acc[:, LANES:2 * LANES]
            res_m[p, dst, :] = jnp.where(head0, maxes[0], maxes[1])
            return carry

        lax.fori_loop(0, SUPER // QBLK, unit, 0, unroll=16)

    def merge(ci, carry):
        rows = pl.ds(pl.multiple_of(ci * 256, 256), 256)
        ms = [res_m[p, rows, :] for p in range(3)]
        m = jnp.maximum(jnp.maximum(ms[0], ms[1]), ms[2])
        ws = [jnp.exp2(x - m) for x in ms]
        den = ws[0] * res_l[0, rows, :] + ws[1] * res_l[1, rows, :] + ws[2] * res_l[2, rows, :]
        num = ws[0] * res_a[0, rows, :] + ws[1] * res_a[1, rows, :] + ws[2] * res_a[2, rows, :]
        o_ref[0, rows, :] = num * (1.0 / den)
        return carry

    lax.fori_loop(0, SUPER // 256, merge, 0)


def _attn_prompt(q, k, v):
    b, s, _ = q.shape
    assert s % SUPER == 0
    blk = pl.BlockSpec((1, SUPER, LANES), lambda bi, hp, sb: (bi, sb, hp))
    scratch = []
    for _, d in DILATED_PATTERNS:
        scratch.append(pltpu.VMEM((d, SUPER // d, LANES), BF16))
    for _, d in DILATED_PATTERNS:
        scratch.append(pltpu.VMEM((d, QBLK + SUPER // d, LANES), BF16))
    for _, d in DILATED_PATTERNS:
        scratch.append(pltpu.VMEM((HEADS_PER_STEP, d, QBLK + SUPER // d, 2 * LANES), BF16))
    scratch += [pltpu.VMEM((3, SUPER, LANES), F32)] * 3
    scratch += [pltpu.VMEM((2, QBLK, 2 * QBLK), F32), pltpu.VMEM((4, SUPER // 4, LANES), F32)]
    return pl.pallas_call(
        _attn_prompt_kernel,
        grid=(b, D_ATTN // LANES, s // SUPER),
        in_specs=[blk, blk, blk],
        out_specs=blk,
        out_shape=jax.ShapeDtypeStruct((b, s, D_ATTN), F32),
        scratch_shapes=scratch,
        compiler_params=pltpu.CompilerParams(
            dimension_semantics=("arbitrary", "arbitrary", "arbitrary"),
            vmem_limit_bytes=VMEM_LIMIT_BYTES),
        name="attn_prompt",
    )(q, k, v)


def _attn_sample_kernel(q_ref, kn_ref, vn_ref, ck_ref, cv_ref, o_ref, *, t_len, w_buf):
    rows = t_len * N_HEADS
    q = q_ref[0]
    qrep = jnp.broadcast_to(q[:, None, :], (t_len, N_HEADS, D_ATTN)).reshape(rows, D_ATTN)
    rid = lax.broadcasted_iota(jnp.int32, (rows, D_ATTN), 0)
    lid = lax.broadcasted_iota(jnp.int32, (rows, D_ATTN), 1)
    head_mask = (rid & (N_HEADS - 1)) == (lid >> HEAD_SHIFT)
    qbd = jnp.where(head_mask, qrep, 0.0).astype(BF16)

    pad = jnp.zeros((SUBLANES - t_len, D_ATTN), F32)
    kn = jnp.concatenate([kn_ref[0], pad], axis=0).astype(BF16)
    vn = jnp.concatenate([vn_ref[0], pad], axis=0).astype(BF16)
    kc = ck_ref[0].astype(BF16)
    vc = cv_ref[0].astype(BF16)
    nt = (((1,), (1,)), ((), ()))
    s_c = lax.dot_general(qbd, kc, nt, preferred_element_type=F32)
    s_n = lax.dot_general(qbd, kn, nt, preferred_element_type=F32)

    t_c = lax.broadcasted_iota(jnp.int32, (rows, w_buf), 0) >> HEADS_SHIFT
    d_c = w_buf + t_c - lax.broadcasted_iota(jnp.int32, (rows, w_buf), 1)
    t_n = lax.broadcasted_iota(jnp.int32, (rows, SUBLANES), 0) >> HEADS_SHIFT
    j_n = lax.broadcasted_iota(jnp.int32, (rows, SUBLANES), 1)
    d_n = t_n - j_n
    new_ok = (d_n >= 0) & (j_n < t_len)

    outs, lses = [], []
    for window, dil in DILATED_PATTERNS:
        ok_c = ((d_c & (dil - 1)) == 0) & (d_c <= window)
        ok_n = new_ok & ((d_n & (dil - 1)) == 0) & (d_n <= window)
        sc = jnp.where(ok_c, s_c, NEG_BIG)
        sn = jnp.where(ok_n, s_n, NEG_BIG)
        m = jnp.maximum(jnp.max(sc, axis=-1, keepdims=True), jnp.max(sn, axis=-1, keepdims=True))
        pc = jnp.exp(sc - m)
        pn = jnp.exp(sn - m)
        l = jnp.sum(pc, axis=-1, keepdims=True) + jnp.sum(pn, axis=-1, keepdims=True)
        o = (jnp.dot(pc.astype(BF16), vc, preferred_element_type=F32)
             + jnp.dot(pn.astype(BF16), vn, preferred_element_type=F32))
        outs.append(o * (1.0 / l))
        lses.append(m + jnp.log(l))
    m = jnp.maximum(jnp.maximum(lses[0], lses[1]), lses[2])
    es = [jnp.exp(x - m) for x in lses]
    o = (es[0] * outs[0] + es[1] * outs[1] + es[2] * outs[2]) * (1.0 / (es[0] + es[1] + es[2]))
    o = jnp.where(head_mask, o, 0.0).reshape(t_len, N_HEADS, D_ATTN)
    o_ref[0] = jnp.sum(o, axis=1)


def _attn_sample(q, k_new, v_new, cache_k, cache_v):
    b, t_len, _ = q.shape
    w_buf = cache_k.shape[1]
    assert w_buf >= MAX_WINDOW and t_len <= SUBLANES
    new = pl.BlockSpec((1, t_len, D_ATTN), lambda bi: (bi, 0, 0))
    cache = pl.BlockSpec((1, w_buf, D_ATTN), lambda bi: (bi, 0, 0))
    return pl.pallas_call(
        functools.partial(_attn_sample_kernel, t_len=t_len, w_buf=w_buf),
        grid=(b,),
        in_specs=[new, new, new, cache, cache],
        out_specs=new,
        out_shape=jax.ShapeDtypeStruct((b, t_len, D_ATTN), F32),
        compiler_params=pltpu.CompilerParams(
            dimension_semantics=("arbitrary",), vmem_limit_bytes=VMEM_LIMIT_BYTES),
        name="attn_sample",
    )(q, k_new, v_new, cache_k, cache_v)


def _mix_ffn_kernel(x_ref, yl_ref, ya_ref, ga_ref, wo_ref, gf_ref, wg_ref, wu_ref, wd_ref, gn_ref,
                    y_ref, *, final_norm):
    ya = ya_ref[...]
    ya_n = (ya * _rms_scale(ya) * ga_ref[...]).astype(BF16)
    mix = (jnp.dot(yl_ref[...], wo_ref[0:D_LRU, :], preferred_element_type=F32)
           + jnp.dot(ya_n, wo_ref[D_LRU:D_LRU + D_ATTN, :], preferred_element_type=F32))
    x1 = x_ref[...] + mix
    xn = (x1 * _rms_scale(x1) * gf_ref[...]).astype(BF16)
    hg = jnp.dot(xn, wg_ref[...], preferred_element_type=F32)
    hu = jnp.dot(xn, wu_ref[...], preferred_element_type=F32)
    act = (hg * _sigmoid(hg) * hu).astype(BF16)
    x2 = x1 + jnp.dot(act, wd_ref[...], preferred_element_type=F32)
    if final_norm:
        x2 = x2 * _rms_scale(x2) * gn_ref[...]
    y_ref[...] = x2


def _mix_ffn(x2d, yl, ya, pw, g_final, tm, final_norm):
    n = x2d.shape[0]
    row = lambda i: (i, 0)
    const = lambda i: (0, 0)
    once = pl.Buffered(1)
    return pl.pallas_call(
        functools.partial(_mix_ffn_kernel, final_norm=final_norm),
        grid=(n // tm,),
        in_specs=[
            pl.BlockSpec((tm, D_MODEL), row),
            pl.BlockSpec((tm, D_LRU), row),
            pl.BlockSpec((tm, D_ATTN), row),
            pl.BlockSpec((1, D_ATTN), const),
            pl.BlockSpec((D_LRU + D_ATTN, D_MODEL), const, pipeline_mode=once),
            pl.BlockSpec((1, D_MODEL), const),
            pl.BlockSpec((D_MODEL, D_FF), const, pipeline_mode=once),
            pl.BlockSpec((D_MODEL, D_FF), const, pipeline_mode=once),
            pl.BlockSpec((D_FF, D_MODEL), const, pipeline_mode=once),
            pl.BlockSpec((1, D_MODEL), const),
        ],
        out_specs=pl.BlockSpec((tm, D_MODEL), row),
        out_shape=jax.ShapeDtypeStruct((n, D_MODEL), F32),
        compiler_params=pltpu.CompilerParams(
            dimension_semantics=("arbitrary",), vmem_limit_bytes=VMEM_LIMIT_BYTES),
        name="mix_ffn",
    )(x2d, yl, ya, pw["g_attn"], pw["w_out"], pw["g_ffn"], pw["w_gate"], pw["w_up"], pw["w_down"],
      g_final)


def _block_diag_gate_weights(w_a, w_x):
    half_blocks = N_LRU_BLOCKS // 2
    half = D_LRU // 2

    def bd(w):
        eye = jnp.eye(half_blocks, dtype=w.dtype)
        return jnp.einsum("nij,nm->nimj", w, eye).reshape(half, half)

    halves = [jnp.concatenate([bd(w_a[c * half_blocks:(c + 1) * half_blocks]),
                               bd(w_x[c * half_blocks:(c + 1) * half_blocks])], axis=1)
              for c in range(2)]
    return jnp.stack(halves, axis=0).astype(BF16)


def _row(v):
    return v.reshape(1, -1).astype(F32)


def kernel(x_prompt, x_sample, state_conv, state_lru, cache_k, cache_v, norm_mix, w_in, conv_w, conv_b,
           lru_w_a, lru_b_a, lru_w_x, lru_b_x, lru_lambda, out_norm_lru, out_norm_attn, w_out, norm_ffn,
           w_gate, w_up, w_down, norm_final):
    depth = w_in.shape[0]
    bp, sp, _ = x_prompt.shape
    bs, ts, _ = x_sample.shape
    w_p = min(MAX_WINDOW, sp)
    assert ts >= CONV_WIDTH - 1
    xp = x_prompt.reshape(bp * sp, D_MODEL)
    xs = x_sample.reshape(bs * ts, D_MODEL)
    g_final = _row(norm_final)
    conv_p, lru_p, kw_p, vw_p = [], [], [], []
    conv_s, lru_s, kn_s, vn_s = [], [], [], []
    for l in range(depth):
        last = l == depth - 1
        w_in_bf = w_in[l].astype(BF16)
        g_mix = _row(norm_mix[l])
        lw = dict(
            conv_w=conv_w[l].astype(F32), conv_b=_row(conv_b[l]),
            w_gates=_block_diag_gate_weights(lru_w_a[l], lru_w_x[l]),
            b_a=_row(lru_b_a[l]), b_x=_row(lru_b_x[l]),
            neg_c_softplus=_row(-LRU_C * jax.nn.softplus(-lru_lambda[l].astype(F32))),
            g_lru=_row(out_norm_lru[l]))
        pw = dict(
            g_attn=_row(out_norm_attn[l]), w_out=w_out[l].astype(BF16), g_ffn=_row(norm_ffn[l]),
            w_gate=w_gate[l].astype(BF16), w_up=w_up[l].astype(BF16), w_down=w_down[l].astype(BF16))

        u, gate, q, k, v = _inproj(xp, g_mix, w_in_bf, tm=512)
        shp = lambda a: a.reshape(bp, sp, -1)
        yl, c_new, h_last = _lru_prompt(
            shp(u), shp(gate), jnp.zeros((bp, CONV_WIDTH - 1, D_LRU), F32),
            jnp.zeros((bp, 1, D_LRU), F32), lw, tt=1024)
        ya = _attn_prompt(shp(q), shp(k), shp(v))
        xp = _mix_ffn(xp, yl.reshape(bp * sp, D_LRU), ya.reshape(bp * sp, D_ATTN), pw, g_final,
                      tm=512, final_norm=last)
        conv_p.append(c_new)
        lru_p.append(h_last.reshape(bp, D_LRU))
        kw_p.append(shp(k)[:, sp - w_p:].reshape(bp, w_p, N_HEADS, HEAD_DIM))
        vw_p.append(shp(v)[:, sp - w_p:].reshape(bp, w_p, N_HEADS, HEAD_DIM))

        u, gate, q, k, v = _inproj(xs, g_mix, w_in_bf, tm=bs * ts)
        tmaj = lambda a: a.reshape(bs, ts, -1).transpose(1, 0, 2)
        yl_t, c_new_t, h_last = _lru_sample(
            tmaj(u), tmaj(gate), state_conv[l].astype(F32).transpose(1, 0, 2),
            state_lru[l].astype(F32), lw)
        w_buf = cache_k.shape[2]
        ya = _attn_sample(q.reshape(bs, ts, D_ATTN), k.reshape(bs, ts, D_ATTN), v.reshape(bs, ts, D_ATTN),
                          cache_k[l].reshape(bs, w_buf, D_ATTN), cache_v[l].reshape(bs, w_buf, D_ATTN))
        xs = _mix_ffn(xs, yl_t.transpose(1, 0, 2).reshape(bs * ts, D_LRU), ya.reshape(bs * ts, D_ATTN),
                      pw, g_final, tm=bs * ts, final_norm=last)
        conv_s.append(c_new_t.transpose(1, 0, 2))
        lru_s.append(h_last)
        kn_s.append(k.reshape(bs, ts, N_HEADS, HEAD_DIM))
        vn_s.append(v.reshape(bs, ts, N_HEADS, HEAD_DIM))
    return (xp.reshape(bp, sp, D_MODEL), xs.reshape(bs, ts, D_MODEL),
            jnp.stack(conv_p, 0), jnp.stack(lru_p, 0), jnp.stack(kw_p, 0), jnp.stack(vw_p, 0),
            jnp.stack(conv_s, 0), jnp.stack(lru_s, 0), jnp.stack(kn_s, 0), jnp.stack(vn_s, 0))
```

```python
import functools
import math

import jax
import jax.numpy as jnp
from jax import lax
from jax.experimental import pallas as pl
from jax.experimental.pallas import tpu as pltpu

F32 = jnp.float32
BF16 = jnp.bfloat16

D_MODEL = 1024
D_LRU = 512
D_ATTN = 512
N_HEADS = 8
HEAD_DIM = 64
N_LRU_BLOCKS = 8
LRU_BLOCK = D_LRU // N_LRU_BLOCKS
CONV_WIDTH = 4
LRU_C = 8.0
DILATED_PATTERNS = ((128, 1), (512, 4), (2048, 16))
MAX_WINDOW = 2048
D_FF = 2816
D_IN = 2 * D_LRU + 3 * D_ATTN
RMS_EPS = 1e-6
Q_SCALE = HEAD_DIM ** -0.5

LANES = 128
SUBLANES = 8
VMEM_LIMIT_BYTES = 56 * 1024 * 1024

KEYS_PER_PATTERN = 129
QBLK = 128
SUPER = 2048
HEADS_PER_STEP = LANES // HEAD_DIM
NEG_BIG = -1e30
LOG2_E = math.log2(math.e)
LRU_CHUNK = 256
SEG_LEN = LRU_CHUNK // SUBLANES
HEAD_SHIFT = HEAD_DIM.bit_length() - 1
HEADS_SHIFT = N_HEADS.bit_length() - 1
assert all(d & (d - 1) == 0 for _, d in DILATED_PATTERNS) and 1 << HEAD_SHIFT == HEAD_DIM
assert all(w // d + 1 == KEYS_PER_PATTERN for w, d in DILATED_PATTERNS)


def _rms_scale(x):
    return lax.rsqrt(jnp.mean(x * x, axis=-1, keepdims=True) + RMS_EPS)


def _inproj_kernel(x_ref, g_ref, w_ref, u_ref, gate_ref, q_ref, k_ref, v_ref):
    x = x_ref[...]
    xn = (x * _rms_scale(x) * g_ref[...]).astype(BF16)
    proj = jnp.dot(xn, w_ref[...], preferred_element_type=F32)
    u_ref[...] = proj[:, 0:D_LRU]
    gate_ref[...] = proj[:, D_LRU:2 * D_LRU]
    o = 2 * D_LRU
    q_ref[...] = proj[:, o:o + D_ATTN] * Q_SCALE
    k_ref[...] = proj[:, o + D_ATTN:o + 2 * D_ATTN]
    v_ref[...] = proj[:, o + 2 * D_ATTN:o + 3 * D_ATTN]


def _inproj(x2d, g, w_bf, tm):
    n = x2d.shape[0]
    assert n % tm == 0
    const = lambda i: (0, 0)
    row = lambda i: (i, 0)
    out_sds = jax.ShapeDtypeStruct((n, D_LRU), F32)
    return pl.pallas_call(
        _inproj_kernel,
        grid=(n // tm,),
        in_specs=[
            pl.BlockSpec((tm, D_MODEL), row),
            pl.BlockSpec((1, D_MODEL), const),
            pl.BlockSpec((D_MODEL, D_IN), const, pipeline_mode=pl.Buffered(1)),
        ],
        out_specs=[pl.BlockSpec((tm, D_LRU), row)] * 5,
        out_shape=[out_sds] * 5,
        compiler_params=pltpu.CompilerParams(
            dimension_semantics=("arbitrary",), vmem_limit_bytes=VMEM_LIMIT_BYTES),
        name="inproj",
    )(x2d, g, w_bf)


def _gelu_tanh(x):
    c = math.sqrt(2.0 / math.pi)
    return 0.5 * x * (1.0 + jnp.tanh(c * (x + 0.044715 * (x * x * x))))


def _sigmoid(x):
    return jax.nn.sigmoid(x)


def _lru_gates(uc, wg_ref, ba, bx_bias, neg_c_softplus):
    half = D_LRU // 2
    ucb = uc.astype(BF16)
    pre0 = jnp.dot(ucb[:, 0:half], wg_ref[0], preferred_element_type=F32)
    pre1 = jnp.dot(ucb[:, half:D_LRU], wg_ref[1], preferred_element_type=F32)
    r_pre = jnp.concatenate([pre0[:, 0:half], pre1[:, 0:half]], axis=1) + ba
    i_pre = jnp.concatenate([pre0[:, half:], pre1[:, half:]], axis=1) + bx_bias
    r = _sigmoid(r_pre)
    i = _sigmoid(i_pre)
    log_a = r * neg_c_softplus
    a = jnp.exp(log_a)
    z = -jnp.tanh(log_a) * (a * a + 1.0)
    root = jnp.where(z > 0.0, z * lax.rsqrt(z), 0.0)
    b = root * (i * uc)
    return a, b


def _shift_rows(x, s, fill):
    t = x.shape[0]
    if s % SUBLANES == 0:
        return jnp.concatenate([jnp.full((s, x.shape[1]), fill, x.dtype), x[0:t - s]], axis=0)
    rolled = pltpu.roll(x, s, axis=0)
    row = lax.broadcasted_iota(jnp.int32, x.shape, 0)
    return jnp.where(row < s, fill, rolled)


def _scan_rows(a, b):
    t = a.shape[0]
    s = 1
    while s < t:
        b = a * _shift_rows(b, s, 0.0) + b
        if 2 * s < t:
            a = a * _shift_rows(a, s, 1.0)
        s *= 2
    return b


def _lru_prompt_kernel(u_ref, gate_ref, cprev_ref, h0_ref, cw_ref, cb_ref, wg_ref, ba_ref, bx_ref,
                       ncs_ref, gl_ref, y_ref, conv_ref, hlast_ref, tail_ref, h_ref, ubuf_ref, gbuf_ref,
                       ybuf_ref, *, n_chunks):
    ti = pl.program_id(1)
    lane_chunks = D_LRU // LANES

    def to_segment_major(src_ref, r0, buf_ref):
        for s in range(SUBLANES):
            rows = pl.ds(pl.multiple_of(r0 + s * SEG_LEN, SEG_LEN), SEG_LEN)
            for c in range(lane_chunks):
                buf_ref[c, pl.ds(s, SEG_LEN, stride=SUBLANES), :] = src_ref[0, rows, c * LANES:(c + 1) * LANES]
        return jnp.concatenate([buf_ref[c] for c in range(lane_chunks)], axis=1)
    n_t = pl.num_programs(1)

    @pl.when(ti == 0)
    def _():
        tail_ref[...] = jnp.zeros_like(tail_ref)
        tail_ref[SUBLANES - (CONV_WIDTH - 1):SUBLANES, :] = cprev_ref[0]
        h_ref[...] = jnp.broadcast_to(h0_ref[0], h_ref.shape)

    cw = cw_ref[...]
    cb = cb_ref[...]
    ba = ba_ref[...]
    bxb = bx_ref[...]
    ncs = ncs_ref[...]
    gl = gl_ref[...]
    row8 = lax.broadcasted_iota(jnp.int32, (SUBLANES, D_LRU), 0)
    seg0 = row8 == 0
    n_prev = CONV_WIDTH - 1

    def vrow(x, tau):
        return x[SUBLANES * tau:SUBLANES * (tau + 1)]

    def chunk(ci, carry):
        r0 = pl.multiple_of(ci * LRU_CHUNK, LRU_CHUNK)
        u = to_segment_major(u_ref, r0, ubuf_ref)
        tail = tail_ref[...]
        pre = []
        for k in range(n_prev, 0, -1):
            from_prev_seg = pltpu.roll(vrow(u, SEG_LEN - k), 1, axis=0)
            pre.append(jnp.where(seg0, tail[SUBLANES - k:SUBLANES - k + 1], from_prev_seg))
            tail_ref[SUBLANES - k:SUBLANES - k + 1, :] = u[SUBLANES * (SEG_LEN - k + 1) - 1:
                                                           SUBLANES * (SEG_LEN - k + 1)]
        ext = jnp.concatenate(pre + [u], axis=0)
        uc = cb
        for j in range(CONV_WIDTH):
            uc = uc + cw[j:j + 1, :] * ext[SUBLANES * j:SUBLANES * j + LRU_CHUNK]
        a, b = _lru_gates(uc, wg_ref, ba, bxb, ncs)

        h = vrow(b, 0)
        p = vrow(a, 0)
        hs, ps = [h], [p]
        for tau in range(1, SEG_LEN):
            a_t = vrow(a, tau)
            h = a_t * h + vrow(b, tau)
            p = a_t * p
            hs.append(h)
            ps.append(p)
        h_in = h_ref[0:1, :]
        seg_end = _scan_rows(p, jnp.where(seg0, h + p * h_in, h))
        carry_in = jnp.where(seg0, h_in, pltpu.roll(seg_end, 1, axis=0))
        h_ref[...] = jnp.broadcast_to(seg_end[SUBLANES - 1:SUBLANES], h_ref.shape)
        h_all = jnp.concatenate([hs[t] + ps[t] * carry_in for t in range(SEG_LEN)], axis=0)

        y = h_all * _gelu_tanh(to_segment_major(gate_ref, r0, gbuf_ref))
        yn = y * _rms_scale(y) * gl
        for c in range(lane_chunks):
            ybuf_ref[c] = yn[:, c * LANES:(c + 1) * LANES]
        for s in range(SUBLANES):
            rows = jnp.concatenate([ybuf_ref[c, pl.ds(s, SEG_LEN, stride=SUBLANES), :]
                                    for c in range(lane_chunks)], axis=1)
            y_ref[0, pl.ds(pl.multiple_of(r0 + s * SEG_LEN, SEG_LEN), SEG_LEN), :] = rows.astype(BF16)
        return carry

    lax.fori_loop(0, n_chunks, chunk, 0)

    @pl.when(ti == n_t - 1)
    def _():
        conv_ref[0] = tail_ref[SUBLANES - (CONV_WIDTH - 1):SUBLANES, :]
        hlast_ref[0] = h_ref[0:1, :]


def _lru_prompt(u, gate, conv_prev, h0, lw, tt):
    b, s, _ = u.shape
    assert s % tt == 0 and tt % LRU_CHUNK == 0
    seq = lambda bi, ti: (bi, ti, 0)
    per_b = lambda bi, ti: (bi, 0, 0)
    c2 = lambda bi, ti: (0, 0)
    c3 = lambda bi, ti: (0, 0, 0)
    vec = pl.BlockSpec((1, D_LRU), c2)
    return pl.pallas_call(
        functools.partial(_lru_prompt_kernel, n_chunks=tt // LRU_CHUNK),
        grid=(b, s // tt),
        in_specs=[
            pl.BlockSpec((1, tt, D_LRU), seq),
            pl.BlockSpec((1, tt, D_LRU), seq),
            pl.BlockSpec((1, CONV_WIDTH - 1, D_LRU), per_b),
            pl.BlockSpec((1, 1, D_LRU), per_b),
            pl.BlockSpec((CONV_WIDTH, D_LRU), c2),
            vec,
            pl.BlockSpec((2, D_LRU // 2, D_LRU), c3),
            vec, vec, vec, vec,
        ],
        out_specs=[
            pl.BlockSpec((1, tt, D_LRU), seq),
            pl.BlockSpec((1, CONV_WIDTH - 1, D_LRU), per_b),
            pl.BlockSpec((1, 1, D_LRU), per_b),
        ],
        out_shape=[
            jax.ShapeDtypeStruct((b, s, D_LRU), BF16),
            jax.ShapeDtypeStruct((b, CONV_WIDTH - 1, D_LRU), F32),
            jax.ShapeDtypeStruct((b, 1, D_LRU), F32),
        ],
        scratch_shapes=[pltpu.VMEM((SUBLANES, D_LRU), F32), pltpu.VMEM((SUBLANES, D_LRU), F32)]
        + [pltpu.VMEM((D_LRU // LANES, LRU_CHUNK, LANES), F32)] * 3,
        compiler_params=pltpu.CompilerParams(
            dimension_semantics=("arbitrary", "arbitrary"), vmem_limit_bytes=VMEM_LIMIT_BYTES),
        name="lru_prompt",
    )(u, gate, conv_prev, h0, lw["conv_w"], lw["conv_b"], lw["w_gates"], lw["b_a"], lw["b_x"],
      lw["neg_c_softplus"], lw["g_lru"])


def _lru_sample_kernel(u_ref, gate_ref, cprev_ref, h0_ref, cw_ref, cb_ref, wg_ref, ba_ref, bx_ref,
                       ncs_ref, gl_ref, y_ref, conv_ref, hlast_ref):
    t_len, nb, _ = u_ref.shape
    cw = cw_ref[...]
    hist = [cprev_ref[j] for j in range(CONV_WIDTH - 1)] + [u_ref[t] for t in range(t_len)]
    ucs = []
    for t in range(t_len):
        uc = cb_ref[...]
        for j in range(CONV_WIDTH):
            uc = uc + cw[j:j + 1, :] * hist[t + j]
        ucs.append(uc)
    uc_all = jnp.concatenate(ucs, axis=0)
    a, b = _lru_gates(uc_all, wg_ref, ba_ref[...], bx_ref[...], ncs_ref[...])
    h = h0_ref[...]
    gl = gl_ref[...]
    for t in range(t_len):
        h = a[t * nb:(t + 1) * nb] * h + b[t * nb:(t + 1) * nb]
        y = h * _gelu_tanh(gate_ref[t])
        y_ref[t] = (y * _rms_scale(y) * gl).astype(BF16)
    hlast_ref[...] = h
    for j in range(CONV_WIDTH - 1):
        conv_ref[j] = hist[t_len + j]


def _lru_sample(u_t, gate_t, conv_prev_t, h0, lw):
    t_len, nb, _ = u_t.shape
    return pl.pallas_call(
        _lru_sample_kernel,
        out_shape=[
            jax.ShapeDtypeStruct((t_len, nb, D_LRU), BF16),
            jax.ShapeDtypeStruct((CONV_WIDTH - 1, nb, D_LRU), F32),
            jax.ShapeDtypeStruct((nb, D_LRU), F32),
        ],
        name="lru_sample",
    )(u_t, gate_t, conv_prev_t, h0, lw["conv_w"], lw["conv_b"], lw["w_gates"], lw["b_a"], lw["b_x"],
      lw["neg_c_softplus"], lw["g_lru"])


def _attn_prompt_kernel(q_ref, k_ref, v_ref, o_ref,
                        qp1, qp4, qp16, kb1, kb4, kb16, vb1, vb4, vb16, res_a, res_l, res_m, bias_ref,
                        stage_ref):
    sb = pl.program_id(2)
    qps = (qp1, qp4, qp16)
    kbs = (kb1, kb4, kb16)
    vbs = (vb1, vb4, vb16)
    dils = tuple(d for _, d in DILATED_PATTERNS)
    assert dils == (1, 4, 16)
    lane = lax.broadcasted_iota(jnp.int32, (QBLK, LANES), 1)
    head0 = lane < HEAD_DIM

    qi = lax.broadcasted_iota(jnp.int32, (QBLK, 2 * QBLK), 0)
    ki = lax.broadcasted_iota(jnp.int32, (QBLK, 2 * QBLK), 1)
    delta = qi + QBLK - ki
    band = (delta >= 0) & (delta <= KEYS_PER_PATTERN - 1)
    bias_ref[0] = jnp.where(band, 0.0, NEG_BIG)
    bias_ref[1] = jnp.where(band & (ki >= QBLK), 0.0, NEG_BIG)

    for kb, vb, d in zip(kbs, vbs, dils):
        cls_len = SUPER // d

        @pl.when(sb == 0)
        def _():
            kb[:, 0:QBLK, :] = jnp.zeros((d, QBLK, LANES), BF16)
            vb[:, :, 0:QBLK, 0:LANES] = jnp.zeros((HEADS_PER_STEP, d, QBLK, LANES), BF16)
            h0_all = lax.broadcasted_iota(jnp.int32, (d, QBLK + cls_len, LANES), 2) < HEAD_DIM
            vb[0, :, :, LANES:2 * LANES] = jnp.where(h0_all, 1.0, 0.0).astype(BF16)
            vb[1, :, :, LANES:2 * LANES] = jnp.where(h0_all, 0.0, 1.0).astype(BF16)

        @pl.when(sb > 0)
        def _():
            kb[:, 0:QBLK, :] = kb[:, cls_len:cls_len + QBLK, :]
            vb[:, :, 0:QBLK, 0:LANES] = vb[:, :, cls_len:cls_len + QBLK, 0:LANES]

    def put(x, p, c, which):
        n_rows = x.shape[0]
        if which == "q":
            qps[p][c] = (x * LOG2_E).astype(BF16)
        elif which == "k":
            kbs[p][c, QBLK:QBLK + n_rows, :] = x.astype(BF16)
        else:
            h0 = lax.broadcasted_iota(jnp.int32, x.shape, 1) < HEAD_DIM
            h0 = lax.broadcasted_iota(jnp.int32, x.shape, 1) < HEAD_DIM
            rows = slice(QBLK, QBLK + n_rows)
            vbs[p][0, c, rows, 0:LANES] = jnp.where(h0, x, 0.0).astype(BF16)
            vbs[p][1, c, rows, 0:LANES] = jnp.where(h0, 0.0, x).astype(BF16)

    for which, ref in (("q", q_ref), ("k", k_ref), ("v", v_ref)):
        put(ref[0], 0, 0, which)
        for c in range(4):
            x = ref[0, pl.ds(c, SUPER // 4, stride=4), :]
            stage_ref[c] = x
            put(x, 1, c, which)
        for r in range(16):
            put(stage_ref[r % 4, pl.ds(r // 4, QBLK, stride=4), :], 2, r, which)

    for p, (qp, kb, vb, d) in enumerate(zip(qps, kbs, vbs, dils)):
        blocks_per_class = SUPER // d // QBLK

        def unit(n, carry, p=p, qp=qp, kb=kb, vb=vb, d=d, blocks_per_class=blocks_per_class):
            c = n // blocks_per_class
            i = n % blocks_per_class
            r0 = pl.multiple_of(i * QBLK, QBLK)
            q_t = qp[c, pl.ds(r0, QBLK), :]
            k_t = kb[c, pl.ds(r0, 2 * QBLK), :]
            first = jnp.logical_and(sb == 0, i == 0).astype(jnp.int32)
            bias = bias_ref[first]
            acc = None
            maxes = []
            for h in range(HEADS_PER_STEP):
                hm = head0 if h == 0 else jnp.logical_not(head0)
                qh = jnp.where(hm, q_t, jnp.zeros_like(q_t))
                s = lax.dot_general(qh, k_t, (((1,), (1,)), ((), ())), preferred_element_type=F32)
                s = s + bias
                m = jnp.max(s, axis=-1, keepdims=True)
                pr = jnp.exp2(s - m).astype(BF16)
                v_t = vb[h, c, pl.ds(r0, 2 * QBLK), :]
                pv = jnp.dot(pr, v_t, preferred_element_type=F32)
                acc = pv if acc is None else acc + pv
                maxes.append(jnp.broadcast_to(m, (QBLK, LANES)))
            start = c + d * r0
            if d == 1:
                dst = pl.ds(pl.multiple_of(start, QBLK), QBLK)
            else:
                dst = pl.ds(start, QBLK, stride=d)
            res_a[p, dst, :] = acc[:, 0:LANES]
            res_l[p, dst, :] = acc[:, LANES:2 * LANES]
            res_m[p, dst, :] = jnp.where(head0, maxes[0], maxes[1])
            return carry

        lax.fori_loop(0, SUPER // QBLK, unit, 0, unroll=16)

    def merge(ci, carry):
        rows = pl.ds(pl.multiple_of(ci * 256, 256), 256)
        ms = [res_m[p, rows, :] for p in range(3)]
        m = jnp.maximum(jnp.maximum(ms[0], ms[1]), ms[2])
        ws = [jnp.exp2(x - m) for x in ms]
        den = ws[0] * res_l[0, rows, :] + ws[1] * res_l[1, rows, :] + ws[2] * res_l[2, rows, :]
        num = ws[0] * res_a[0, rows, :] + ws[1] * res_a[1, rows, :] + ws[2] * res_a[2, rows, :]
        o_ref[0, rows, :] = num * (1.0 / den)
        return carry

    lax.fori_loop(0, SUPER // 256, merge, 0)


def _attn_prompt(q, k, v):
    b, s, _ = q.shape
    assert s % SUPER == 0
    blk = pl.BlockSpec((1, SUPER, LANES), lambda bi, hp, sb: (bi, sb, hp))
    scratch = []
    for _, d in DILATED_PATTERNS:
        scratch.append(pltpu.VMEM((d, SUPER // d, LANES), BF16))
    for _, d in DILATED_PATTERNS:
        scratch.append(pltpu.VMEM((d, QBLK + SUPER // d, LANES), BF16))
    for _, d in DILATED_PATTERNS:
        scratch.append(pltpu.VMEM((HEADS_PER_STEP, d, QBLK + SUPER // d, 2 * LANES), BF16))
    scratch += [pltpu.VMEM((3, SUPER, LANES), F32)] * 3
    scratch += [pltpu.VMEM((2, QBLK, 2 * QBLK), F32), pltpu.VMEM((4, SUPER // 4, LANES), F32)]
    return pl.pallas_call(
        _attn_prompt_kernel,
        grid=(b, D_ATTN // LANES, s // SUPER),
        in_specs=[blk, blk, blk],
        out_specs=blk,
        out_shape=jax.ShapeDtypeStruct((b, s, D_ATTN), F32),
        scratch_shapes=scratch,
        compiler_params=pltpu.CompilerParams(
            dimension_semantics=("arbitrary", "arbitrary", "arbitrary"),
            vmem_limit_bytes=VMEM_LIMIT_BYTES),
        name="attn_prompt",
    )(q, k, v)


def _attn_sample_kernel(q_ref, kn_ref, vn_ref, kfar_ref, knear_ref, vfar_ref, vnear_ref, o_ref, *, t_len):
    kn = kn_ref[0]
    vn = vn_ref[0]
    j_new = lax.broadcasted_iota(jnp.int32, (t_len, N_HEADS, 1), 0)
    i_near = lax.broadcasted_iota(jnp.int32, (QBLK, N_HEADS, 1), 0)
    (w_small, _), (w_mid, d_mid), (_, d_far) = DILATED_PATTERNS
    for t in range(t_len):
        q = q_ref[0, t][None] * LOG2_E
        s_new = jnp.sum(kn * q, axis=-1, keepdims=True)
        outs, lses = [], []
        for window, dil in DILATED_PATTERNS:
            if dil == d_far:
                k_c = kfar_ref[0, :, t]
                v_c = vfar_ref[0, :, t]
            elif dil == d_mid:
                k_c = knear_ref[0, :, t]
                v_c = vnear_ref[0, :, t]
            else:
                first = (w_mid - w_small) // d_mid
                k_c = knear_ref[0, first:].reshape(w_small, N_HEADS, HEAD_DIM)
                v_c = vnear_ref[0, first:].reshape(w_small, N_HEADS, HEAD_DIM)
            s_c = jnp.sum(k_c * q, axis=-1, keepdims=True)
            if dil == 1:
                s_c = jnp.where(i_near >= t, s_c, NEG_BIG)
            dist_new = t - j_new
            ok_new = (dist_new >= 0) & ((dist_new & (dil - 1)) == 0) & (dist_new <= window)
            s_n = jnp.where(ok_new, s_new, NEG_BIG)
            m = jnp.maximum(jnp.max(s_c, axis=0, keepdims=True), jnp.max(s_n, axis=0, keepdims=True))
            p_c = jnp.exp2(s_c - m)
            p_n = jnp.exp2(s_n - m)
            l = jnp.sum(p_c, axis=0, keepdims=True) + jnp.sum(p_n, axis=0, keepdims=True)
            o = jnp.sum(p_c * v_c, axis=0, keepdims=True) + jnp.sum(p_n * vn, axis=0, keepdims=True)
            outs.append(o * (1.0 / l))
            lses.append(m + jnp.log2(l))
        m = jnp.maximum(jnp.maximum(lses[0], lses[1]), lses[2])
        es = [jnp.exp2(x - m) for x in lses]
        o = (es[0] * outs[0] + es[1] * outs[1] + es[2] * outs[2]) * (1.0 / (es[0] + es[1] + es[2]))
        o_ref[0, t] = o[0]


def _attn_sample(q, k_new, v_new, cache_k, cache_v):
    b, t_len, _, _ = q.shape
    w_buf = cache_k.shape[1]
    (w_small, d_small), (w_mid, d_mid), (w_far, d_far) = DILATED_PATTERNS
    assert w_buf == w_far == MAX_WINDOW and d_small == 1 and w_small == QBLK
    assert t_len <= d_mid and w_mid // d_mid == QBLK and w_far // d_far == QBLK
    new = pl.BlockSpec((1, t_len, N_HEADS, HEAD_DIM), lambda bi: (bi, 0, 0, 0))
    far = pl.BlockSpec((1, w_buf // d_far, t_len, N_HEADS, HEAD_DIM), lambda bi: (bi, 0, 0, 0, 0))
    near = pl.BlockSpec((1, w_mid // d_mid, d_mid, N_HEADS, HEAD_DIM),
                        lambda bi: (bi, w_buf // w_mid - 1, 0, 0, 0))
    far_view = lambda c: c.reshape(b, w_buf // d_far, d_far, N_HEADS, HEAD_DIM)
    near_view = lambda c: c.reshape(b, w_buf // d_mid, d_mid, N_HEADS, HEAD_DIM)
    return pl.pallas_call(
        functools.partial(_attn_sample_kernel, t_len=t_len),
        grid=(b,),
        in_specs=[new, new, new, far, near, far, near],
        out_specs=new,
        out_shape=jax.ShapeDtypeStruct((b, t_len, N_HEADS, HEAD_DIM), F32),
        compiler_params=pltpu.CompilerParams(
            dimension_semantics=("arbitrary",), vmem_limit_bytes=VMEM_LIMIT_BYTES),
        name="attn_sample",
    )(q, k_new, v_new, far_view(cache_k), near_view(cache_k), far_view(cache_v), near_view(cache_v))


def _mix_ffn_kernel(x_ref, yl_ref, ya_ref, ga_ref, wo_ref, gf_ref, wg_ref, wu_ref, wd_ref, gn_ref,
                    y_ref, *, final_norm):
    ya = ya_ref[...]
    ya_n = (ya * _rms_scale(ya) * ga_ref[...]).astype(BF16)
    mix = (jnp.dot(yl_ref[...], wo_ref[0:D_LRU, :], preferred_element_type=F32)
           + jnp.dot(ya_n, wo_ref[D_LRU:D_LRU + D_ATTN, :], preferred_element_type=F32))
    x1 = x_ref[...] + mix
    xn = (x1 * _rms_scale(x1) * gf_ref[...]).astype(BF16)
    hg = jnp.dot(xn, wg_ref[...], preferred_element_type=F32)
    hu = jnp.dot(xn, wu_ref[...], preferred_element_type=F32)
    act = (hg * _sigmoid(hg) * hu).astype(BF16)
    x2 = x1 + jnp.dot(act, wd_ref[...], preferred_element_type=F32)
    if final_norm:
        x2 = x2 * _rms_scale(x2) * gn_ref[...]
    y_ref[...] = x2


def _mix_ffn(x2d, yl, ya, pw, g_final, tm, final_norm):
    n = x2d.shape[0]
    row = lambda i: (i, 0)
    const = lambda i: (0, 0)
    once = pl.Buffered(1)
    return pl.pallas_call(
        functools.partial(_mix_ffn_kernel, final_norm=final_norm),
        grid=(n // tm,),
        in_specs=[
            pl.BlockSpec((tm, D_MODEL), row),
            pl.BlockSpec((tm, D_LRU), row),
            pl.BlockSpec((tm, D_ATTN), row),
            pl.BlockSpec((1, D_ATTN), const),
            pl.BlockSpec((D_LRU + D_ATTN, D_MODEL), const, pipeline_mode=once),
            pl.BlockSpec((1, D_MODEL), const),
            pl.BlockSpec((D_MODEL, D_FF), const, pipeline_mode=once),
            pl.BlockSpec((D_MODEL, D_FF), const, pipeline_mode=once),
            pl.BlockSpec((D_FF, D_MODEL), const, pipeline_mode=once),
            pl.BlockSpec((1, D_MODEL), const),
        ],
        out_specs=pl.BlockSpec((tm, D_MODEL), row),
        out_shape=jax.ShapeDtypeStruct((n, D_MODEL), F32),
        compiler_params=pltpu.CompilerParams(
            dimension_semantics=("arbitrary",), vmem_limit_bytes=VMEM_LIMIT_BYTES),
        name="mix_ffn",
    )(x2d, yl, ya, pw["g_attn"], pw["w_out"], pw["g_ffn"], pw["w_gate"], pw["w_up"], pw["w_down"],
      g_final)


def _block_diag_gate_weights(w_a, w_x):
    half_blocks = N_LRU_BLOCKS // 2
    half = D_LRU // 2

    def bd(w):
        eye = jnp.eye(half_blocks, dtype=w.dtype)
        return jnp.einsum("nij,nm->nimj", w, eye).reshape(half, half)

    halves = [jnp.concatenate([bd(w_a[c * half_blocks:(c + 1) * half_blocks]),
                               bd(w_x[c * half_blocks:(c + 1) * half_blocks])], axis=1)
              for c in range(2)]
    return jnp.stack(halves, axis=0).astype(BF16)


def _row(v):
    return v.reshape(1, -1).astype(F32)


def kernel(x_prompt, x_sample, state_conv, state_lru, cache_k, cache_v, norm_mix, w_in, conv_w, conv_b,
           lru_w_a, lru_b_a, lru_w_x, lru_b_x, lru_lambda, out_norm_lru, out_norm_attn, w_out, norm_ffn,
           w_gate, w_up, w_down, norm_final):
    depth = w_in.shape[0]
    bp, sp, _ = x_prompt.shape
    bs, ts, _ = x_sample.shape
    w_p = min(MAX_WINDOW, sp)
    assert ts >= CONV_WIDTH - 1
    xp = x_prompt.reshape(bp * sp, D_MODEL)
    xs = x_sample.reshape(bs * ts, D_MODEL)
    g_final = _row(norm_final)
    conv_p, lru_p, kw_p, vw_p = [], [], [], []
    conv_s, lru_s, kn_s, vn_s = [], [], [], []
    for l in range(depth):
        last = l == depth - 1
        w_in_bf = w_in[l].astype(BF16)
        g_mix = _row(norm_mix[l])
        lw = dict(
            conv_w=conv_w[l].astype(F32), conv_b=_row(conv_b[l]),
            w_gates=_block_diag_gate_weights(lru_w_a[l], lru_w_x[l]),
            b_a=_row(lru_b_a[l]), b_x=_row(lru_b_x[l]),
            neg_c_softplus=_row(-LRU_C * jax.nn.softplus(-lru_lambda[l].astype(F32))),
            g_lru=_row(out_norm_lru[l]))
        pw = dict(
            g_attn=_row(out_norm_attn[l]), w_out=w_out[l].astype(BF16), g_ffn=_row(norm_ffn[l]),
            w_gate=w_gate[l].astype(BF16), w_up=w_up[l].astype(BF16), w_down=w_down[l].astype(BF16))

        u, gate, q, k, v = _inproj(xp, g_mix, w_in_bf, tm=512)
        shp = lambda a: a.reshape(bp, sp, -1)
        yl, c_new, h_last = _lru_prompt(
            shp(u), shp(gate), jnp.zeros((bp, CONV_WIDTH - 1, D_LRU), F32),
            jnp.zeros((bp, 1, D_LRU), F32), lw, tt=1024)
        ya = _attn_prompt(shp(q), shp(k), shp(v))
        xp = _mix_ffn(xp, yl.reshape(bp * sp, D_LRU), ya.reshape(bp * sp, D_ATTN), pw, g_final,
                      tm=512, final_norm=last)
        conv_p.append(c_new)
        lru_p.append(h_last.reshape(bp, D_LRU))
        kw_p.append(shp(k)[:, sp - w_p:].reshape(bp, w_p, N_HEADS, HEAD_DIM))
        vw_p.append(shp(v)[:, sp - w_p:].reshape(bp, w_p, N_HEADS, HEAD_DIM))

        u, gate, q, k, v = _inproj(xs, g_mix, w_in_bf, tm=bs * ts)
        tmaj = lambda a: a.reshape(bs, ts, -1).transpose(1, 0, 2)
        yl_t, c_new_t, h_last = _lru_sample(
            tmaj(u), tmaj(gate), state_conv[l].astype(F32).transpose(1, 0, 2),
            state_lru[l].astype(F32), lw)
        heads = lambda a: a.reshape(bs, ts, N_HEADS, HEAD_DIM)
        k_heads, v_heads = heads(k), heads(v)
        ya = _attn_sample(heads(q), k_heads, v_heads, cache_k[l].astype(F32), cache_v[l].astype(F32))
        xs = _mix_ffn(xs, yl_t.transpose(1, 0, 2).reshape(bs * ts, D_LRU), ya.reshape(bs * ts, D_ATTN),
                      pw, g_final, tm=bs * ts, final_norm=last)
        conv_s.append(c_new_t.transpose(1, 0, 2))
        lru_s.append(h_last)
        kn_s.append(k_heads)
        vn_s.append(v_heads)
    return (xp.reshape(bp, sp, D_MODEL), xs.reshape(bs, ts, D_MODEL),
            jnp.stack(conv_p, 0), jnp.stack(lru_p, 0), jnp.stack(kw_p, 0), jnp.stack(vw_p, 0),
            jnp.stack(conv_s, 0), jnp.stack(lru_s, 0), jnp.stack(kn_s, 0), jnp.stack(vn_s, 0))
```

```python
import functools
import math

import jax
import jax.numpy as jnp
from jax import lax
from jax.experimental import pallas as pl
from jax.experimental.pallas import tpu as pltpu

F32 = jnp.float32
BF16 = jnp.bfloat16

D_MODEL = 1024
D_LRU = 512
D_ATTN = 512
N_HEADS = 8
HEAD_DIM = 64
N_LRU_BLOCKS = 8
LRU_BLOCK = D_LRU // N_LRU_BLOCKS
CONV_WIDTH = 4
LRU_C = 8.0
DILATED_PATTERNS = ((128, 1), (512, 4), (2048, 16))
MAX_WINDOW = 2048
D_FF = 2816
D_IN = 2 * D_LRU + 3 * D_ATTN
RMS_EPS = 1e-6
Q_SCALE = HEAD_DIM ** -0.5

LANES = 128
SUBLANES = 8
VMEM_LIMIT_BYTES = 56 * 1024 * 1024

KEYS_PER_PATTERN = 129
QBLK = 128
SUPER = 2048
HEADS_PER_STEP = LANES // HEAD_DIM
NEG_BIG = -1e30
LOG2_E = math.log2(math.e)
LRU_CHUNK = 256
SEG_LEN = LRU_CHUNK // SUBLANES
HEAD_SHIFT = HEAD_DIM.bit_length() - 1
HEADS_SHIFT = N_HEADS.bit_length() - 1
assert all(d & (d - 1) == 0 for _, d in DILATED_PATTERNS) and 1 << HEAD_SHIFT == HEAD_DIM
assert all(w // d + 1 == KEYS_PER_PATTERN for w, d in DILATED_PATTERNS)


def _rms_scale(x):
    return lax.rsqrt(jnp.mean(x * x, axis=-1, keepdims=True) + RMS_EPS)


def _inproj_kernel(x_ref, g_ref, w_ref, u_ref, gate_ref, q_ref, k_ref, v_ref):
    x = x_ref[...]
    xn = (x * _rms_scale(x) * g_ref[...]).astype(BF16)
    proj = jnp.dot(xn, w_ref[...], preferred_element_type=F32)
    u_ref[...] = proj[:, 0:D_LRU]
    gate_ref[...] = proj[:, D_LRU:2 * D_LRU]
    o = 2 * D_LRU
    q_ref[...] = proj[:, o:o + D_ATTN] * Q_SCALE
    k_ref[...] = proj[:, o + D_ATTN:o + 2 * D_ATTN]
    v_ref[...] = proj[:, o + 2 * D_ATTN:o + 3 * D_ATTN]


def _inproj(x2d, g, w_bf, tm):
    n = x2d.shape[0]
    assert n % tm == 0
    const = lambda i: (0, 0)
    row = lambda i: (i, 0)
    out_sds = jax.ShapeDtypeStruct((n, D_LRU), F32)
    return pl.pallas_call(
        _inproj_kernel,
        grid=(n // tm,),
        in_specs=[
            pl.BlockSpec((tm, D_MODEL), row),
            pl.BlockSpec((1, D_MODEL), const),
            pl.BlockSpec((D_MODEL, D_IN), const, pipeline_mode=pl.Buffered(1)),
        ],
        out_specs=[pl.BlockSpec((tm, D_LRU), row)] * 5,
        out_shape=[out_sds] * 5,
        compiler_params=pltpu.CompilerParams(
            dimension_semantics=("arbitrary",), vmem_limit_bytes=VMEM_LIMIT_BYTES),
        name="inproj",
    )(x2d, g, w_bf)


def _gelu_tanh(x):
    c = math.sqrt(2.0 / math.pi)
    return 0.5 * x * (1.0 + jnp.tanh(c * (x + 0.044715 * (x * x * x))))


def _sigmoid(x):
    return jax.nn.sigmoid(x)


def _lru_gates(uc, wg_ref, ba, bx_bias, neg_c_softplus):
    half = D_LRU // 2
    ucb = uc.astype(BF16)
    pre0 = jnp.dot(ucb[:, 0:half], wg_ref[0], preferred_element_type=F32)
    pre1 = jnp.dot(ucb[:, half:D_LRU], wg_ref[1], preferred_element_type=F32)
    r_pre = jnp.concatenate([pre0[:, 0:half], pre1[:, 0:half]], axis=1) + ba
    i_pre = jnp.concatenate([pre0[:, half:], pre1[:, half:]], axis=1) + bx_bias
    r = _sigmoid(r_pre)
    i = _sigmoid(i_pre)
    log_a = r * neg_c_softplus
    a = jnp.exp(log_a)
    z = -jnp.tanh(log_a) * (a * a + 1.0)
    root = jnp.where(z > 0.0, z * lax.rsqrt(z), 0.0)
    b = root * (i * uc)
    return a, b


def _shift_rows(x, s, fill):
    t = x.shape[0]
    if s % SUBLANES == 0:
        return jnp.concatenate([jnp.full((s, x.shape[1]), fill, x.dtype), x[0:t - s]], axis=0)
    rolled = pltpu.roll(x, s, axis=0)
    row = lax.broadcasted_iota(jnp.int32, x.shape, 0)
    return jnp.where(row < s, fill, rolled)


def _scan_rows(a, b):
    t = a.shape[0]
    s = 1
    while s < t:
        b = a * _shift_rows(b, s, 0.0) + b
        if 2 * s < t:
            a = a * _shift_rows(a, s, 1.0)
        s *= 2
    return b


def _lru_prompt_kernel(u_ref, gate_ref, cprev_ref, h0_ref, cw_ref, cb_ref, wg_ref, ba_ref, bx_ref,
                       ncs_ref, gl_ref, y_ref, conv_ref, hlast_ref, tail_ref, h_ref, ubuf_ref, gbuf_ref,
                       ybuf_ref, *, n_chunks):
    ti = pl.program_id(1)
    lane_chunks = D_LRU // LANES

    def to_segment_major(src_ref, r0, buf_ref):
        for s in range(SUBLANES):
            rows = pl.ds(pl.multiple_of(r0 + s * SEG_LEN, SEG_LEN), SEG_LEN)
            for c in range(lane_chunks):
                buf_ref[c, pl.ds(s, SEG_LEN, stride=SUBLANES), :] = src_ref[0, rows, c * LANES:(c + 1) * LANES]
        return jnp.concatenate([buf_ref[c] for c in range(lane_chunks)], axis=1)
    n_t = pl.num_programs(1)

    @pl.when(ti == 0)
    def _():
        tail_ref[...] = jnp.zeros_like(tail_ref)
        tail_ref[SUBLANES - (CONV_WIDTH - 1):SUBLANES, :] = cprev_ref[0]
        h_ref[...] = jnp.broadcast_to(h0_ref[0], h_ref.shape)

    cw = cw_ref[...]
    cb = cb_ref[...]
    ba = ba_ref[...]
    bxb = bx_ref[...]
    ncs = ncs_ref[...]
    gl = gl_ref[...]
    row8 = lax.broadcasted_iota(jnp.int32, (SUBLANES, D_LRU), 0)
    seg0 = row8 == 0
    n_prev = CONV_WIDTH - 1

    def vrow(x, tau):
        return x[SUBLANES * tau:SUBLANES * (tau + 1)]

    def chunk(ci, carry):
        r0 = pl.multiple_of(ci * LRU_CHUNK, LRU_CHUNK)
        u = to_segment_major(u_ref, r0, ubuf_ref)
        tail = tail_ref[...]
        pre = []
        for k in range(n_prev, 0, -1):
            from_prev_seg = pltpu.roll(vrow(u, SEG_LEN - k), 1, axis=0)
            pre.append(jnp.where(seg0, tail[SUBLANES - k:SUBLANES - k + 1], from_prev_seg))
            tail_ref[SUBLANES - k:SUBLANES - k + 1, :] = u[SUBLANES * (SEG_LEN - k + 1) - 1:
                                                           SUBLANES * (SEG_LEN - k + 1)]
        ext = jnp.concatenate(pre + [u], axis=0)
        uc = cb
        for j in range(CONV_WIDTH):
            uc = uc + cw[j:j + 1, :] * ext[SUBLANES * j:SUBLANES * j + LRU_CHUNK]
        a, b = _lru_gates(uc, wg_ref, ba, bxb, ncs)

        h = vrow(b, 0)
        p = vrow(a, 0)
        hs, ps = [h], [p]
        for tau in range(1, SEG_LEN):
            a_t = vrow(a, tau)
            h = a_t * h + vrow(b, tau)
            p = a_t * p
            hs.append(h)
            ps.append(p)
        h_in = h_ref[0:1, :]
        seg_end = _scan_rows(p, jnp.where(seg0, h + p * h_in, h))
        carry_in = jnp.where(seg0, h_in, pltpu.roll(seg_end, 1, axis=0))
        h_ref[...] = jnp.broadcast_to(seg_end[SUBLANES - 1:SUBLANES], h_ref.shape)
        h_all = jnp.concatenate([hs[t] + ps[t] * carry_in for t in range(SEG_LEN)], axis=0)

        y = h_all * _gelu_tanh(to_segment_major(gate_ref, r0, gbuf_ref))
        yn = y * _rms_scale(y) * gl
        for c in range(lane_chunks):
            ybuf_ref[c] = yn[:, c * LANES:(c + 1) * LANES]
        for s in range(SUBLANES):
            rows = jnp.concatenate([ybuf_ref[c, pl.ds(s, SEG_LEN, stride=SUBLANES), :]
                                    for c in range(lane_chunks)], axis=1)
            y_ref[0, pl.ds(pl.multiple_of(r0 + s * SEG_LEN, SEG_LEN), SEG_LEN), :] = rows.astype(BF16)
        return carry

    lax.fori_loop(0, n_chunks, chunk, 0)

    @pl.when(ti == n_t - 1)
    def _():
        conv_ref[0] = tail_ref[SUBLANES - (CONV_WIDTH - 1):SUBLANES, :]
        hlast_ref[0] = h_ref[0:1, :]


def _lru_prompt(u, gate, conv_prev, h0, lw, tt):
    b, s, _ = u.shape
    assert s % tt == 0 and tt % LRU_CHUNK == 0
    seq = lambda bi, ti: (bi, ti, 0)
    per_b = lambda bi, ti: (bi, 0, 0)
    c2 = lambda bi, ti: (0, 0)
    c3 = lambda bi, ti: (0, 0, 0)
    vec = pl.BlockSpec((1, D_LRU), c2)
    return pl.pallas_call(
        functools.partial(_lru_prompt_kernel, n_chunks=tt // LRU_CHUNK),
        grid=(b, s // tt),
        in_specs=[
            pl.BlockSpec((1, tt, D_LRU), seq),
            pl.BlockSpec((1, tt, D_LRU), seq),
            pl.BlockSpec((1, CONV_WIDTH - 1, D_LRU), per_b),
            pl.BlockSpec((1, 1, D_LRU), per_b),
            pl.BlockSpec((CONV_WIDTH, D_LRU), c2),
            vec,
            pl.BlockSpec((2, D_LRU // 2, D_LRU), c3),
            vec, vec, vec, vec,
        ],
        out_specs=[
            pl.BlockSpec((1, tt, D_LRU), seq),
            pl.BlockSpec((1, CONV_WIDTH - 1, D_LRU), per_b),
            pl.BlockSpec((1, 1, D_LRU), per_b),
        ],
        out_shape=[
            jax.ShapeDtypeStruct((b, s, D_LRU), BF16),
            jax.ShapeDtypeStruct((b, CONV_WIDTH - 1, D_LRU), F32),
            jax.ShapeDtypeStruct((b, 1, D_LRU), F32),
        ],
        scratch_shapes=[pltpu.VMEM((SUBLANES, D_LRU), F32), pltpu.VMEM((SUBLANES, D_LRU), F32)]
        + [pltpu.VMEM((D_LRU // LANES, LRU_CHUNK, LANES), F32)] * 3,
        compiler_params=pltpu.CompilerParams(
            dimension_semantics=("arbitrary", "arbitrary"), vmem_limit_bytes=VMEM_LIMIT_BYTES),
        name="lru_prompt",
    )(u, gate, conv_prev, h0, lw["conv_w"], lw["conv_b"], lw["w_gates"], lw["b_a"], lw["b_x"],
      lw["neg_c_softplus"], lw["g_lru"])


def _lru_sample_kernel(u_ref, gate_ref, cprev_ref, h0_ref, cw_ref, cb_ref, wg_ref, ba_ref, bx_ref,
                       ncs_ref, gl_ref, y_ref, conv_ref, hlast_ref):
    t_len, nb, _ = u_ref.shape
    cw = cw_ref[...]
    hist = [cprev_ref[j] for j in range(CONV_WIDTH - 1)] + [u_ref[t] for t in range(t_len)]
    ucs = []
    for t in range(t_len):
        uc = cb_ref[...]
        for j in range(CONV_WIDTH):
            uc = uc + cw[j:j + 1, :] * hist[t + j]
        ucs.append(uc)
    uc_all = jnp.concatenate(ucs, axis=0)
    a, b = _lru_gates(uc_all, wg_ref, ba_ref[...], bx_ref[...], ncs_ref[...])
    h = h0_ref[...]
    gl = gl_ref[...]
    for t in range(t_len):
        h = a[t * nb:(t + 1) * nb] * h + b[t * nb:(t + 1) * nb]
        y = h * _gelu_tanh(gate_ref[t])
        y_ref[t] = (y * _rms_scale(y) * gl).astype(BF16)
    hlast_ref[...] = h
    for j in range(CONV_WIDTH - 1):
        conv_ref[j] = hist[t_len + j]


def _lru_sample(u_t, gate_t, conv_prev_t, h0, lw):
    t_len, nb, _ = u_t.shape
    return pl.pallas_call(
        _lru_sample_kernel,
        out_shape=[
            jax.ShapeDtypeStruct((t_len, nb, D_LRU), BF16),
            jax.ShapeDtypeStruct((CONV_WIDTH - 1, nb, D_LRU), F32),
            jax.ShapeDtypeStruct((nb, D_LRU), F32),
        ],
        name="lru_sample",
    )(u_t, gate_t, conv_prev_t, h0, lw["conv_w"], lw["conv_b"], lw["w_gates"], lw["b_a"], lw["b_x"],
      lw["neg_c_softplus"], lw["g_lru"])


def _attn_prompt_kernel(q_ref, k_ref, v_ref, o_ref,
                        qp1, qp4, qp16, kb1, kb4, kb16, vb1, vb4, vb16, res_a, res_l, res_m, bias_ref,
                        stage_ref):
    sb = pl.program_id(2)
    qps = (qp1, qp4, qp16)
    kbs = (kb1, kb4, kb16)
    vbs = (vb1, vb4, vb16)
    dils = tuple(d for _, d in DILATED_PATTERNS)
    assert dils == (1, 4, 16)
    lane = lax.broadcasted_iota(jnp.int32, (QBLK, LANES), 1)
    head0 = lane < HEAD_DIM

    qi = lax.broadcasted_iota(jnp.int32, (QBLK, 2 * QBLK), 0)
    ki = lax.broadcasted_iota(jnp.int32, (QBLK, 2 * QBLK), 1)
    delta = qi + QBLK - ki
    band = (delta >= 0) & (delta <= KEYS_PER_PATTERN - 1)
    bias_ref[0] = jnp.where(band, 0.0, NEG_BIG)
    bias_ref[1] = jnp.where(band & (ki >= QBLK), 0.0, NEG_BIG)

    for kb, vb, d in zip(kbs, vbs, dils):
        cls_len = SUPER // d

        @pl.when(sb == 0)
        def _():
            kb[:, 0:QBLK, :] = jnp.zeros((d, QBLK, LANES), BF16)
            vb[:, :, 0:QBLK, 0:LANES] = jnp.zeros((HEADS_PER_STEP, d, QBLK, LANES), BF16)
            h0_all = lax.broadcasted_iota(jnp.int32, (d, QBLK + cls_len, LANES), 2) < HEAD_DIM
            vb[0, :, :, LANES:2 * LANES] = jnp.where(h0_all, 1.0, 0.0).astype(BF16)
            vb[1, :, :, LANES:2 * LANES] = jnp.where(h0_all, 0.0, 1.0).astype(BF16)

        @pl.when(sb > 0)
        def _():
            kb[:, 0:QBLK, :] = kb[:, cls_len:cls_len + QBLK, :]
            vb[:, :, 0:QBLK, 0:LANES] = vb[:, :, cls_len:cls_len + QBLK, 0:LANES]

    def put(x, p, c, which):
        n_rows = x.shape[0]
        if which == "q":
            qps[p][c] = (x * LOG2_E).astype(BF16)
        elif which == "k":
            kbs[p][c, QBLK:QBLK + n_rows, :] = x.astype(BF16)
        else:
            h0 = lax.broadcasted_iota(jnp.int32, x.shape, 1) < HEAD_DIM
            h0 = lax.broadcasted_iota(jnp.int32, x.shape, 1) < HEAD_DIM
            rows = slice(QBLK, QBLK + n_rows)
            vbs[p][0, c, rows, 0:LANES] = jnp.where(h0, x, 0.0).astype(BF16)
            vbs[p][1, c, rows, 0:LANES] = jnp.where(h0, 0.0, x).astype(BF16)

    for which, ref in (("q", q_ref), ("k", k_ref), ("v", v_ref)):
        put(ref[0], 0, 0, which)
        for c in range(4):
            x = ref[0, pl.ds(c, SUPER // 4, stride=4), :]
            stage_ref[c] = x
            put(x, 1, c, which)
        for r in range(16):
            put(stage_ref[r % 4, pl.ds(r // 4, QBLK, stride=4), :], 2, r, which)

    for p, (qp, kb, vb, d) in enumerate(zip(qps, kbs, vbs, dils)):
        blocks_per_class = SUPER // d // QBLK

        def unit(n, carry, p=p, qp=qp, kb=kb, vb=vb, d=d, blocks_per_class=blocks_per_class):
            c = n // blocks_per_class
            i = n % blocks_per_class
            r0 = pl.multiple_of(i * QBLK, QBLK)
            q_t = qp[c, pl.ds(r0, QBLK), :]
            k_t = kb[c, pl.ds(r0, 2 * QBLK), :]
            first = jnp.logical_and(sb == 0, i == 0).astype(jnp.int32)
            bias = bias_ref[first]
            acc = None
            maxes = []
            for h in range(HEADS_PER_STEP):
                hm = head0 if h == 0 else jnp.logical_not(head0)
                qh = jnp.where(hm, q_t, jnp.zeros_like(q_t))
                s = lax.dot_general(qh, k_t, (((1,), (1,)), ((), ())), preferred_element_type=F32)
                s = s + bias
                m = jnp.max(s, axis=-1, keepdims=True)
                pr = jnp.exp2(s - m).astype(BF16)
                v_t = vb[h, c, pl.ds(r0, 2 * QBLK), :]
                pv = jnp.dot(pr, v_t, preferred_element_type=F32)
                acc = pv if acc is None else acc + pv
                maxes.append(jnp.broadcast_to(m, (QBLK, LANES)))
            start = c + d * r0
            if d == 1:
                dst = pl.ds(pl.multiple_of(start, QBLK), QBLK)
            else:
                dst = pl.ds(start, QBLK, stride=d)
            res_a[p, dst, :] = acc[:, 0:LANES]
            res_l[p, dst, :] = acc[:, LANES:2 * LANES]
            res_m[p, dst, :] = jnp.where(head0, maxes[0], maxes[1])
            return carry

        lax.fori_loop(0, SUPER // QBLK, unit, 0, unroll=16)

    def merge(ci, carry):
        rows = pl.ds(pl.multiple_of(ci * 256, 256), 256)
        ms = [res_m[p, rows, :] for p in range(3)]
        m = jnp.maximum(jnp.maximum(ms[0], ms[1]), ms[2])
        ws = [jnp.exp2(x - m) for x in ms]
        den = ws[0] * res_l[0, rows, :] + ws[1] * res_l[1, rows, :] + ws[2] * res_l[2, rows, :]
        num = ws[0] * res_a[0, rows, :] + ws[1] * res_a[1, rows, :] + ws[2] * res_a[2, rows, :]
        o_ref[0, rows, :] = num * (1.0 / den)
        return carry

    lax.fori_loop(0, SUPER // 256, merge, 0)


def _attn_prompt(q, k, v):
    b, s, _ = q.shape
    assert s % SUPER == 0
    blk = pl.BlockSpec((1, SUPER, LANES), lambda bi, hp, sb: (bi, sb, hp))
    scratch = []
    for _, d in DILATED_PATTERNS:
        scratch.append(pltpu.VMEM((d, SUPER // d, LANES), BF16))
    for _, d in DILATED_PATTERNS:
        scratch.append(pltpu.VMEM((d, QBLK + SUPER // d, LANES), BF16))
    for _, d in DILATED_PATTERNS:
        scratch.append(pltpu.VMEM((HEADS_PER_STEP, d, QBLK + SUPER // d, 2 * LANES), BF16))
    scratch += [pltpu.VMEM((3, SUPER, LANES), F32)] * 3
    scratch += [pltpu.VMEM((2, QBLK, 2 * QBLK), F32), pltpu.VMEM((4, SUPER // 4, LANES), F32)]
    return pl.pallas_call(
        _attn_prompt_kernel,
        grid=(b, D_ATTN // LANES, s // SUPER),
        in_specs=[blk, blk, blk],
        out_specs=blk,
        out_shape=jax.ShapeDtypeStruct((b, s, D_ATTN), F32),
        scratch_shapes=scratch,
        compiler_params=pltpu.CompilerParams(
            dimension_semantics=("arbitrary", "arbitrary", "arbitrary"),
            vmem_limit_bytes=VMEM_LIMIT_BYTES),
        name="attn_prompt",
    )(q, k, v)


def _attn_sample_kernel(q_ref, kn_ref, vn_ref, kt_ref, vt_ref, o_ref, *, t_len, w_buf):
    rows = N_HEADS * SUBLANES
    n_pat = len(DILATED_PATTERNS)
    pad = jnp.zeros((SUBLANES - t_len, D_ATTN), F32)
    q8 = jnp.concatenate([q_ref[0] * LOG2_E, pad], axis=0)
    kn8 = jnp.concatenate([kn_ref[0], pad], axis=0).astype(BF16)
    vn8 = jnp.concatenate([vn_ref[0], pad], axis=0).astype(BF16)
    head_lanes = lambda h: slice(h * HEAD_DIM, (h + 1) * HEAD_DIM)

    s_c = jnp.concatenate(
        [jnp.dot(q8[:, head_lanes(h)].astype(BF16), kt_ref[0, h].astype(BF16), preferred_element_type=F32)
         for h in range(N_HEADS)], axis=0)
    rid = lax.broadcasted_iota(jnp.int32, (rows, D_ATTN), 0)
    lid = lax.broadcasted_iota(jnp.int32, (rows, D_ATTN), 1)
    qbd = jnp.where((rid >> 3) == (lid >> HEAD_SHIFT), jnp.concatenate([q8] * N_HEADS, axis=0), 0.0)
    s_n = lax.dot_general(qbd.astype(BF16), kn8, (((1,), (1,)), ((), ())), preferred_element_type=F32)

    t_c = lax.broadcasted_iota(jnp.int32, (rows, w_buf), 0) & (SUBLANES - 1)
    d_c = w_buf + t_c - lax.broadcasted_iota(jnp.int32, (rows, w_buf), 1)
    t_n = lax.broadcasted_iota(jnp.int32, (rows, SUBLANES), 0) & (SUBLANES - 1)
    j_n = lax.broadcasted_iota(jnp.int32, (rows, SUBLANES), 1)
    d_n = t_n - j_n
    new_ok = (d_n >= 0) & (j_n < t_len)

    p_cs, p_ns, lses, ls = [], [], [], []
    for window, dil in DILATED_PATTERNS:
        ok_c = ((d_c & (dil - 1)) == 0) & (d_c <= window)
        ok_n = new_ok & ((d_n & (dil - 1)) == 0) & (d_n <= window)
        sc = jnp.where(ok_c, s_c, NEG_BIG)
        sn = jnp.where(ok_n, s_n, NEG_BIG)
        m = jnp.maximum(jnp.max(sc, axis=-1, keepdims=True), jnp.max(sn, axis=-1, keepdims=True))
        pc = jnp.exp2(sc - m)
        pn = jnp.exp2(sn - m)
        l = jnp.sum(pc, axis=-1, keepdims=True) + jnp.sum(pn, axis=-1, keepdims=True)
        p_cs.append(pc)
        p_ns.append(pn)
        ls.append(l)
        lses.append(m + jnp.log2(l))
    m = jnp.maximum(jnp.maximum(lses[0], lses[1]), lses[2])
    es = [jnp.exp2(x - m) for x in lses]
    inv_den = 1.0 / (es[0] + es[1] + es[2])
    wts = [es[p] * inv_den * (1.0 / ls[p]) for p in range(n_pat)]

    o_new = jnp.dot(jnp.concatenate(p_ns, axis=0).astype(BF16), vn8, preferred_element_type=F32)
    for h in range(N_HEADS):
        rs = slice(h * SUBLANES, (h + 1) * SUBLANES)
        p_h = jnp.concatenate([p_cs[p][rs] for p in range(n_pat)], axis=0).astype(BF16)
        o_h = lax.dot_general(p_h, vt_ref[0, h].astype(BF16), (((1,), (1,)), ((), ())),
                              preferred_element_type=F32)
        out = None
        for p in range(n_pat):
            new_part = o_new[p * rows + h * SUBLANES:p * rows + (h + 1) * SUBLANES, head_lanes(h)]
            term = wts[p][rs] * (o_h[p * SUBLANES:(p + 1) * SUBLANES] + new_part)
            out = term if out is None else out + term
        o_ref[0, :, head_lanes(h)] = out[0:t_len]


def _attn_sample(q, k_new, v_new, cache_k, cache_v):
    b, t_len, _ = q.shape
    w_buf = cache_k.shape[1]
    assert w_buf >= MAX_WINDOW and t_len <= SUBLANES
    new = pl.BlockSpec((1, t_len, D_ATTN), lambda bi: (bi, 0, 0))
    cache = pl.BlockSpec((1, N_HEADS, HEAD_DIM, w_buf), lambda bi: (bi, 0, 0, 0))
    row_minor = lambda c: jnp.transpose(c, (0, 2, 3, 1))
    return pl.pallas_call(
        functools.partial(_attn_sample_kernel, t_len=t_len, w_buf=w_buf),
        grid=(b,),
        in_specs=[new, new, new, cache, cache],
        out_specs=new,
        out_shape=jax.ShapeDtypeStruct((b, t_len, D_ATTN), F32),
        compiler_params=pltpu.CompilerParams(
            dimension_semantics=("arbitrary",), vmem_limit_bytes=VMEM_LIMIT_BYTES),
        name="attn_sample",
    )(q, k_new, v_new, row_minor(cache_k), row_minor(cache_v))


def _mix_ffn_kernel(x_ref, yl_ref, ya_ref, ga_ref, wo_ref, gf_ref, wg_ref, wu_ref, wd_ref, gn_ref,
                    y_ref, *, final_norm):
    ya = ya_ref[...]
    ya_n = (ya * _rms_scale(ya) * ga_ref[...]).astype(BF16)
    mix = (jnp.dot(yl_ref[...], wo_ref[0:D_LRU, :], preferred_element_type=F32)
           + jnp.dot(ya_n, wo_ref[D_LRU:D_LRU + D_ATTN, :], preferred_element_type=F32))
    x1 = x_ref[...] + mix
    xn = (x1 * _rms_scale(x1) * gf_ref[...]).astype(BF16)
    hg = jnp.dot(xn, wg_ref[...], preferred_element_type=F32)
    hu = jnp.dot(xn, wu_ref[...], preferred_element_type=F32)
    act = (hg * _sigmoid(hg) * hu).astype(BF16)
    x2 = x1 + jnp.dot(act, wd_ref[...], preferred_element_type=F32)
    if final_norm:
        x2 = x2 * _rms_scale(x2) * gn_ref[...]
    y_ref[...] = x2


def _mix_ffn(x2d, yl, ya, pw, g_final, tm, final_norm):
    n = x2d.shape[0]
    row = lambda i: (i, 0)
    const = lambda i: (0, 0)
    once = pl.Buffered(1)
    return pl.pallas_call(
        functools.partial(_mix_ffn_kernel, final_norm=final_norm),
        grid=(n // tm,),
        in_specs=[
            pl.BlockSpec((tm, D_MODEL), row),
            pl.BlockSpec((tm, D_LRU), row),
            pl.BlockSpec((tm, D_ATTN), row),
            pl.BlockSpec((1, D_ATTN), const),
            pl.BlockSpec((D_LRU + D_ATTN, D_MODEL), const, pipeline_mode=once),
            pl.BlockSpec((1, D_MODEL), const),
            pl.BlockSpec((D_MODEL, D_FF), const, pipeline_mode=once),
            pl.BlockSpec((D_MODEL, D_FF), const, pipeline_mode=once),
            pl.BlockSpec((D_FF, D_MODEL), const, pipeline_mode=once),
            pl.BlockSpec((1, D_MODEL), const),
        ],
        out_specs=pl.BlockSpec((tm, D_MODEL), row),
        out_shape=jax.ShapeDtypeStruct((n, D_MODEL), F32),
        compiler_params=pltpu.CompilerParams(
            dimension_semantics=("arbitrary",), vmem_limit_bytes=VMEM_LIMIT_BYTES),
        name="mix_ffn",
    )(x2d, yl, ya, pw["g_attn"], pw["w_out"], pw["g_ffn"], pw["w_gate"], pw["w_up"], pw["w_down"],
      g_final)


def _block_diag_gate_weights(w_a, w_x):
    half_blocks = N_LRU_BLOCKS // 2
    half = D_LRU // 2

    def bd(w):
        eye = jnp.eye(half_blocks, dtype=w.dtype)
        return jnp.einsum("nij,nm->nimj", w, eye).reshape(half, half)

    halves = [jnp.concatenate([bd(w_a[c * half_blocks:(c + 1) * half_blocks]),
                               bd(w_x[c * half_blocks:(c + 1) * half_blocks])], axis=1)
              for c in range(2)]
    return jnp.stack(halves, axis=0).astype(BF16)


def _row(v):
    return v.reshape(1, -1).astype(F32)


def kernel(x_prompt, x_sample, state_conv, state_lru, cache_k, cache_v, norm_mix, w_in, conv_w, conv_b,
           lru_w_a, lru_b_a, lru_w_x, lru_b_x, lru_lambda, out_norm_lru, out_norm_attn, w_out, norm_ffn,
           w_gate, w_up, w_down, norm_final):
    depth = w_in.shape[0]
    bp, sp, _ = x_prompt.shape
    bs, ts, _ = x_sample.shape
    w_p = min(MAX_WINDOW, sp)
    assert ts >= CONV_WIDTH - 1
    xp = x_prompt.reshape(bp * sp, D_MODEL)
    xs = x_sample.reshape(bs * ts, D_MODEL)
    g_final = _row(norm_final)
    conv_p, lru_p, kw_p, vw_p = [], [], [], []
    conv_s, lru_s, kn_s, vn_s = [], [], [], []
    for l in range(depth):
        last = l == depth - 1
        w_in_bf = w_in[l].astype(BF16)
        g_mix = _row(norm_mix[l])
        lw = dict(
            conv_w=conv_w[l].astype(F32), conv_b=_row(conv_b[l]),
            w_gates=_block_diag_gate_weights(lru_w_a[l], lru_w_x[l]),
            b_a=_row(lru_b_a[l]), b_x=_row(lru_b_x[l]),
            neg_c_softplus=_row(-LRU_C * jax.nn.softplus(-lru_lambda[l].astype(F32))),
            g_lru=_row(out_norm_lru[l]))
        pw = dict(
            g_attn=_row(out_norm_attn[l]), w_out=w_out[l].astype(BF16), g_ffn=_row(norm_ffn[l]),
            w_gate=w_gate[l].astype(BF16), w_up=w_up[l].astype(BF16), w_down=w_down[l].astype(BF16))

        u, gate, q, k, v = _inproj(xp, g_mix, w_in_bf, tm=512)
        shp = lambda a: a.reshape(bp, sp, -1)
        yl, c_new, h_last = _lru_prompt(
            shp(u), shp(gate), jnp.zeros((bp, CONV_WIDTH - 1, D_LRU), F32),
            jnp.zeros((bp, 1, D_LRU), F32), lw, tt=1024)
        ya = _attn_prompt(shp(q), shp(k), shp(v))
        xp = _mix_ffn(xp, yl.reshape(bp * sp, D_LRU), ya.reshape(bp * sp, D_ATTN), pw, g_final,
                      tm=512, final_norm=last)
        conv_p.append(c_new)
        lru_p.append(h_last.reshape(bp, D_LRU))
        kw_p.append(shp(k)[:, sp - w_p:].reshape(bp, w_p, N_HEADS, HEAD_DIM))
        vw_p.append(shp(v)[:, sp - w_p:].reshape(bp, w_p, N_HEADS, HEAD_DIM))

        u, gate, q, k, v = _inproj(xs, g_mix, w_in_bf, tm=bs * ts)
        tmaj = lambda a: a.reshape(bs, ts, -1).transpose(1, 0, 2)
        yl_t, c_new_t, h_last = _lru_sample(
            tmaj(u), tmaj(gate), state_conv[l].astype(F32).transpose(1, 0, 2),
            state_lru[l].astype(F32), lw)
        seq = lambda a: a.reshape(bs, ts, D_ATTN)
        ya = _attn_sample(seq(q), seq(k), seq(v), cache_k[l].astype(F32), cache_v[l].astype(F32))
        xs = _mix_ffn(xs, yl_t.transpose(1, 0, 2).reshape(bs * ts, D_LRU), ya.reshape(bs * ts, D_ATTN),
                      pw, g_final, tm=bs * ts, final_norm=last)
        conv_s.append(c_new_t.transpose(1, 0, 2))
        lru_s.append(h_last)
        kn_s.append(k.reshape(bs, ts, N_HEADS, HEAD_DIM))
        vn_s.append(v.reshape(bs, ts, N_HEADS, HEAD_DIM))
    return (xp.reshape(bp, sp, D_MODEL), xs.reshape(bs, ts, D_MODEL),
            jnp.stack(conv_p, 0), jnp.stack(lru_p, 0), jnp.stack(kw_p, 0), jnp.stack(vw_p, 0),
            jnp.stack(conv_s, 0), jnp.stack(lru_s, 0), jnp.stack(kn_s, 0), jnp.stack(vn_s, 0))
```

```python
import functools
import math

import jax
import jax.numpy as jnp
from jax import lax
from jax.experimental import pallas as pl
from jax.experimental.pallas import tpu as pltpu

F32 = jnp.float32
BF16 = jnp.bfloat16

D_MODEL = 1024
D_LRU = 512
D_ATTN = 512
N_HEADS = 8
HEAD_DIM = 64
N_LRU_BLOCKS = 8
LRU_BLOCK = D_LRU // N_LRU_BLOCKS
CONV_WIDTH = 4
LRU_C = 8.0
DILATED_PATTERNS = ((128, 1), (512, 4), (2048, 16))
MAX_WINDOW = 2048
D_FF = 2816
D_IN = 2 * D_LRU + 3 * D_ATTN
RMS_EPS = 1e-6
Q_SCALE = HEAD_DIM ** -0.5

LANES = 128
SUBLANES = 8
VMEM_LIMIT_BYTES = 56 * 1024 * 1024

KEYS_PER_PATTERN = 129
QBLK = 128
SUPER = 2048
HEADS_PER_STEP = LANES // HEAD_DIM
NEG_BIG = -1e30
LOG2_E = math.log2(math.e)
MERGE_DIL = 4
LRU_CHUNK = 256
SEG_LEN = LRU_CHUNK // SUBLANES
HEAD_SHIFT = HEAD_DIM.bit_length() - 1
HEADS_SHIFT = N_HEADS.bit_length() - 1
assert all(d & (d - 1) == 0 for _, d in DILATED_PATTERNS) and 1 << HEAD_SHIFT == HEAD_DIM
assert all(w // d + 1 == KEYS_PER_PATTERN for w, d in DILATED_PATTERNS)


def _rms_scale(x):
    return lax.rsqrt(jnp.mean(x * x, axis=-1, keepdims=True) + RMS_EPS)


def _inproj_kernel(x_ref, g_ref, w_ref, u_ref, gate_ref, q_ref, k_ref, v_ref, *window_refs):
    x = x_ref[...]
    xn = (x * _rms_scale(x) * g_ref[...]).astype(BF16)
    proj = jnp.dot(xn, w_ref[...], preferred_element_type=F32)
    u_ref[...] = proj[:, 0:D_LRU]
    gate_ref[...] = proj[:, D_LRU:2 * D_LRU]
    o = 2 * D_LRU
    q_ref[...] = proj[:, o:o + D_ATTN] * Q_SCALE
    k = proj[:, o + D_ATTN:o + 2 * D_ATTN]
    v = proj[:, o + 2 * D_ATTN:o + 3 * D_ATTN]
    k_ref[...] = k
    v_ref[...] = v
    if window_refs:
        kw_ref, vw_ref = window_refs
        kw_ref[...] = k
        vw_ref[...] = v


def _inproj(x2d, g, w_bf, tm, seq_len=None, window=None):
    n = x2d.shape[0]
    assert n % tm == 0
    const = lambda i: (0, 0)
    row = lambda i: (i, 0)
    out_sds = jax.ShapeDtypeStruct((n, D_LRU), F32)
    out_specs = [pl.BlockSpec((tm, D_LRU), row)] * 5
    out_shape = [out_sds] * 5
    if window is not None:
        assert seq_len % tm == 0 and window % tm == 0 and n % seq_len == 0
        per_seq, per_win = seq_len // tm, window // tm

        def win_row(i):
            return (i // per_seq * per_win + jnp.maximum(i % per_seq - (per_seq - per_win), 0), 0)

        out_specs += [pl.BlockSpec((tm, D_ATTN), win_row)] * 2
        out_shape += [jax.ShapeDtypeStruct((n // seq_len * window, D_ATTN), F32)] * 2
    return pl.pallas_call(
        _inproj_kernel,
        grid=(n // tm,),
        in_specs=[
            pl.BlockSpec((tm, D_MODEL), row),
            pl.BlockSpec((1, D_MODEL), const),
            pl.BlockSpec((D_MODEL, D_IN), const, pipeline_mode=pl.Buffered(1)),
        ],
        out_specs=out_specs,
        out_shape=out_shape,
        compiler_params=pltpu.CompilerParams(
            dimension_semantics=("arbitrary",), vmem_limit_bytes=VMEM_LIMIT_BYTES),
        name="inproj",
    )(x2d, g, w_bf)


def _gelu_tanh(x):
    c = math.sqrt(2.0 / math.pi)
    return 0.5 * x * (1.0 + jnp.tanh(c * (x + 0.044715 * (x * x * x))))


def _sigmoid(x):
    return jax.nn.sigmoid(x)


def _lru_gates(uc, wg_ref, ba, bx_bias, neg_c_softplus):
    half = D_LRU // 2
    ucb = uc.astype(BF16)
    pre0 = jnp.dot(ucb[:, 0:half], wg_ref[0], preferred_element_type=F32)
    pre1 = jnp.dot(ucb[:, half:D_LRU], wg_ref[1], preferred_element_type=F32)
    r_pre = jnp.concatenate([pre0[:, 0:half], pre1[:, 0:half]], axis=1) + ba
    i_pre = jnp.concatenate([pre0[:, half:], pre1[:, half:]], axis=1) + bx_bias
    r = _sigmoid(r_pre)
    i = _sigmoid(i_pre)
    log_a = r * neg_c_softplus
    a = jnp.exp(log_a)
    z = -jnp.tanh(log_a) * (a * a + 1.0)
    root = jnp.where(z > 0.0, z * lax.rsqrt(z), 0.0)
    b = root * (i * uc)
    return a, b


def _shift_rows(x, s, fill):
    t = x.shape[0]
    if s % SUBLANES == 0:
        return jnp.concatenate([jnp.full((s, x.shape[1]), fill, x.dtype), x[0:t - s]], axis=0)
    rolled = pltpu.roll(x, s, axis=0)
    row = lax.broadcasted_iota(jnp.int32, x.shape, 0)
    return jnp.where(row < s, fill, rolled)


def _scan_rows(a, b):
    t = a.shape[0]
    s = 1
    while s < t:
        b = a * _shift_rows(b, s, 0.0) + b
        if 2 * s < t:
            a = a * _shift_rows(a, s, 1.0)
        s *= 2
    return b


def _lru_prompt_kernel(u_ref, gate_ref, cprev_ref, h0_ref, cw_ref, cb_ref, wg_ref, ba_ref, bx_ref,
                       ncs_ref, gl_ref, y_ref, conv_ref, hlast_ref, tail_ref, h_ref, ubuf_ref, gbuf_ref,
                       ybuf_ref, *, n_chunks):
    ti = pl.program_id(1)
    lane_chunks = D_LRU // LANES

    def to_segment_major(src_ref, r0, buf_ref):
        for s in range(SUBLANES):
            rows = pl.ds(pl.multiple_of(r0 + s * SEG_LEN, SEG_LEN), SEG_LEN)
            for c in range(lane_chunks):
                buf_ref[c, pl.ds(s, SEG_LEN, stride=SUBLANES), :] = src_ref[0, rows, c * LANES:(c + 1) * LANES]
        return jnp.concatenate([buf_ref[c] for c in range(lane_chunks)], axis=1)
    n_t = pl.num_programs(1)

    @pl.when(ti == 0)
    def _():
        tail_ref[...] = jnp.zeros_like(tail_ref)
        tail_ref[SUBLANES - (CONV_WIDTH - 1):SUBLANES, :] = cprev_ref[0]
        h_ref[...] = jnp.broadcast_to(h0_ref[0], h_ref.shape)

    cw = cw_ref[...]
    cb = cb_ref[...]
    ba = ba_ref[...]
    bxb = bx_ref[...]
    ncs = ncs_ref[...]
    gl = gl_ref[...]
    row8 = lax.broadcasted_iota(jnp.int32, (SUBLANES, D_LRU), 0)
    seg0 = row8 == 0
    n_prev = CONV_WIDTH - 1

    def vrow(x, tau):
        return x[SUBLANES * tau:SUBLANES * (tau + 1)]

    def chunk(ci, carry):
        r0 = pl.multiple_of(ci * LRU_CHUNK, LRU_CHUNK)
        u = to_segment_major(u_ref, r0, ubuf_ref)
        tail = tail_ref[...]
        pre = []
        for k in range(n_prev, 0, -1):
            from_prev_seg = pltpu.roll(vrow(u, SEG_LEN - k), 1, axis=0)
            pre.append(jnp.where(seg0, tail[SUBLANES - k:SUBLANES - k + 1], from_prev_seg))
            tail_ref[SUBLANES - k:SUBLANES - k + 1, :] = u[SUBLANES * (SEG_LEN - k + 1) - 1:
                                                           SUBLANES * (SEG_LEN - k + 1)]
        ext = jnp.concatenate(pre + [u], axis=0)
        uc = cb
        for j in range(CONV_WIDTH):
            uc = uc + cw[j:j + 1, :] * ext[SUBLANES * j:SUBLANES * j + LRU_CHUNK]
        a, b = _lru_gates(uc, wg_ref, ba, bxb, ncs)

        h = vrow(b, 0)
        p = vrow(a, 0)
        hs, ps = [h], [p]
        for tau in range(1, SEG_LEN):
            a_t = vrow(a, tau)
            h = a_t * h + vrow(b, tau)
            p = a_t * p
            hs.append(h)
            ps.append(p)
        h_in = h_ref[0:1, :]
        seg_end = _scan_rows(p, jnp.where(seg0, h + p * h_in, h))
        carry_in = jnp.where(seg0, h_in, pltpu.roll(seg_end, 1, axis=0))
        h_ref[...] = jnp.broadcast_to(seg_end[SUBLANES - 1:SUBLANES], h_ref.shape)
        h_all = jnp.concatenate([hs[t] + ps[t] * carry_in for t in range(SEG_LEN)], axis=0)

        y = h_all * _gelu_tanh(to_segment_major(gate_ref, r0, gbuf_ref))
        yn = y * _rms_scale(y) * gl
        for c in range(lane_chunks):
            ybuf_ref[c] = yn[:, c * LANES:(c + 1) * LANES]
        for s in range(SUBLANES):
            rows = jnp.concatenate([ybuf_ref[c, pl.ds(s, SEG_LEN, stride=SUBLANES), :]
                                    for c in range(lane_chunks)], axis=1)
            y_ref[0, pl.ds(pl.multiple_of(r0 + s * SEG_LEN, SEG_LEN), SEG_LEN), :] = rows.astype(BF16)
        return carry

    lax.fori_loop(0, n_chunks, chunk, 0)

    @pl.when(ti == n_t - 1)
    def _():
        conv_ref[0] = tail_ref[SUBLANES - (CONV_WIDTH - 1):SUBLANES, :]
        hlast_ref[0] = h_ref[0:1, :]


def _lru_prompt(u, gate, conv_prev, h0, lw, tt):
    b, s, _ = u.shape
    assert s % tt == 0 and tt % LRU_CHUNK == 0
    seq = lambda bi, ti: (bi, ti, 0)
    per_b = lambda bi, ti: (bi, 0, 0)
    c2 = lambda bi, ti: (0, 0)
    c3 = lambda bi, ti: (0, 0, 0)
    vec = pl.BlockSpec((1, D_LRU), c2)
    return pl.pallas_call(
        functools.partial(_lru_prompt_kernel, n_chunks=tt // LRU_CHUNK),
        grid=(b, s // tt),
        in_specs=[
            pl.BlockSpec((1, tt, D_LRU), seq),
            pl.BlockSpec((1, tt, D_LRU), seq),
            pl.BlockSpec((1, CONV_WIDTH - 1, D_LRU), per_b),
            pl.BlockSpec((1, 1, D_LRU), per_b),
            pl.BlockSpec((CONV_WIDTH, D_LRU), c2),
            vec,
            pl.BlockSpec((2, D_LRU // 2, D_LRU), c3),
            vec, vec, vec, vec,
        ],
        out_specs=[
            pl.BlockSpec((1, tt, D_LRU), seq),
            pl.BlockSpec((1, CONV_WIDTH - 1, D_LRU), per_b),
            pl.BlockSpec((1, 1, D_LRU), per_b),
        ],
        out_shape=[
            jax.ShapeDtypeStruct((b, s, D_LRU), BF16),
            jax.ShapeDtypeStruct((b, CONV_WIDTH - 1, D_LRU), F32),
            jax.ShapeDtypeStruct((b, 1, D_LRU), F32),
        ],
        scratch_shapes=[pltpu.VMEM((SUBLANES, D_LRU), F32), pltpu.VMEM((SUBLANES, D_LRU), F32)]
        + [pltpu.VMEM((D_LRU // LANES, LRU_CHUNK, LANES), F32)] * 3,
        compiler_params=pltpu.CompilerParams(
            dimension_semantics=("arbitrary", "arbitrary"), vmem_limit_bytes=VMEM_LIMIT_BYTES),
        name="lru_prompt",
    )(u, gate, conv_prev, h0, lw["conv_w"], lw["conv_b"], lw["w_gates"], lw["b_a"], lw["b_x"],
      lw["neg_c_softplus"], lw["g_lru"])


def _lru_sample_kernel(u_ref, gate_ref, cprev_ref, h0_ref, cw_ref, cb_ref, wg_ref, ba_ref, bx_ref,
                       ncs_ref, gl_ref, y_ref, conv_ref, hlast_ref):
    t_len, nb, _ = u_ref.shape
    cw = cw_ref[...]
    hist = [cprev_ref[j] for j in range(CONV_WIDTH - 1)] + [u_ref[t] for t in range(t_len)]
    ucs = []
    for t in range(t_len):
        uc = cb_ref[...]
        for j in range(CONV_WIDTH):
            uc = uc + cw[j:j + 1, :] * hist[t + j]
        ucs.append(uc)
    uc_all = jnp.concatenate(ucs, axis=0)
    a, b = _lru_gates(uc_all, wg_ref, ba_ref[...], bx_ref[...], ncs_ref[...])
    h = h0_ref[...]
    gl = gl_ref[...]
    for t in range(t_len):
        h = a[t * nb:(t + 1) * nb] * h + b[t * nb:(t + 1) * nb]
        y = h * _gelu_tanh(gate_ref[t])
        y_ref[t] = (y * _rms_scale(y) * gl).astype(BF16)
    hlast_ref[...] = h
    for j in range(CONV_WIDTH - 1):
        conv_ref[j] = hist[t_len + j]


def _lru_sample(u_t, gate_t, conv_prev_t, h0, lw):
    t_len, nb, _ = u_t.shape
    return pl.pallas_call(
        _lru_sample_kernel,
        out_shape=[
            jax.ShapeDtypeStruct((t_len, nb, D_LRU), BF16),
            jax.ShapeDtypeStruct((CONV_WIDTH - 1, nb, D_LRU), F32),
            jax.ShapeDtypeStruct((nb, D_LRU), F32),
        ],
        name="lru_sample",
    )(u_t, gate_t, conv_prev_t, h0, lw["conv_w"], lw["conv_b"], lw["w_gates"], lw["b_a"], lw["b_x"],
      lw["neg_c_softplus"], lw["g_lru"])


def _attn_prompt_kernel(q_ref, k_ref, v_ref, o_ref,
                        qp1, qp4, qp16, kb1, kb4, kb16, vb1, vb4, vb16, res_a, res_l, res_m, bias_ref,
                        stage_ref):
    sb = pl.program_id(2)
    qps = (qp1, qp4, qp16)
    kbs = (kb1, kb4, kb16)
    vbs = (vb1, vb4, vb16)
    dils = tuple(d for _, d in DILATED_PATTERNS)
    assert dils == (1, 4, 16)
    lane = lax.broadcasted_iota(jnp.int32, (QBLK, LANES), 1)
    head0 = lane < HEAD_DIM

    qi = lax.broadcasted_iota(jnp.int32, (QBLK, 2 * QBLK), 0)
    ki = lax.broadcasted_iota(jnp.int32, (QBLK, 2 * QBLK), 1)
    delta = qi + QBLK - ki
    band = (delta >= 0) & (delta <= KEYS_PER_PATTERN - 1)
    bias_ref[0] = jnp.where(band, 0.0, NEG_BIG)
    bias_ref[1] = jnp.where(band & (ki >= QBLK), 0.0, NEG_BIG)

    for kb, vb, d in zip(kbs, vbs, dils):
        cls_len = SUPER // d

        @pl.when(sb == 0)
        def _():
            kb[:, 0:QBLK, :] = jnp.zeros((d, QBLK, LANES), BF16)
            vb[:, :, 0:QBLK, 0:LANES] = jnp.zeros((HEADS_PER_STEP, d, QBLK, LANES), BF16)
            h0_all = lax.broadcasted_iota(jnp.int32, (d, QBLK + cls_len, LANES), 2) < HEAD_DIM
            vb[0, :, :, LANES:2 * LANES] = jnp.where(h0_all, 1.0, 0.0).astype(BF16)
            vb[1, :, :, LANES:2 * LANES] = jnp.where(h0_all, 0.0, 1.0).astype(BF16)

        @pl.when(sb > 0)
        def _():
            kb[:, 0:QBLK, :] = kb[:, cls_len:cls_len + QBLK, :]
            vb[:, :, 0:QBLK, 0:LANES] = vb[:, :, cls_len:cls_len + QBLK, 0:LANES]

    def put(x, p, c, which):
        n_rows = x.shape[0]
        if which == "q":
            qps[p][c] = (x * LOG2_E).astype(BF16)
        elif which == "k":
            kbs[p][c, QBLK:QBLK + n_rows, :] = x.astype(BF16)
        else:
            h0 = lax.broadcasted_iota(jnp.int32, x.shape, 1) < HEAD_DIM
            rows = slice(QBLK, QBLK + n_rows)
            vbs[p][0, c, rows, 0:LANES] = jnp.where(h0, x, 0.0).astype(BF16)
            vbs[p][1, c, rows, 0:LANES] = jnp.where(h0, 0.0, x).astype(BF16)

    def tile(p, c, i):
        d = dils[p]
        r0 = i * QBLK
        q_t = qps[p][c, pl.ds(r0, QBLK), :]
        k_t = kbs[p][c, pl.ds(r0, 2 * QBLK), :]
        bias = bias_ref[(sb == 0).astype(jnp.int32)] if i == 0 else bias_ref[0]
        acc = None
        maxes = []
        for h in range(HEADS_PER_STEP):
            hm = head0 if h == 0 else jnp.logical_not(head0)
            qh = jnp.where(hm, q_t, jnp.zeros_like(q_t))
            s = lax.dot_general(qh, k_t, (((1,), (1,)), ((), ())), preferred_element_type=F32)
            s = s + bias
            m = jnp.max(s, axis=-1, keepdims=True)
            pr = jnp.exp2(s - m).astype(BF16)
            v_t = vbs[p][h, c, pl.ds(r0, 2 * QBLK), :]
            pv = jnp.dot(pr, v_t, preferred_element_type=F32)
            acc = pv if acc is None else acc + pv
            maxes.append(jnp.broadcast_to(m, (QBLK, LANES)))
        if d == 1:
            dst = pl.ds(r0, QBLK)
        elif d == MERGE_DIL:
            dst = pl.ds(c * (SUPER // MERGE_DIL) + r0, QBLK)
        else:
            dst = pl.ds((c % MERGE_DIL) * (SUPER // MERGE_DIL) + c // MERGE_DIL, QBLK, stride=MERGE_DIL)
        res_a[p, dst, :] = acc[:, 0:LANES]
        res_l[p, dst, :] = acc[:, LANES:2 * LANES]
        res_m[p, dst, :] = jnp.where(head0, maxes[0], maxes[1])

    sources = (("q", q_ref), ("k", k_ref), ("v", v_ref))
    for which, ref in sources:
        put(ref[0], 0, 0, which)
    for i in range(SUPER // QBLK):
        tile(0, 0, i)
    for idx, (which, ref) in enumerate(sources):
        for c in range(4):
            x = ref[0, pl.ds(c, SUPER // 4, stride=4), :]
            stage_ref[idx, c] = x
            put(x, 1, c, which)
        for r in range(16):
            put(stage_ref[idx, r % 4, pl.ds(r // 4, QBLK, stride=4), :], 2, r, which)

    merge_rows = 256
    for c in range(MERGE_DIL):
        for i in range(SUPER // MERGE_DIL // QBLK):
            tile(1, c, i)
        for a in range(dils[2] // MERGE_DIL):
            tile(2, a * MERGE_DIL + c, 0)
        for n0 in range(0, SUPER // MERGE_DIL, merge_rows):
            in_time = pl.ds(MERGE_DIL * n0 + c, merge_rows, stride=MERGE_DIL)
            in_class = pl.ds(c * (SUPER // MERGE_DIL) + n0, merge_rows)
            rows = (in_time, in_class, in_class)
            ms = [res_m[p, rows[p], :] for p in range(3)]
            m = jnp.maximum(jnp.maximum(ms[0], ms[1]), ms[2])
            ws = [jnp.exp2(x - m) for x in ms]
            den = ws[0] * res_l[0, rows[0], :] + ws[1] * res_l[1, rows[1], :] + ws[2] * res_l[2, rows[2], :]
            num = ws[0] * res_a[0, rows[0], :] + ws[1] * res_a[1, rows[1], :] + ws[2] * res_a[2, rows[2], :]
            o_ref[0, in_time, :] = num * (1.0 / den)


def _attn_prompt(q, k, v):
    b, s, _ = q.shape
    assert s % SUPER == 0
    blk = pl.BlockSpec((1, SUPER, LANES), lambda bi, hp, sb: (bi, sb, hp))
    scratch = []
    for _, d in DILATED_PATTERNS:
        scratch.append(pltpu.VMEM((d, SUPER // d, LANES), BF16))
    for _, d in DILATED_PATTERNS:
        scratch.append(pltpu.VMEM((d, QBLK + SUPER // d, LANES), BF16))
    for _, d in DILATED_PATTERNS:
        scratch.append(pltpu.VMEM((HEADS_PER_STEP, d, QBLK + SUPER // d, 2 * LANES), BF16))
    scratch += [pltpu.VMEM((3, SUPER, LANES), F32)] * 3
    scratch += [pltpu.VMEM((2, QBLK, 2 * QBLK), F32), pltpu.VMEM((3, 4, SUPER // 4, LANES), F32)]
    return pl.pallas_call(
        _attn_prompt_kernel,
        grid=(b, D_ATTN // LANES, s // SUPER),
        in_specs=[blk, blk, blk],
        out_specs=blk,
        out_shape=jax.ShapeDtypeStruct((b, s, D_ATTN), F32),
        scratch_shapes=scratch,
        compiler_params=pltpu.CompilerParams(
            dimension_semantics=("arbitrary", "arbitrary", "arbitrary"),
            vmem_limit_bytes=VMEM_LIMIT_BYTES),
        name="attn_prompt",
    )(q, k, v)


def _attn_sample_kernel(q_ref, kn_ref, vn_ref, kt_ref, vt_ref, o_ref, *, t_len, w_buf):
    rows = N_HEADS * SUBLANES
    n_pat = len(DILATED_PATTERNS)
    pad = jnp.zeros((SUBLANES - t_len, D_ATTN), F32)
    q8 = jnp.concatenate([q_ref[0] * LOG2_E, pad], axis=0)
    kn8 = jnp.concatenate([kn_ref[0], pad], axis=0).astype(BF16)
    vn8 = jnp.concatenate([vn_ref[0], pad], axis=0).astype(BF16)
    head_lanes = lambda h: slice(h * HEAD_DIM, (h + 1) * HEAD_DIM)

    s_c = jnp.concatenate(
        [jnp.dot(q8[:, head_lanes(h)].astype(BF16), kt_ref[0, h].astype(BF16), preferred_element_type=F32)
         for h in range(N_HEADS)], axis=0)
    rid = lax.broadcasted_iota(jnp.int32, (rows, D_ATTN), 0)
    lid = lax.broadcasted_iota(jnp.int32, (rows, D_ATTN), 1)
    qbd = jnp.where((rid >> 3) == (lid >> HEAD_SHIFT), jnp.concatenate([q8] * N_HEADS, axis=0), 0.0)
    s_n = lax.dot_general(qbd.astype(BF16), kn8, (((1,), (1,)), ((), ())), preferred_element_type=F32)

    t_c = lax.broadcasted_iota(jnp.int32, (rows, w_buf), 0) & (SUBLANES - 1)
    d_c = w_buf + t_c - lax.broadcasted_iota(jnp.int32, (rows, w_buf), 1)
    t_n = lax.broadcasted_iota(jnp.int32, (rows, SUBLANES), 0) & (SUBLANES - 1)
    j_n = lax.broadcasted_iota(jnp.int32, (rows, SUBLANES), 1)
    d_n = t_n - j_n
    new_ok = (d_n >= 0) & (j_n < t_len)

    p_cs, p_ns, lses, ls = [], [], [], []
    for window, dil in DILATED_PATTERNS:
        ok_c = ((d_c & (dil - 1)) == 0) & (d_c <= window)
        ok_n = new_ok & ((d_n & (dil - 1)) == 0) & (d_n <= window)
        sc = jnp.where(ok_c, s_c, NEG_BIG)
        sn = jnp.where(ok_n, s_n, NEG_BIG)
        m = jnp.maximum(jnp.max(sc, axis=-1, keepdims=True), jnp.max(sn, axis=-1, keepdims=True))
        pc = jnp.exp2(sc - m)
        pn = jnp.exp2(sn - m)
        l = jnp.sum(pc, axis=-1, keepdims=True) + jnp.sum(pn, axis=-1, keepdims=True)
        p_cs.append(pc)
        p_ns.append(pn)
        ls.append(l)
        lses.append(m + jnp.log2(l))
    m = jnp.maximum(jnp.maximum(lses[0], lses[1]), lses[2])
    es = [jnp.exp2(x - m) for x in lses]
    inv_den = 1.0 / (es[0] + es[1] + es[2])
    wts = [es[p] * inv_den * (1.0 / ls[p]) for p in range(n_pat)]

    o_new = jnp.dot(jnp.concatenate(p_ns, axis=0).astype(BF16), vn8, preferred_element_type=F32)
    for h in range(N_HEADS):
        rs = slice(h * SUBLANES, (h + 1) * SUBLANES)
        p_h = jnp.concatenate([p_cs[p][rs] for p in range(n_pat)], axis=0).astype(BF16)
        o_h = lax.dot_general(p_h, vt_ref[0, h].astype(BF16), (((1,), (1,)), ((), ())),
                              preferred_element_type=F32)
        out = None
        for p in range(n_pat):
            new_part = o_new[p * rows + h * SUBLANES:p * rows + (h + 1) * SUBLANES, head_lanes(h)]
            term = wts[p][rs] * (o_h[p * SUBLANES:(p + 1) * SUBLANES] + new_part)
            out = term if out is None else out + term
        o_ref[0, :, head_lanes(h)] = out[0:t_len]


def _attn_sample(q, k_new, v_new, cache_k, cache_v):
    b, t_len, _ = q.shape
    w_buf = cache_k.shape[1]
    assert w_buf >= MAX_WINDOW and t_len <= SUBLANES
    new = pl.BlockSpec((1, t_len, D_ATTN), lambda bi: (bi, 0, 0))
    cache = pl.BlockSpec((1, N_HEADS, HEAD_DIM, w_buf), lambda bi: (bi, 0, 0, 0))
    row_minor = lambda c: jnp.transpose(c, (0, 2, 3, 1))
    return pl.pallas_call(
        functools.partial(_attn_sample_kernel, t_len=t_len, w_buf=w_buf),
        grid=(b,),
        in_specs=[new, new, new, cache, cache],
        out_specs=new,
        out_shape=jax.ShapeDtypeStruct((b, t_len, D_ATTN), F32),
        compiler_params=pltpu.CompilerParams(
            dimension_semantics=("arbitrary",), vmem_limit_bytes=VMEM_LIMIT_BYTES),
        name="attn_sample",
    )(q, k_new, v_new, row_minor(cache_k), row_minor(cache_v))


def _mix_ffn_kernel(x_ref, yl_ref, ya_ref, ga_ref, wo_ref, gf_ref, wg_ref, wu_ref, wd_ref, gn_ref,
                    y_ref, *, final_norm):
    ya = ya_ref[...]
    ya_n = (ya * _rms_scale(ya) * ga_ref[...]).astype(BF16)
    mix = (jnp.dot(yl_ref[...], wo_ref[0:D_LRU, :], preferred_element_type=F32)
           + jnp.dot(ya_n, wo_ref[D_LRU:D_LRU + D_ATTN, :], preferred_element_type=F32))
    x1 = x_ref[...] + mix
    xn = (x1 * _rms_scale(x1) * gf_ref[...]).astype(BF16)
    hg = jnp.dot(xn, wg_ref[...], preferred_element_type=F32)
    hu = jnp.dot(xn, wu_ref[...], preferred_element_type=F32)
    act = (hg * _sigmoid(hg) * hu).astype(BF16)
    x2 = x1 + jnp.dot(act, wd_ref[...], preferred_element_type=F32)
    if final_norm:
        x2 = x2 * _rms_scale(x2) * gn_ref[...]
    y_ref[...] = x2


def _mix_ffn(x2d, yl, ya, pw, g_final, tm, final_norm):
    n = x2d.shape[0]
    row = lambda i: (i, 0)
    const = lambda i: (0, 0)
    once = pl.Buffered(1)
    return pl.pallas_call(
        functools.partial(_mix_ffn_kernel, final_norm=final_norm),
        grid=(n // tm,),
        in_specs=[
            pl.BlockSpec((tm, D_MODEL), row),
            pl.BlockSpec((tm, D_LRU), row),
            pl.BlockSpec((tm, D_ATTN), row),
            pl.BlockSpec((1, D_ATTN), const),
            pl.BlockSpec((D_LRU + D_ATTN, D_MODEL), const, pipeline_mode=once),
            pl.BlockSpec((1, D_MODEL), const),
            pl.BlockSpec((D_MODEL, D_FF), const, pipeline_mode=once),
            pl.BlockSpec((D_MODEL, D_FF), const, pipeline_mode=once),
            pl.BlockSpec((D_FF, D_MODEL), const, pipeline_mode=once),
            pl.BlockSpec((1, D_MODEL), const),
        ],
        out_specs=pl.BlockSpec((tm, D_MODEL), row),
        out_shape=jax.ShapeDtypeStruct((n, D_MODEL), F32),
        compiler_params=pltpu.CompilerParams(
            dimension_semantics=("arbitrary",), vmem_limit_bytes=VMEM_LIMIT_BYTES),
        name="mix_ffn",
    )(x2d, yl, ya, pw["g_attn"], pw["w_out"], pw["g_ffn"], pw["w_gate"], pw["w_up"], pw["w_down"],
      g_final)


def _block_diag_gate_weights(w_a, w_x):
    half_blocks = N_LRU_BLOCKS // 2
    half = D_LRU // 2

    def bd(w):
        eye = jnp.eye(half_blocks, dtype=w.dtype)
        return jnp.einsum("nij,nm->nimj", w, eye).reshape(half, half)

    halves = [jnp.concatenate([bd(w_a[c * half_blocks:(c + 1) * half_blocks]),
                               bd(w_x[c * half_blocks:(c + 1) * half_blocks])], axis=1)
              for c in range(2)]
    return jnp.stack(halves, axis=0).astype(BF16)


def _row(v):
    return v.reshape(1, -1).astype(F32)


def kernel(x_prompt, x_sample, state_conv, state_lru, cache_k, cache_v, norm_mix, w_in, conv_w, conv_b,
           lru_w_a, lru_b_a, lru_w_x, lru_b_x, lru_lambda, out_norm_lru, out_norm_attn, w_out, norm_ffn,
           w_gate, w_up, w_down, norm_final):
    depth = w_in.shape[0]
    bp, sp, _ = x_prompt.shape
    bs, ts, _ = x_sample.shape
    w_p = min(MAX_WINDOW, sp)
    assert ts >= CONV_WIDTH - 1
    xp = x_prompt.reshape(bp * sp, D_MODEL)
    xs = x_sample.reshape(bs * ts, D_MODEL)
    g_final = _row(norm_final)
    conv_p, lru_p, kw_p, vw_p = [], [], [], []
    conv_s, lru_s, kn_s, vn_s = [], [], [], []
    for l in range(depth):
        last = l == depth - 1
        w_in_bf = w_in[l].astype(BF16)
        g_mix = _row(norm_mix[l])
        lw = dict(
            conv_w=conv_w[l].astype(F32), conv_b=_row(conv_b[l]),
            w_gates=_block_diag_gate_weights(lru_w_a[l], lru_w_x[l]),
            b_a=_row(lru_b_a[l]), b_x=_row(lru_b_x[l]),
            neg_c_softplus=_row(-LRU_C * jax.nn.softplus(-lru_lambda[l].astype(F32))),
            g_lru=_row(out_norm_lru[l]))
        pw = dict(
            g_attn=_row(out_norm_attn[l]), w_out=w_out[l].astype(BF16), g_ffn=_row(norm_ffn[l]),
            w_gate=w_gate[l].astype(BF16), w_up=w_up[l].astype(BF16), w_down=w_down[l].astype(BF16))

        u, gate, q, k, v, k_win, v_win = _inproj(xp, g_mix, w_in_bf, tm=512, seq_len=sp, window=w_p)
        shp = lambda a: a.reshape(bp, sp, -1)
        yl, c_new, h_last = _lru_prompt(
            shp(u), shp(gate), jnp.zeros((bp, CONV_WIDTH - 1, D_LRU), F32),
            jnp.zeros((bp, 1, D_LRU), F32), lw, tt=1024)
        ya = _attn_prompt(shp(q), shp(k), shp(v))
        xp = _mix_ffn(xp, yl.reshape(bp * sp, D_LRU), ya.reshape(bp * sp, D_ATTN), pw, g_final,
                      tm=512, final_norm=last)
        conv_p.append(c_new)
        lru_p.append(h_last.reshape(bp, D_LRU))
        kw_p.append(k_win.reshape(bp, w_p, N_HEADS, HEAD_DIM))
        vw_p.append(v_win.reshape(bp, w_p, N_HEADS, HEAD_DIM))

        u, gate, q, k, v = _inproj(xs, g_mix, w_in_bf, tm=bs * ts)
        tmaj = lambda a: a.reshape(bs, ts, -1).transpose(1, 0, 2)
        yl_t, c_new_t, h_last = _lru_sample(
            tmaj(u), tmaj(gate), state_conv[l].astype(F32).transpose(1, 0, 2),
            state_lru[l].astype(F32), lw)
        seq = lambda a: a.reshape(bs, ts, D_ATTN)
        ya = _attn_sample(seq(q), seq(k), seq(v), cache_k[l].astype(F32), cache_v[l].astype(F32))
        xs = _mix_ffn(xs, yl_t.transpose(1, 0, 2).reshape(bs * ts, D_LRU), ya.reshape(bs * ts, D_ATTN),
                      pw, g_final, tm=bs * ts, final_norm=last)
        conv_s.append(c_new_t.transpose(1, 0, 2))
        lru_s.append(h_last)
        kn_s.append(k.reshape(bs, ts, N_HEADS, HEAD_DIM))
        vn_s.append(v.reshape(bs, ts, N_HEADS, HEAD_DIM))
    return (xp.reshape(bp, sp, D_MODEL), xs.reshape(bs, ts, D_MODEL),
            jnp.stack(conv_p, 0), jnp.stack(lru_p, 0), jnp.stack(kw_p, 0), jnp.stack(vw_p, 0),
            jnp.stack(conv_s, 0), jnp.stack(lru_s, 0), jnp.stack(kn_s, 0), jnp.stack(vn_s, 0))
```

```python
import functools
import math

import jax
import jax.numpy as jnp
from jax import lax
from jax.experimental import pallas as pl
from jax.experimental.pallas import tpu as pltpu

F32 = jnp.float32
BF16 = jnp.bfloat16

D_MODEL = 1024
D_LRU = 512
D_ATTN = 512
N_HEADS = 8
HEAD_DIM = 64
N_LRU_BLOCKS = 8
LRU_BLOCK = D_LRU // N_LRU_BLOCKS
CONV_WIDTH = 4
LRU_C = 8.0
DILATED_PATTERNS = ((128, 1), (512, 4), (2048, 16))
MAX_WINDOW = 2048
D_FF = 2816
D_IN = 2 * D_LRU + 3 * D_ATTN
RMS_EPS = 1e-6
Q_SCALE = HEAD_DIM ** -0.5

LANES = 128
SUBLANES = 8
VMEM_LIMIT_BYTES = 56 * 1024 * 1024

KEYS_PER_PATTERN = 129
QBLK = 128
SUPER = 2048
HEADS_PER_STEP = LANES // HEAD_DIM
NEG_BIG = -1e30
LOG2_E = math.log2(math.e)
MERGE_DIL = 4
LRU_CHUNK = 256
SEG_LEN = LRU_CHUNK // SUBLANES
HEAD_SHIFT = HEAD_DIM.bit_length() - 1
HEADS_SHIFT = N_HEADS.bit_length() - 1
assert all(d & (d - 1) == 0 for _, d in DILATED_PATTERNS) and 1 << HEAD_SHIFT == HEAD_DIM
assert all(w // d + 1 == KEYS_PER_PATTERN for w, d in DILATED_PATTERNS)


def _rms_scale(x):
    return lax.rsqrt(jnp.mean(x * x, axis=-1, keepdims=True) + RMS_EPS)


def _gelu_tanh(x):
    c = math.sqrt(2.0 / math.pi)
    return 0.5 * x * (1.0 + jnp.tanh(c * (x + 0.044715 * (x * x * x))))


def _normed(x_ref, g_ref):
    x = x_ref[...]
    return (x * _rms_scale(x) * g_ref[...]).astype(BF16)


def _project(xn, w_ref, first_col, width=D_LRU):
    return jnp.dot(xn, w_ref[:, first_col:first_col + width], preferred_element_type=F32)


def _project_all(x_ref, g_ref, w_ref, u_ref, gate_ref, q_ref, k_ref, v_ref):
    xn = _normed(x_ref, g_ref)
    u_ref[...] = _project(xn, w_ref, 0)
    gate_ref[...] = _project(xn, w_ref, D_LRU)
    o = 2 * D_LRU
    q_ref[...] = _project(xn, w_ref, o, D_ATTN) * Q_SCALE
    k = _project(xn, w_ref, o + D_ATTN, D_ATTN)
    v = _project(xn, w_ref, o + 2 * D_ATTN, D_ATTN)
    k_ref[...] = k
    v_ref[...] = v
    return k, v


def _inproj_kernel(x_ref, g_ref, w_ref, u_ref, gate_ref, q_ref, k_ref, v_ref):
    _project_all(x_ref, g_ref, w_ref, u_ref, gate_ref, q_ref, k_ref, v_ref)


def _inproj_window_kernel(x_ref, g_ref, w_ref, u_ref, gate_ref, q_ref, k_ref, v_ref, kw_ref, vw_ref):
    k, v = _project_all(x_ref, g_ref, w_ref, u_ref, gate_ref, q_ref, k_ref, v_ref)
    kw_ref[0] = k.T
    vw_ref[0] = v.T


def _inproj(x2d, g, w_bf, tm):
    n = x2d.shape[0]
    assert n % tm == 0
    const = lambda i: (0, 0)
    row = lambda i: (i, 0)
    return pl.pallas_call(
        _inproj_kernel,
        grid=(n // tm,),
        in_specs=[
            pl.BlockSpec((tm, D_MODEL), row),
            pl.BlockSpec((1, D_MODEL), const),
            pl.BlockSpec((D_MODEL, D_IN), const, pipeline_mode=pl.Buffered(1)),
        ],
        out_specs=[pl.BlockSpec((tm, D_LRU), row)] * 5,
        out_shape=[jax.ShapeDtypeStruct((n, D_LRU), F32)] * 5,
        compiler_params=pltpu.CompilerParams(
            dimension_semantics=("arbitrary",), vmem_limit_bytes=VMEM_LIMIT_BYTES),
        name="inproj",
    )(x2d, g, w_bf)


def _inproj_window(x2d, g, w_bf, tm, seq_len, window):
    n = x2d.shape[0]
    assert n % seq_len == 0 and seq_len % tm == 0 and window % tm == 0
    per_seq, per_win, n_seq = seq_len // tm, window // tm, n // seq_len
    const = lambda i: (0, 0)
    row = lambda i: (i, 0)
    win_block = lambda i: (i // per_seq, 0, jnp.maximum(i % per_seq - (per_seq - per_win), 0))
    return pl.pallas_call(
        _inproj_window_kernel,
        grid=(n // tm,),
        in_specs=[
            pl.BlockSpec((tm, D_MODEL), row),
            pl.BlockSpec((1, D_MODEL), const),
            pl.BlockSpec((D_MODEL, D_IN), const, pipeline_mode=pl.Buffered(1)),
        ],
        out_specs=[pl.BlockSpec((tm, D_LRU), row)] * 5 + [pl.BlockSpec((1, D_ATTN, tm), win_block)] * 2,
        out_shape=[jax.ShapeDtypeStruct((n, D_LRU), F32)] * 5
        + [jax.ShapeDtypeStruct((n_seq, D_ATTN, window), F32)] * 2,
        compiler_params=pltpu.CompilerParams(
            dimension_semantics=("arbitrary",), vmem_limit_bytes=VMEM_LIMIT_BYTES),
        name="inproj_window",
    )(x2d, g, w_bf)


def _sigmoid(x):
    return jax.nn.sigmoid(x)


def _lru_gates(uc, wg_ref, ba, bx_bias, neg_c_softplus):
    half = D_LRU // 2
    ucb = uc.astype(BF16)
    pre0 = jnp.dot(ucb[:, 0:half], wg_ref[0], preferred_element_type=F32)
    pre1 = jnp.dot(ucb[:, half:D_LRU], wg_ref[1], preferred_element_type=F32)
    r_pre = jnp.concatenate([pre0[:, 0:half], pre1[:, 0:half]], axis=1) + ba
    i_pre = jnp.concatenate([pre0[:, half:], pre1[:, half:]], axis=1) + bx_bias
    r = _sigmoid(r_pre)
    i = _sigmoid(i_pre)
    log_a = r * neg_c_softplus
    a = jnp.exp(log_a)
    z = -jnp.tanh(log_a) * (a * a + 1.0)
    root = jnp.where(z > 0.0, z * lax.rsqrt(z), 0.0)
    b = root * (i * uc)
    return a, b


def _shift_rows(x, s, fill):
    t = x.shape[0]
    if s % SUBLANES == 0:
        return jnp.concatenate([jnp.full((s, x.shape[1]), fill, x.dtype), x[0:t - s]], axis=0)
    rolled = pltpu.roll(x, s, axis=0)
    row = lax.broadcasted_iota(jnp.int32, x.shape, 0)
    return jnp.where(row < s, fill, rolled)


def _scan_rows(a, b):
    t = a.shape[0]
    s = 1
    while s < t:
        b = a * _shift_rows(b, s, 0.0) + b
        if 2 * s < t:
            a = a * _shift_rows(a, s, 1.0)
        s *= 2
    return b


def _lru_prompt_kernel(u_ref, gate_ref, cprev_ref, h0_ref, cw_ref, cb_ref, wg_ref, ba_ref, bx_ref,
                       ncs_ref, gl_ref, y_ref, conv_ref, hlast_ref, tail_ref, h_ref, ubuf_ref, gbuf_ref,
                       ybuf_ref, *, n_chunks):
    ti = pl.program_id(1)
    lane_chunks = D_LRU // LANES

    def to_segment_major(src_ref, r0, buf_ref):
        for s in range(SUBLANES):
            rows = pl.ds(pl.multiple_of(r0 + s * SEG_LEN, SEG_LEN), SEG_LEN)
            for c in range(lane_chunks):
                buf_ref[c, pl.ds(s, SEG_LEN, stride=SUBLANES), :] = src_ref[0, rows, c * LANES:(c + 1) * LANES]
        return jnp.concatenate([buf_ref[c] for c in range(lane_chunks)], axis=1)
    n_t = pl.num_programs(1)

    @pl.when(ti == 0)
    def _():
        tail_ref[...] = jnp.zeros_like(tail_ref)
        tail_ref[SUBLANES - (CONV_WIDTH - 1):SUBLANES, :] = cprev_ref[0]
        h_ref[...] = jnp.broadcast_to(h0_ref[0], h_ref.shape)

    cw = cw_ref[...]
    cb = cb_ref[...]
    ba = ba_ref[...]
    bxb = bx_ref[...]
    ncs = ncs_ref[...]
    gl = gl_ref[...]
    row8 = lax.broadcasted_iota(jnp.int32, (SUBLANES, D_LRU), 0)
    seg0 = row8 == 0
    n_prev = CONV_WIDTH - 1

    def vrow(x, tau):
        return x[SUBLANES * tau:SUBLANES * (tau + 1)]

    def chunk(ci, carry):
        r0 = pl.multiple_of(ci * LRU_CHUNK, LRU_CHUNK)
        u = to_segment_major(u_ref, r0, ubuf_ref)
        tail = tail_ref[...]
        pre = []
        for k in range(n_prev, 0, -1):
            from_prev_seg = pltpu.roll(vrow(u, SEG_LEN - k), 1, axis=0)
            pre.append(jnp.where(seg0, tail[SUBLANES - k:SUBLANES - k + 1], from_prev_seg))
            tail_ref[SUBLANES - k:SUBLANES - k + 1, :] = u[SUBLANES * (SEG_LEN - k + 1) - 1:
                                                           SUBLANES * (SEG_LEN - k + 1)]
        ext = jnp.concatenate(pre + [u], axis=0)
        uc = cb
        for j in range(CONV_WIDTH):
            uc = uc + cw[j:j + 1, :] * ext[SUBLANES * j:SUBLANES * j + LRU_CHUNK]
        a, b = _lru_gates(uc, wg_ref, ba, bxb, ncs)

        h = vrow(b, 0)
        p = vrow(a, 0)
        hs, ps = [h], [p]
        for tau in range(1, SEG_LEN):
            a_t = vrow(a, tau)
            h = a_t * h + vrow(b, tau)
            p = a_t * p
            hs.append(h)
            ps.append(p)
        h_in = h_ref[0:1, :]
        seg_end = _scan_rows(p, jnp.where(seg0, h + p * h_in, h))
        carry_in = jnp.where(seg0, h_in, pltpu.roll(seg_end, 1, axis=0))
        h_ref[...] = jnp.broadcast_to(seg_end[SUBLANES - 1:SUBLANES], h_ref.shape)
        h_all = jnp.concatenate([hs[t] + ps[t] * carry_in for t in range(SEG_LEN)], axis=0)

        y = h_all * _gelu_tanh(to_segment_major(gate_ref, r0, gbuf_ref))
        yn = y * _rms_scale(y) * gl
        for c in range(lane_chunks):
            ybuf_ref[c] = yn[:, c * LANES:(c + 1) * LANES]
        for s in range(SUBLANES):
            rows = jnp.concatenate([ybuf_ref[c, pl.ds(s, SEG_LEN, stride=SUBLANES), :]
                                    for c in range(lane_chunks)], axis=1)
            y_ref[0, pl.ds(pl.multiple_of(r0 + s * SEG_LEN, SEG_LEN), SEG_LEN), :] = rows.astype(BF16)
        return carry

    lax.fori_loop(0, n_chunks, chunk, 0)

    @pl.when(ti == n_t - 1)
    def _():
        conv_ref[0] = tail_ref[SUBLANES - (CONV_WIDTH - 1):SUBLANES, :]
        hlast_ref[0] = h_ref[0:1, :]


def _lru_prompt(u, gate, conv_prev, h0, lw, tt):
    b, s, _ = u.shape
    assert s % tt == 0 and tt % LRU_CHUNK == 0
    seq = lambda bi, ti: (bi, ti, 0)
    per_b = lambda bi, ti: (bi, 0, 0)
    c2 = lambda bi, ti: (0, 0)
    c3 = lambda bi, ti: (0, 0, 0)
    vec = pl.BlockSpec((1, D_LRU), c2)
    return pl.pallas_call(
        functools.partial(_lru_prompt_kernel, n_chunks=tt // LRU_CHUNK),
        grid=(b, s // tt),
        in_specs=[
            pl.BlockSpec((1, tt, D_LRU), seq),
            pl.BlockSpec((1, tt, D_LRU), seq),
            pl.BlockSpec((1, CONV_WIDTH - 1, D_LRU), per_b),
            pl.BlockSpec((1, 1, D_LRU), per_b),
            pl.BlockSpec((CONV_WIDTH, D_LRU), c2),
            vec,
            pl.BlockSpec((2, D_LRU // 2, D_LRU), c3),
            vec, vec, vec, vec,
        ],
        out_specs=[
            pl.BlockSpec((1, tt, D_LRU), seq),
            pl.BlockSpec((1, CONV_WIDTH - 1, D_LRU), per_b),
            pl.BlockSpec((1, 1, D_LRU), per_b),
        ],
        out_shape=[
            jax.ShapeDtypeStruct((b, s, D_LRU), BF16),
            jax.ShapeDtypeStruct((b, CONV_WIDTH - 1, D_LRU), F32),
            jax.ShapeDtypeStruct((b, 1, D_LRU), F32),
        ],
        scratch_shapes=[pltpu.VMEM((SUBLANES, D_LRU), F32), pltpu.VMEM((SUBLANES, D_LRU), F32)]
        + [pltpu.VMEM((D_LRU // LANES, LRU_CHUNK, LANES), F32)] * 3,
        compiler_params=pltpu.CompilerParams(
            dimension_semantics=("arbitrary", "arbitrary"), vmem_limit_bytes=VMEM_LIMIT_BYTES),
        name="lru_prompt",
    )(u, gate, conv_prev, h0, lw["conv_w"], lw["conv_b"], lw["w_gates"], lw["b_a"], lw["b_x"],
      lw["neg_c_softplus"], lw["g_lru"])


def _lru_sample_kernel(u_ref, gate_ref, cprev_ref, h0_ref, cw_ref, cb_ref, wg_ref, ba_ref, bx_ref,
                       ncs_ref, gl_ref, y_ref, conv_ref, hlast_ref):
    t_len, nb, _ = u_ref.shape
    cw = cw_ref[...]
    hist = [cprev_ref[j] for j in range(CONV_WIDTH - 1)] + [u_ref[t] for t in range(t_len)]
    ucs = []
    for t in range(t_len):
        uc = cb_ref[...]
        for j in range(CONV_WIDTH):
            uc = uc + cw[j:j + 1, :] * hist[t + j]
        ucs.append(uc)
    uc_all = jnp.concatenate(ucs, axis=0)
    a, b = _lru_gates(uc_all, wg_ref, ba_ref[...], bx_ref[...], ncs_ref[...])
    h = h0_ref[...]
    gl = gl_ref[...]
    for t in range(t_len):
        h = a[t * nb:(t + 1) * nb] * h + b[t * nb:(t + 1) * nb]
        y = h * _gelu_tanh(gate_ref[t])
        y_ref[t] = (y * _rms_scale(y) * gl).astype(BF16)
    hlast_ref[...] = h
    for j in range(CONV_WIDTH - 1):
        conv_ref[j] = hist[t_len + j]


def _lru_sample(u_t, gate_t, conv_prev_t, h0, lw):
    t_len, nb, _ = u_t.shape
    return pl.pallas_call(
        _lru_sample_kernel,
        out_shape=[
            jax.ShapeDtypeStruct((t_len, nb, D_LRU), BF16),
            jax.ShapeDtypeStruct((CONV_WIDTH - 1, nb, D_LRU), F32),
            jax.ShapeDtypeStruct((nb, D_LRU), F32),
        ],
        name="lru_sample",
    )(u_t, gate_t, conv_prev_t, h0, lw["conv_w"], lw["conv_b"], lw["w_gates"], lw["b_a"], lw["b_x"],
      lw["neg_c_softplus"], lw["g_lru"])


def _attn_prompt_kernel(q_ref, k_ref, v_ref, o_ref,
                        qp1, qp4, qp16, kb1, kb4, kb16, vb1, vb4, vb16, res_a, res_l, res_m, bias_ref,
                        stage_ref):
    sb = pl.program_id(2)
    qps = (qp1, qp4, qp16)
    kbs = (kb1, kb4, kb16)
    vbs = (vb1, vb4, vb16)
    dils = tuple(d for _, d in DILATED_PATTERNS)
    assert dils == (1, 4, 16)
    lane = lax.broadcasted_iota(jnp.int32, (QBLK, LANES), 1)
    head0 = lane < HEAD_DIM

    qi = lax.broadcasted_iota(jnp.int32, (QBLK, 2 * QBLK), 0)
    ki = lax.broadcasted_iota(jnp.int32, (QBLK, 2 * QBLK), 1)
    delta = qi + QBLK - ki
    band = (delta >= 0) & (delta <= KEYS_PER_PATTERN - 1)
    bias_ref[0] = jnp.where(band, 0.0, NEG_BIG)
    bias_ref[1] = jnp.where(band & (ki >= QBLK), 0.0, NEG_BIG)

    for kb, vb, d in zip(kbs, vbs, dils):
        cls_len = SUPER // d

        @pl.when(sb == 0)
        def _():
            kb[:, 0:QBLK, :] = jnp.zeros((d, QBLK, LANES), BF16)
            vb[:, :, 0:QBLK, 0:LANES] = jnp.zeros((HEADS_PER_STEP, d, QBLK, LANES), BF16)
            h0_all = lax.broadcasted_iota(jnp.int32, (d, QBLK + cls_len, LANES), 2) < HEAD_DIM
            vb[0, :, :, LANES:2 * LANES] = jnp.where(h0_all, 1.0, 0.0).astype(BF16)
            vb[1, :, :, LANES:2 * LANES] = jnp.where(h0_all, 0.0, 1.0).astype(BF16)

        @pl.when(sb > 0)
        def _():
            kb[:, 0:QBLK, :] = kb[:, cls_len:cls_len + QBLK, :]
            vb[:, :, 0:QBLK, 0:LANES] = vb[:, :, cls_len:cls_len + QBLK, 0:LANES]

    def put(x, p, c, which):
        n_rows = x.shape[0]
        if which == "q":
            qps[p][c] = (x * LOG2_E).astype(BF16)
        elif which == "k":
            kbs[p][c, QBLK:QBLK + n_rows, :] = x.astype(BF16)
        else:
            h0 = lax.broadcasted_iota(jnp.int32, x.shape, 1) < HEAD_DIM
            rows = slice(QBLK, QBLK + n_rows)
            vbs[p][0, c, rows, 0:LANES] = jnp.where(h0, x, 0.0).astype(BF16)
            vbs[p][1, c, rows, 0:LANES] = jnp.where(h0, 0.0, x).astype(BF16)

    def tile(p, c, i):
        d = dils[p]
        r0 = i * QBLK
        q_t = qps[p][c, pl.ds(r0, QBLK), :]
        k_t = kbs[p][c, pl.ds(r0, 2 * QBLK), :]
        bias = bias_ref[(sb == 0).astype(jnp.int32)] if i == 0 else bias_ref[0]
        acc = None
        maxes = []
        for h in range(HEADS_PER_STEP):
            hm = head0 if h == 0 else jnp.logical_not(head0)
            qh = jnp.where(hm, q_t, jnp.zeros_like(q_t))
            s = lax.dot_general(qh, k_t, (((1,), (1,)), ((), ())), preferred_element_type=F32)
            s = s + bias
            m = jnp.max(s, axis=-1, keepdims=True)
            pr = jnp.exp2(s - m).astype(BF16)
            v_t = vbs[p][h, c, pl.ds(r0, 2 * QBLK), :]
            pv = jnp.dot(pr, v_t, preferred_element_type=F32)
            acc = pv if acc is None else acc + pv
            maxes.append(jnp.broadcast_to(m, (QBLK, LANES)))
        if d == 1:
            dst = pl.ds(r0, QBLK)
        elif d == MERGE_DIL:
            dst = pl.ds(c * (SUPER // MERGE_DIL) + r0, QBLK)
        else:
            dst = pl.ds((c % MERGE_DIL) * (SUPER // MERGE_DIL) + c // MERGE_DIL, QBLK, stride=MERGE_DIL)
        res_a[p, dst, :] = acc[:, 0:LANES]
        res_l[p, dst, :] = acc[:, LANES:2 * LANES]
        res_m[p, dst, :] = jnp.where(head0, maxes[0], maxes[1])

    sources = (("q", q_ref), ("k", k_ref), ("v", v_ref))
    for which, ref in sources:
        put(ref[0], 0, 0, which)
    for i in range(SUPER // QBLK):
        tile(0, 0, i)
    for idx, (which, ref) in enumerate(sources):
        for c in range(4):
            x = ref[0, pl.ds(c, SUPER // 4, stride=4), :]
            stage_ref[idx, c] = x
            put(x, 1, c, which)
        for r in range(16):
            put(stage_ref[idx, r % 4, pl.ds(r // 4, QBLK, stride=4), :], 2, r, which)

    merge_rows = 256
    for c in range(MERGE_DIL):
        for i in range(SUPER // MERGE_DIL // QBLK):
            tile(1, c, i)
        for a in range(dils[2] // MERGE_DIL):
            tile(2, a * MERGE_DIL + c, 0)
        for n0 in range(0, SUPER // MERGE_DIL, merge_rows):
            in_time = pl.ds(MERGE_DIL * n0 + c, merge_rows, stride=MERGE_DIL)
            in_class = pl.ds(c * (SUPER // MERGE_DIL) + n0, merge_rows)
            rows = (in_time, in_class, in_class)
            ms = [res_m[p, rows[p], :] for p in range(3)]
            m = jnp.maximum(jnp.maximum(ms[0], ms[1]), ms[2])
            ws = [jnp.exp2(x - m) for x in ms]
            den = ws[0] * res_l[0, rows[0], :] + ws[1] * res_l[1, rows[1], :] + ws[2] * res_l[2, rows[2], :]
            num = ws[0] * res_a[0, rows[0], :] + ws[1] * res_a[1, rows[1], :] + ws[2] * res_a[2, rows[2], :]
            o_ref[0, in_time, :] = num * (1.0 / den)


def _attn_prompt(q, k, v):
    b, s, _ = q.shape
    assert s % SUPER == 0
    blk = pl.BlockSpec((1, SUPER, LANES), lambda bi, hp, sb: (bi, sb, hp))
    scratch = []
    for _, d in DILATED_PATTERNS:
        scratch.append(pltpu.VMEM((d, SUPER // d, LANES), BF16))
    for _, d in DILATED_PATTERNS:
        scratch.append(pltpu.VMEM((d, QBLK + SUPER // d, LANES), BF16))
    for _, d in DILATED_PATTERNS:
        scratch.append(pltpu.VMEM((HEADS_PER_STEP, d, QBLK + SUPER // d, 2 * LANES), BF16))
    scratch += [pltpu.VMEM((3, SUPER, LANES), F32)] * 3
    scratch += [pltpu.VMEM((2, QBLK, 2 * QBLK), F32), pltpu.VMEM((3, 4, SUPER // 4, LANES), F32)]
    return pl.pallas_call(
        _attn_prompt_kernel,
        grid=(b, D_ATTN // LANES, s // SUPER),
        in_specs=[blk, blk, blk],
        out_specs=blk,
        out_shape=jax.ShapeDtypeStruct((b, s, D_ATTN), F32),
        scratch_shapes=scratch,
        compiler_params=pltpu.CompilerParams(
            dimension_semantics=("arbitrary", "arbitrary", "arbitrary"),
            vmem_limit_bytes=VMEM_LIMIT_BYTES),
        name="attn_prompt",
    )(q, k, v)


def _attn_sample_kernel(q_ref, kn_ref, vn_ref, kt_ref, vt_ref, o_ref, *, t_len, w_buf):
    rows = N_HEADS * SUBLANES
    n_pat = len(DILATED_PATTERNS)
    pad = jnp.zeros((SUBLANES - t_len, D_ATTN), F32)
    q8 = jnp.concatenate([q_ref[0] * LOG2_E, pad], axis=0)
    kn8 = jnp.concatenate([kn_ref[0], pad], axis=0).astype(BF16)
    vn8 = jnp.concatenate([vn_ref[0], pad], axis=0).astype(BF16)
    head_lanes = lambda h: slice(h * HEAD_DIM, (h + 1) * HEAD_DIM)

    s_c = jnp.concatenate(
        [jnp.dot(q8[:, head_lanes(h)].astype(BF16), kt_ref[0, h].astype(BF16), preferred_element_type=F32)
         for h in range(N_HEADS)], axis=0)
    rid = lax.broadcasted_iota(jnp.int32, (rows, D_ATTN), 0)
    lid = lax.broadcasted_iota(jnp.int32, (rows, D_ATTN), 1)
    qbd = jnp.where((rid >> 3) == (lid >> HEAD_SHIFT), jnp.concatenate([q8] * N_HEADS, axis=0), 0.0)
    s_n = lax.dot_general(qbd.astype(BF16), kn8, (((1,), (1,)), ((), ())), preferred_element_type=F32)

    t_c = lax.broadcasted_iota(jnp.int32, (rows, w_buf), 0) & (SUBLANES - 1)
    d_c = w_buf + t_c - lax.broadcasted_iota(jnp.int32, (rows, w_buf), 1)
    t_n = lax.broadcasted_iota(jnp.int32, (rows, SUBLANES), 0) & (SUBLANES - 1)
    j_n = lax.broadcasted_iota(jnp.int32, (rows, SUBLANES), 1)
    d_n = t_n - j_n
    new_ok = (d_n >= 0) & (j_n < t_len)

    p_cs, p_ns, lses, ls = [], [], [], []
    for window, dil in DILATED_PATTERNS:
        ok_c = ((d_c & (dil - 1)) == 0) & (d_c <= window)
        ok_n = new_ok & ((d_n & (dil - 1)) == 0) & (d_n <= window)
        sc = jnp.where(ok_c, s_c, NEG_BIG)
        sn = jnp.where(ok_n, s_n, NEG_BIG)
        m = jnp.maximum(jnp.max(sc, axis=-1, keepdims=True), jnp.max(sn, axis=-1, keepdims=True))
        pc = jnp.exp2(sc - m)
        pn = jnp.exp2(sn - m)
        l = jnp.sum(pc, axis=-1, keepdims=True) + jnp.sum(pn, axis=-1, keepdims=True)
        p_cs.append(pc)
        p_ns.append(pn)
        ls.append(l)
        lses.append(m + jnp.log2(l))
    m = jnp.maximum(jnp.maximum(lses[0], lses[1]), lses[2])
    es = [jnp.exp2(x - m) for x in lses]
    inv_den = 1.0 / (es[0] + es[1] + es[2])
    wts = [es[p] * inv_den * (1.0 / ls[p]) for p in range(n_pat)]

    o_new = jnp.dot(jnp.concatenate(p_ns, axis=0).astype(BF16), vn8, preferred_element_type=F32)
    for h in range(N_HEADS):
        rs = slice(h * SUBLANES, (h + 1) * SUBLANES)
        p_h = jnp.concatenate([p_cs[p][rs] for p in range(n_pat)], axis=0).astype(BF16)
        o_h = lax.dot_general(p_h, vt_ref[0, h].astype(BF16), (((1,), (1,)), ((), ())),
                              preferred_element_type=F32)
        out = None
        for p in range(n_pat):
            new_part = o_new[p * rows + h * SUBLANES:p * rows + (h + 1) * SUBLANES, head_lanes(h)]
            term = wts[p][rs] * (o_h[p * SUBLANES:(p + 1) * SUBLANES] + new_part)
            out = term if out is None else out + term
        o_ref[0, :, head_lanes(h)] = out[0:t_len]


def _attn_sample(q, k_new, v_new, cache_k, cache_v):
    b, t_len, _ = q.shape
    w_buf = cache_k.shape[1]
    assert w_buf >= MAX_WINDOW and t_len <= SUBLANES
    new = pl.BlockSpec((1, t_len, D_ATTN), lambda bi: (bi, 0, 0))
    cache = pl.BlockSpec((1, N_HEADS, HEAD_DIM, w_buf), lambda bi: (bi, 0, 0, 0))
    row_minor = lambda c: jnp.transpose(c, (0, 2, 3, 1))
    return pl.pallas_call(
        functools.partial(_attn_sample_kernel, t_len=t_len, w_buf=w_buf),
        grid=(b,),
        in_specs=[new, new, new, cache, cache],
        out_specs=new,
        out_shape=jax.ShapeDtypeStruct((b, t_len, D_ATTN), F32),
        compiler_params=pltpu.CompilerParams(
            dimension_semantics=("arbitrary",), vmem_limit_bytes=VMEM_LIMIT_BYTES),
        name="attn_sample",
    )(q, k_new, v_new, row_minor(cache_k), row_minor(cache_v))


def _mix_ffn_kernel(x_ref, yl_ref, ya_ref, ga_ref, wo_ref, gf_ref, wg_ref, wu_ref, wd_ref, gn_ref,
                    y_ref, *, final_norm):
    ya = ya_ref[...]
    ya_n = (ya * _rms_scale(ya) * ga_ref[...]).astype(BF16)
    mix = (jnp.dot(yl_ref[...], wo_ref[0:D_LRU, :], preferred_element_type=F32)
           + jnp.dot(ya_n, wo_ref[D_LRU:D_LRU + D_ATTN, :], preferred_element_type=F32))
    x1 = x_ref[...] + mix
    xn = (x1 * _rms_scale(x1) * gf_ref[...]).astype(BF16)
    hg = jnp.dot(xn, wg_ref[...], preferred_element_type=F32)
    hu = jnp.dot(xn, wu_ref[...], preferred_element_type=F32)
    act = (hg * _sigmoid(hg) * hu).astype(BF16)
    x2 = x1 + jnp.dot(act, wd_ref[...], preferred_element_type=F32)
    if final_norm:
        x2 = x2 * _rms_scale(x2) * gn_ref[...]
    y_ref[...] = x2


def _mix_ffn(x2d, yl, ya, pw, g_final, tm, final_norm):
    n = x2d.shape[0]
    row = lambda i: (i, 0)
    const = lambda i: (0, 0)
    once = pl.Buffered(1)
    return pl.pallas_call(
        functools.partial(_mix_ffn_kernel, final_norm=final_norm),
        grid=(n // tm,),
        in_specs=[
            pl.BlockSpec((tm, D_MODEL), row),
            pl.BlockSpec((tm, D_LRU), row),
            pl.BlockSpec((tm, D_ATTN), row),
            pl.BlockSpec((1, D_ATTN), const),
            pl.BlockSpec((D_LRU + D_ATTN, D_MODEL), const, pipeline_mode=once),
            pl.BlockSpec((1, D_MODEL), const),
            pl.BlockSpec((D_MODEL, D_FF), const, pipeline_mode=once),
            pl.BlockSpec((D_MODEL, D_FF), const, pipeline_mode=once),
            pl.BlockSpec((D_FF, D_MODEL), const, pipeline_mode=once),
            pl.BlockSpec((1, D_MODEL), const),
        ],
        out_specs=pl.BlockSpec((tm, D_MODEL), row),
        out_shape=jax.ShapeDtypeStruct((n, D_MODEL), F32),
        compiler_params=pltpu.CompilerParams(
            dimension_semantics=("arbitrary",), vmem_limit_bytes=VMEM_LIMIT_BYTES),
        name="mix_ffn",
    )(x2d, yl, ya, pw["g_attn"], pw["w_out"], pw["g_ffn"], pw["w_gate"], pw["w_up"], pw["w_down"],
      g_final)


def _block_diag_gate_weights(w_a, w_x):
    half_blocks = N_LRU_BLOCKS // 2
    half = D_LRU // 2

    def bd(w):
        eye = jnp.eye(half_blocks, dtype=w.dtype)
        return jnp.einsum("nij,nm->nimj", w, eye).reshape(half, half)

    halves = [jnp.concatenate([bd(w_a[c * half_blocks:(c + 1) * half_blocks]),
                               bd(w_x[c * half_blocks:(c + 1) * half_blocks])], axis=1)
              for c in range(2)]
    return jnp.stack(halves, axis=0).astype(BF16)


def _row(v):
    return v.reshape(1, -1).astype(F32)


def kernel(x_prompt, x_sample, state_conv, state_lru, cache_k, cache_v, norm_mix, w_in, conv_w, conv_b,
           lru_w_a, lru_b_a, lru_w_x, lru_b_x, lru_lambda, out_norm_lru, out_norm_attn, w_out, norm_ffn,
           w_gate, w_up, w_down, norm_final):
    depth = w_in.shape[0]
    bp, sp, _ = x_prompt.shape
    bs, ts, _ = x_sample.shape
    w_p = min(MAX_WINDOW, sp)
    assert ts >= CONV_WIDTH - 1
    xp = x_prompt.reshape(bp * sp, D_MODEL)
    xs = x_sample.reshape(bs * ts, D_MODEL)
    g_final = _row(norm_final)
    conv_p, lru_p, kw_p, vw_p = [], [], [], []
    conv_s, lru_s, kn_s, vn_s = [], [], [], []
    for l in range(depth):
        last = l == depth - 1
        w_in_bf = w_in[l].astype(BF16)
        g_mix = _row(norm_mix[l])
        lw = dict(
            conv_w=conv_w[l].astype(F32), conv_b=_row(conv_b[l]),
            w_gates=_block_diag_gate_weights(lru_w_a[l], lru_w_x[l]),
            b_a=_row(lru_b_a[l]), b_x=_row(lru_b_x[l]),
            neg_c_softplus=_row(-LRU_C * jax.nn.softplus(-lru_lambda[l].astype(F32))),
            g_lru=_row(out_norm_lru[l]))
        pw = dict(
            g_attn=_row(out_norm_attn[l]), w_out=w_out[l].astype(BF16), g_ffn=_row(norm_ffn[l]),
            w_gate=w_gate[l].astype(BF16), w_up=w_up[l].astype(BF16), w_down=w_down[l].astype(BF16))

        u, gate, q, k, v, k_win, v_win = _inproj_window(xp, g_mix, w_in_bf, tm=512, seq_len=sp, window=w_p)
        shp = lambda a: a.reshape(bp, sp, -1)
        yl, c_new, h_last = _lru_prompt(
            shp(u), shp(gate), jnp.zeros((bp, CONV_WIDTH - 1, D_LRU), F32),
            jnp.zeros((bp, 1, D_LRU), F32), lw, tt=1024)
        ya = _attn_prompt(shp(q), shp(k), shp(v))
        xp = _mix_ffn(xp, yl.reshape(bp * sp, D_LRU), ya.reshape(bp * sp, D_ATTN), pw, g_final,
                      tm=512, final_norm=last)
        conv_p.append(c_new)
        lru_p.append(h_last.reshape(bp, D_LRU))
        win = lambda a: a.reshape(bp, N_HEADS, HEAD_DIM, w_p).transpose(0, 3, 1, 2)
        kw_p.append(win(k_win))
        vw_p.append(win(v_win))

        u, gate, q, k, v = _inproj(xs, g_mix, w_in_bf, tm=bs * ts)
        tmaj = lambda a: a.reshape(bs, ts, -1).transpose(1, 0, 2)
        yl_t, c_new_t, h_last = _lru_sample(
            tmaj(u), tmaj(gate), state_conv[l].astype(F32).transpose(1, 0, 2),
            state_lru[l].astype(F32), lw)
        seq = lambda a: a.reshape(bs, ts, D_ATTN)
        ya = _attn_sample(seq(q), seq(k), seq(v), cache_k[l].astype(F32), cache_v[l].astype(F32))
        xs = _mix_ffn(xs, yl_t.transpose(1, 0, 2).reshape(bs * ts, D_LRU), ya.reshape(bs * ts, D_ATTN),
                      pw, g_final, tm=bs * ts, final_norm=last)
        conv_s.append(c_new_t.transpose(1, 0, 2))
        lru_s.append(h_last)
        kn_s.append(k.reshape(bs, ts, N_HEADS, HEAD_DIM))
        vn_s.append(v.reshape(bs, ts, N_HEADS, HEAD_DIM))
    return (xp.reshape(bp, sp, D_MODEL), xs.reshape(bs, ts, D_MODEL),
            jnp.stack(conv_p, 0), jnp.stack(lru_p, 0), jnp.stack(kw_p, 0), jnp.stack(vw_p, 0),
            jnp.stack(conv_s, 0), jnp.stack(lru_s, 0), jnp.stack(kn_s, 0), jnp.stack(vn_s, 0))
```

```python
import functools
import math

import jax
import jax.numpy as jnp
from jax import lax
from jax.experimental import pallas as pl
from jax.experimental.pallas import tpu as pltpu

F32 = jnp.float32
BF16 = jnp.bfloat16

D_MODEL = 1024
D_LRU = 512
D_ATTN = 512
N_HEADS = 8
HEAD_DIM = 64
N_LRU_BLOCKS = 8
LRU_BLOCK = D_LRU // N_LRU_BLOCKS
CONV_WIDTH = 4
LRU_C = 8.0
DILATED_PATTERNS = ((128, 1), (512, 4), (2048, 16))
MAX_WINDOW = 2048
D_FF = 2816
D_IN = 2 * D_LRU + 3 * D_ATTN
RMS_EPS = 1e-6
Q_SCALE = HEAD_DIM ** -0.5

LANES = 128
SUBLANES = 8
VMEM_LIMIT_BYTES = 56 * 1024 * 1024

KEYS_PER_PATTERN = 129
QBLK = 128
SUPER = 2048
HEADS_PER_STEP = LANES // HEAD_DIM
NEG_BIG = -1e30
LOG2_E = math.log2(math.e)
MERGE_DIL = 4
LRU_CHUNK = 256
SEG_LEN = LRU_CHUNK // SUBLANES
HEAD_SHIFT = HEAD_DIM.bit_length() - 1
HEADS_SHIFT = N_HEADS.bit_length() - 1
assert all(d & (d - 1) == 0 for _, d in DILATED_PATTERNS) and 1 << HEAD_SHIFT == HEAD_DIM
assert all(w // d + 1 == KEYS_PER_PATTERN for w, d in DILATED_PATTERNS)


def _rms_scale(x):
    return lax.rsqrt(jnp.mean(x * x, axis=-1, keepdims=True) + RMS_EPS)


def _gelu_tanh(x):
    c = math.sqrt(2.0 / math.pi)
    return 0.5 * x * (1.0 + jnp.tanh(c * (x + 0.044715 * (x * x * x))))


def _normed(x_ref, g_ref):
    x = x_ref[...]
    return (x * _rms_scale(x) * g_ref[...]).astype(BF16)


def _project(xn, w_ref, first_col, width=D_LRU):
    return jnp.dot(xn, w_ref[:, first_col:first_col + width], preferred_element_type=F32)


def _project_all(x_ref, g_ref, w_ref, u_ref, gate_ref, q_ref, k_ref, v_ref):
    xn = _normed(x_ref, g_ref)
    u_ref[...] = _project(xn, w_ref, 0)
    gate_ref[...] = _project(xn, w_ref, D_LRU)
    o = 2 * D_LRU
    q_ref[...] = _project(xn, w_ref, o, D_ATTN) * Q_SCALE
    k = _project(xn, w_ref, o + D_ATTN, D_ATTN)
    v = _project(xn, w_ref, o + 2 * D_ATTN, D_ATTN)
    k_ref[...] = k
    v_ref[...] = v
    return k, v


def _inproj_kernel(x_ref, g_ref, w_ref, u_ref, gate_ref, q_ref, k_ref, v_ref):
    _project_all(x_ref, g_ref, w_ref, u_ref, gate_ref, q_ref, k_ref, v_ref)


def _inproj_window_kernel(x_ref, g_ref, w_ref, u_ref, gate_ref, q_ref, k_ref, v_ref, kw_ref, vw_ref):
    k, v = _project_all(x_ref, g_ref, w_ref, u_ref, gate_ref, q_ref, k_ref, v_ref)
    kw_ref[0] = k.T
    vw_ref[0] = v.T


def _inproj(x2d, g, w_bf, tm):
    n = x2d.shape[0]
    assert n % tm == 0
    const = lambda i: (0, 0)
    row = lambda i: (i, 0)
    return pl.pallas_call(
        _inproj_kernel,
        grid=(n // tm,),
        in_specs=[
            pl.BlockSpec((tm, D_MODEL), row),
            pl.BlockSpec((1, D_MODEL), const),
            pl.BlockSpec((D_MODEL, D_IN), const, pipeline_mode=pl.Buffered(1)),
        ],
        out_specs=[pl.BlockSpec((tm, D_LRU), row)] * 5,
        out_shape=[jax.ShapeDtypeStruct((n, D_LRU), F32)] * 5,
        compiler_params=pltpu.CompilerParams(
            dimension_semantics=("arbitrary",), vmem_limit_bytes=VMEM_LIMIT_BYTES),
        name="inproj",
    )(x2d, g, w_bf)


def _inproj_window(x2d, g, w_bf, tm, seq_len, window):
    n = x2d.shape[0]
    assert n % seq_len == 0 and seq_len % tm == 0 and window % tm == 0
    per_seq, per_win, n_seq = seq_len // tm, window // tm, n // seq_len
    const = lambda i: (0, 0)
    row = lambda i: (i, 0)
    win_block = lambda i: (i // per_seq, 0, jnp.maximum(i % per_seq - (per_seq - per_win), 0))
    return pl.pallas_call(
        _inproj_window_kernel,
        grid=(n // tm,),
        in_specs=[
            pl.BlockSpec((tm, D_MODEL), row),
            pl.BlockSpec((1, D_MODEL), const),
            pl.BlockSpec((D_MODEL, D_IN), const, pipeline_mode=pl.Buffered(1)),
        ],
        out_specs=[pl.BlockSpec((tm, D_LRU), row)] * 5 + [pl.BlockSpec((1, D_ATTN, tm), win_block)] * 2,
        out_shape=[jax.ShapeDtypeStruct((n, D_LRU), F32)] * 5
        + [jax.ShapeDtypeStruct((n_seq, D_ATTN, window), F32)] * 2,
        compiler_params=pltpu.CompilerParams(
            dimension_semantics=("arbitrary",), vmem_limit_bytes=VMEM_LIMIT_BYTES),
        name="inproj_window",
    )(x2d, g, w_bf)


def _sigmoid(x):
    return jax.nn.sigmoid(x)


def _lru_gates(uc, wg_ref, ba, bx_bias, neg_c_softplus):
    half = D_LRU // 2
    ucb = uc.astype(BF16)
    pre0 = jnp.dot(ucb[:, 0:half], wg_ref[0], preferred_element_type=F32)
    pre1 = jnp.dot(ucb[:, half:D_LRU], wg_ref[1], preferred_element_type=F32)
    r_pre = jnp.concatenate([pre0[:, 0:half], pre1[:, 0:half]], axis=1) + ba
    i_pre = jnp.concatenate([pre0[:, half:], pre1[:, half:]], axis=1) + bx_bias
    r = _sigmoid(r_pre)
    i = _sigmoid(i_pre)
    log_a = r * neg_c_softplus
    a = jnp.exp(log_a)
    z = -jnp.tanh(log_a) * (a * a + 1.0)
    root = jnp.where(z > 0.0, z * lax.rsqrt(z), 0.0)
    b = root * (i * uc)
    return a, b


def _shift_rows(x, s, fill):
    t = x.shape[0]
    if s % SUBLANES == 0:
        return jnp.concatenate([jnp.full((s, x.shape[1]), fill, x.dtype), x[0:t - s]], axis=0)
    rolled = pltpu.roll(x, s, axis=0)
    row = lax.broadcasted_iota(jnp.int32, x.shape, 0)
    return jnp.where(row < s, fill, rolled)


def _scan_rows(a, b):
    t = a.shape[0]
    s = 1
    while s < t:
        b = a * _shift_rows(b, s, 0.0) + b
        if 2 * s < t:
            a = a * _shift_rows(a, s, 1.0)
        s *= 2
    return b


def _lru_prompt_kernel(u_ref, gate_ref, cprev_ref, h0_ref, cw_ref, cb_ref, wg_ref, ba_ref, bx_ref,
                       ncs_ref, gl_ref, y_ref, conv_ref, hlast_ref, tail_ref, h_ref, ubuf_ref, gbuf_ref,
                       ybuf_ref, *, n_chunks):
    ti = pl.program_id(1)
    lane_chunks = D_LRU // LANES

    def to_segment_major(src_ref, r0, buf_ref):
        for s in range(SUBLANES):
            rows = pl.ds(pl.multiple_of(r0 + s * SEG_LEN, SEG_LEN), SEG_LEN)
            for c in range(lane_chunks):
                buf_ref[c, pl.ds(s, SEG_LEN, stride=SUBLANES), :] = src_ref[0, rows, c * LANES:(c + 1) * LANES]
        return jnp.concatenate([buf_ref[c] for c in range(lane_chunks)], axis=1)
    n_t = pl.num_programs(1)

    @pl.when(ti == 0)
    def _():
        tail_ref[...] = jnp.zeros_like(tail_ref)
        tail_ref[SUBLANES - (CONV_WIDTH - 1):SUBLANES, :] = cprev_ref[0]
        h_ref[...] = jnp.broadcast_to(h0_ref[0], h_ref.shape)

    cw = cw_ref[...]
    cb = cb_ref[...]
    ba = ba_ref[...]
    bxb = bx_ref[...]
    ncs = ncs_ref[...]
    gl = gl_ref[...]
    row8 = lax.broadcasted_iota(jnp.int32, (SUBLANES, D_LRU), 0)
    seg0 = row8 == 0
    n_prev = CONV_WIDTH - 1

    def vrow(x, tau):
        return x[SUBLANES * tau:SUBLANES * (tau + 1)]

    def chunk(ci, carry):
        r0 = pl.multiple_of(ci * LRU_CHUNK, LRU_CHUNK)
        u = to_segment_major(u_ref, r0, ubuf_ref)
        tail = tail_ref[...]
        pre = []
        for k in range(n_prev, 0, -1):
            from_prev_seg = pltpu.roll(vrow(u, SEG_LEN - k), 1, axis=0)
            pre.append(jnp.where(seg0, tail[SUBLANES - k:SUBLANES - k + 1], from_prev_seg))
            tail_ref[SUBLANES - k:SUBLANES - k + 1, :] = u[SUBLANES * (SEG_LEN - k + 1) - 1:
                                                           SUBLANES * (SEG_LEN - k + 1)]
        ext = jnp.concatenate(pre + [u], axis=0)
        uc = cb
        for j in range(CONV_WIDTH):
            uc = uc + cw[j:j + 1, :] * ext[SUBLANES * j:SUBLANES * j + LRU_CHUNK]
        a, b = _lru_gates(uc, wg_ref, ba, bxb, ncs)

        h = vrow(b, 0)
        p = vrow(a, 0)
        hs, ps = [h], [p]
        for tau in range(1, SEG_LEN):
            a_t = vrow(a, tau)
            h = a_t * h + vrow(b, tau)
            p = a_t * p
            hs.append(h)
            ps.append(p)
        h_in = h_ref[0:1, :]
        seg_end = _scan_rows(p, jnp.where(seg0, h + p * h_in, h))
        carry_in = jnp.where(seg0, h_in, pltpu.roll(seg_end, 1, axis=0))
        h_ref[...] = jnp.broadcast_to(seg_end[SUBLANES - 1:SUBLANES], h_ref.shape)
        h_all = jnp.concatenate([hs[t] + ps[t] * carry_in for t in range(SEG_LEN)], axis=0)

        y = h_all * _gelu_tanh(to_segment_major(gate_ref, r0, gbuf_ref))
        yn = y * _rms_scale(y) * gl
        for c in range(lane_chunks):
            ybuf_ref[c] = yn[:, c * LANES:(c + 1) * LANES]
        for s in range(SUBLANES):
            rows = jnp.concatenate([ybuf_ref[c, pl.ds(s, SEG_LEN, stride=SUBLANES), :]
                                    for c in range(lane_chunks)], axis=1)
            y_ref[0, pl.ds(pl.multiple_of(r0 + s * SEG_LEN, SEG_LEN), SEG_LEN), :] = rows.astype(BF16)
        return carry

    lax.fori_loop(0, n_chunks, chunk, 0)

    @pl.when(ti == n_t - 1)
    def _():
        conv_ref[0] = tail_ref[SUBLANES - (CONV_WIDTH - 1):SUBLANES, :]
        hlast_ref[0] = h_ref[0:1, :]


def _lru_prompt(u, gate, conv_prev, h0, lw, tt):
    b, s, _ = u.shape
    assert s % tt == 0 and tt % LRU_CHUNK == 0
    seq = lambda bi, ti: (bi, ti, 0)
    per_b = lambda bi, ti: (bi, 0, 0)
    c2 = lambda bi, ti: (0, 0)
    c3 = lambda bi, ti: (0, 0, 0)
    vec = pl.BlockSpec((1, D_LRU), c2)
    return pl.pallas_call(
        functools.partial(_lru_prompt_kernel, n_chunks=tt // LRU_CHUNK),
        grid=(b, s // tt),
        in_specs=[
            pl.BlockSpec((1, tt, D_LRU), seq),
            pl.BlockSpec((1, tt, D_LRU), seq),
            pl.BlockSpec((1, CONV_WIDTH - 1, D_LRU), per_b),
            pl.BlockSpec((1, 1, D_LRU), per_b),
            pl.BlockSpec((CONV_WIDTH, D_LRU), c2),
            vec,
            pl.BlockSpec((2, D_LRU // 2, D_LRU), c3),
            vec, vec, vec, vec,
        ],
        out_specs=[
            pl.BlockSpec((1, tt, D_LRU), seq),
            pl.BlockSpec((1, CONV_WIDTH - 1, D_LRU), per_b),
            pl.BlockSpec((1, 1, D_LRU), per_b),
        ],
        out_shape=[
            jax.ShapeDtypeStruct((b, s, D_LRU), BF16),
            jax.ShapeDtypeStruct((b, CONV_WIDTH - 1, D_LRU), F32),
            jax.ShapeDtypeStruct((b, 1, D_LRU), F32),
        ],
        scratch_shapes=[pltpu.VMEM((SUBLANES, D_LRU), F32), pltpu.VMEM((SUBLANES, D_LRU), F32)]
        + [pltpu.VMEM((D_LRU // LANES, LRU_CHUNK, LANES), F32)] * 3,
        compiler_params=pltpu.CompilerParams(
            dimension_semantics=("arbitrary", "arbitrary"), vmem_limit_bytes=VMEM_LIMIT_BYTES),
        name="lru_prompt",
    )(u, gate, conv_prev, h0, lw["conv_w"], lw["conv_b"], lw["w_gates"], lw["b_a"], lw["b_x"],
      lw["neg_c_softplus"], lw["g_lru"])


def _lru_sample_kernel(u_ref, gate_ref, cprev_ref, h0_ref, cw_ref, cb_ref, wg_ref, ba_ref, bx_ref,
                       ncs_ref, gl_ref, y_ref, conv_ref, hlast_ref):
    t_len, nb, _ = u_ref.shape
    cw = cw_ref[...]
    hist = [cprev_ref[j] for j in range(CONV_WIDTH - 1)] + [u_ref[t] for t in range(t_len)]
    ucs = []
    for t in range(t_len):
        uc = cb_ref[...]
        for j in range(CONV_WIDTH):
            uc = uc + cw[j:j + 1, :] * hist[t + j]
        ucs.append(uc)
    uc_all = jnp.concatenate(ucs, axis=0)
    a, b = _lru_gates(uc_all, wg_ref, ba_ref[...], bx_ref[...], ncs_ref[...])
    h = h0_ref[...]
    gl = gl_ref[...]
    for t in range(t_len):
        h = a[t * nb:(t + 1) * nb] * h + b[t * nb:(t + 1) * nb]
        y = h * _gelu_tanh(gate_ref[t])
        y_ref[t] = (y * _rms_scale(y) * gl).astype(BF16)
    hlast_ref[...] = h
    for j in range(CONV_WIDTH - 1):
        conv_ref[j] = hist[t_len + j]


def _lru_sample(u_t, gate_t, conv_prev_t, h0, lw):
    t_len, nb, _ = u_t.shape
    return pl.pallas_call(
        _lru_sample_kernel,
        out_shape=[
            jax.ShapeDtypeStruct((t_len, nb, D_LRU), BF16),
            jax.ShapeDtypeStruct((CONV_WIDTH - 1, nb, D_LRU), F32),
            jax.ShapeDtypeStruct((nb, D_LRU), F32),
        ],
        name="lru_sample",
    )(u_t, gate_t, conv_prev_t, h0, lw["conv_w"], lw["conv_b"], lw["w_gates"], lw["b_a"], lw["b_x"],
      lw["neg_c_softplus"], lw["g_lru"])


def _attn_prompt_kernel(q_ref, k_ref, v_ref, o_ref,
                        qp1, qp4, qp16, kb1, kb4, kb16, vb1, vb4, vb16, res_a, res_l, res_m, bias_ref,
                        stage_ref):
    sb = pl.program_id(2)
    qps = (qp1, qp4, qp16)
    kbs = (kb1, kb4, kb16)
    vbs = (vb1, vb4, vb16)
    dils = tuple(d for _, d in DILATED_PATTERNS)
    assert dils == (1, 4, 16)
    lane = lax.broadcasted_iota(jnp.int32, (QBLK, LANES), 1)
    head0 = lane < HEAD_DIM

    qi = lax.broadcasted_iota(jnp.int32, (QBLK, 2 * QBLK), 0)
    ki = lax.broadcasted_iota(jnp.int32, (QBLK, 2 * QBLK), 1)
    delta = qi + QBLK - ki
    band = (delta >= 0) & (delta <= KEYS_PER_PATTERN - 1)
    bias_ref[0] = jnp.where(band, 0.0, NEG_BIG)
    bias_ref[1] = jnp.where(band & (ki >= QBLK), 0.0, NEG_BIG)

    for kb, vb, d in zip(kbs, vbs, dils):
        cls_len = SUPER // d

        @pl.when(sb == 0)
        def _():
            kb[:, 0:QBLK, :] = jnp.zeros((d, QBLK, LANES), BF16)
            vb[:, 0:QBLK, 0:LANES] = jnp.zeros((d, QBLK, LANES), BF16)
            vb[:, :, LANES:2 * LANES] = jnp.ones((d, QBLK + cls_len, LANES), BF16)

        @pl.when(sb > 0)
        def _():
            kb[:, 0:QBLK, :] = kb[:, cls_len:cls_len + QBLK, :]
            vb[:, 0:QBLK, 0:LANES] = vb[:, cls_len:cls_len + QBLK, 0:LANES]

    def put(x, p, c, which):
        n_rows = x.shape[0]
        if which == "q":
            qps[p][c] = (x * LOG2_E).astype(BF16)
        elif which == "k":
            kbs[p][c, QBLK:QBLK + n_rows, :] = x.astype(BF16)
        else:
            vbs[p][c, QBLK:QBLK + n_rows, 0:LANES] = x.astype(BF16)

    def tile(p, c, i):
        d = dils[p]
        r0 = i * QBLK
        q_t = qps[p][c, pl.ds(r0, QBLK), :]
        k_t = kbs[p][c, pl.ds(r0, 2 * QBLK), :]
        bias = bias_ref[(sb == 0).astype(jnp.int32)] if i == 0 else bias_ref[0]
        probs, maxes = [], []
        for h in range(HEADS_PER_STEP):
            hm = head0 if h == 0 else jnp.logical_not(head0)
            qh = jnp.where(hm, q_t, jnp.zeros_like(q_t))
            s = lax.dot_general(qh, k_t, (((1,), (1,)), ((), ())), preferred_element_type=F32)
            s = s + bias
            m = jnp.max(s, axis=-1, keepdims=True)
            probs.append(jnp.exp2(s - m).astype(BF16))
            maxes.append(jnp.broadcast_to(m, (QBLK, LANES)))
        v_t = vbs[p][c, pl.ds(r0, 2 * QBLK), :]
        pv = jnp.dot(jnp.concatenate(probs, axis=0), v_t, preferred_element_type=F32)
        pv0, pv1 = pv[0:QBLK], pv[QBLK:2 * QBLK]
        if d == 1:
            dst = pl.ds(r0, QBLK)
        elif d == MERGE_DIL:
            dst = pl.ds(c * (SUPER // MERGE_DIL) + r0, QBLK)
        else:
            dst = pl.ds((c % MERGE_DIL) * (SUPER // MERGE_DIL) + c // MERGE_DIL, QBLK, stride=MERGE_DIL)
        res_a[p, dst, :] = jnp.where(head0, pv0[:, 0:LANES], pv1[:, 0:LANES])
        res_l[p, dst, :] = jnp.where(head0, pv0[:, LANES:2 * LANES], pv1[:, LANES:2 * LANES])
        res_m[p, dst, :] = jnp.where(head0, maxes[0], maxes[1])

    sources = (("q", q_ref), ("k", k_ref), ("v", v_ref))
    for which, ref in sources:
        put(ref[0], 0, 0, which)
    for i in range(SUPER // QBLK):
        tile(0, 0, i)
    for idx, (which, ref) in enumerate(sources):
        for c in range(4):
            x = ref[0, pl.ds(c, SUPER // 4, stride=4), :]
            stage_ref[idx, c] = x
            put(x, 1, c, which)
        for r in range(16):
            put(stage_ref[idx, r % 4, pl.ds(r // 4, QBLK, stride=4), :], 2, r, which)

    merge_rows = 256
    for c in range(MERGE_DIL):
        for i in range(SUPER // MERGE_DIL // QBLK):
            tile(1, c, i)
        for a in range(dils[2] // MERGE_DIL):
            tile(2, a * MERGE_DIL + c, 0)
        for n0 in range(0, SUPER // MERGE_DIL, merge_rows):
            in_time = pl.ds(MERGE_DIL * n0 + c, merge_rows, stride=MERGE_DIL)
            in_class = pl.ds(c * (SUPER // MERGE_DIL) + n0, merge_rows)
            rows = (in_time, in_class, in_class)
            ms = [res_m[p, rows[p], :] for p in range(3)]
            m = jnp.maximum(jnp.maximum(ms[0], ms[1]), ms[2])
            ws = [jnp.exp2(x - m) for x in ms]
            den = ws[0] * res_l[0, rows[0], :] + ws[1] * res_l[1, rows[1], :] + ws[2] * res_l[2, rows[2], :]
            num = ws[0] * res_a[0, rows[0], :] + ws[1] * res_a[1, rows[1], :] + ws[2] * res_a[2, rows[2], :]
            o_ref[0, in_time, :] = num * (1.0 / den)


def _attn_prompt(q, k, v):
    b, s, _ = q.shape
    assert s % SUPER == 0
    blk = pl.BlockSpec((1, SUPER, LANES), lambda bi, hp, sb: (bi, sb, hp))
    scratch = []
    for _, d in DILATED_PATTERNS:
        scratch.append(pltpu.VMEM((d, SUPER // d, LANES), BF16))
    for _, d in DILATED_PATTERNS:
        scratch.append(pltpu.VMEM((d, QBLK + SUPER // d, LANES), BF16))
    for _, d in DILATED_PATTERNS:
        scratch.append(pltpu.VMEM((d, QBLK + SUPER // d, 2 * LANES), BF16))
    scratch += [pltpu.VMEM((3, SUPER, LANES), F32)] * 3
    scratch += [pltpu.VMEM((2, QBLK, 2 * QBLK), F32), pltpu.VMEM((3, 4, SUPER // 4, LANES), F32)]
    return pl.pallas_call(
        _attn_prompt_kernel,
        grid=(b, D_ATTN // LANES, s // SUPER),
        in_specs=[blk, blk, blk],
        out_specs=blk,
        out_shape=jax.ShapeDtypeStruct((b, s, D_ATTN), F32),
        scratch_shapes=scratch,
        compiler_params=pltpu.CompilerParams(
            dimension_semantics=("arbitrary", "arbitrary", "arbitrary"),
            vmem_limit_bytes=VMEM_LIMIT_BYTES),
        name="attn_prompt",
    )(q, k, v)


def _attn_sample_kernel(q_ref, kn_ref, vn_ref, kt_ref, vt_ref, o_ref, *, t_len, w_buf):
    rows = N_HEADS * SUBLANES
    n_pat = len(DILATED_PATTERNS)
    pad = jnp.zeros((SUBLANES - t_len, D_ATTN), F32)
    q8 = jnp.concatenate([q_ref[0] * LOG2_E, pad], axis=0)
    kn8 = jnp.concatenate([kn_ref[0], pad], axis=0).astype(BF16)
    vn8 = jnp.concatenate([vn_ref[0], pad], axis=0).astype(BF16)
    head_lanes = lambda h: slice(h * HEAD_DIM, (h + 1) * HEAD_DIM)

    s_c = jnp.concatenate(
        [jnp.dot(q8[:, head_lanes(h)].astype(BF16), kt_ref[0, h].astype(BF16), preferred_element_type=F32)
         for h in range(N_HEADS)], axis=0)
    rid = lax.broadcasted_iota(jnp.int32, (rows, D_ATTN), 0)
    lid = lax.broadcasted_iota(jnp.int32, (rows, D_ATTN), 1)
    qbd = jnp.where((rid >> 3) == (lid >> HEAD_SHIFT), jnp.concatenate([q8] * N_HEADS, axis=0), 0.0)
    s_n = lax.dot_general(qbd.astype(BF16), kn8, (((1,), (1,)), ((), ())), preferred_element_type=F32)

    t_c = lax.broadcasted_iota(jnp.int32, (rows, w_buf), 0) & (SUBLANES - 1)
    d_c = w_buf + t_c - lax.broadcasted_iota(jnp.int32, (rows, w_buf), 1)
    t_n = lax.broadcasted_iota(jnp.int32, (rows, SUBLANES), 0) & (SUBLANES - 1)
    j_n = lax.broadcasted_iota(jnp.int32, (rows, SUBLANES), 1)
    d_n = t_n - j_n
    new_ok = (d_n >= 0) & (j_n < t_len)

    p_cs, p_ns, lses, ls = [], [], [], []
    for window, dil in DILATED_PATTERNS:
        ok_c = ((d_c & (dil - 1)) == 0) & (d_c <= window)
        ok_n = new_ok & ((d_n & (dil - 1)) == 0) & (d_n <= window)
        sc = jnp.where(ok_c, s_c, NEG_BIG)
        sn = jnp.where(ok_n, s_n, NEG_BIG)
        m = jnp.maximum(jnp.max(sc, axis=-1, keepdims=True), jnp.max(sn, axis=-1, keepdims=True))
        pc = jnp.exp2(sc - m)
        pn = jnp.exp2(sn - m)
        l = jnp.sum(pc, axis=-1, keepdims=True) + jnp.sum(pn, axis=-1, keepdims=True)
        p_cs.append(pc)
        p_ns.append(pn)
        ls.append(l)
        lses.append(m + jnp.log2(l))
    m = jnp.maximum(jnp.maximum(lses[0], lses[1]), lses[2])
    es = [jnp.exp2(x - m) for x in lses]
    inv_den = 1.0 / (es[0] + es[1] + es[2])
    wts = [es[p] * inv_den * (1.0 / ls[p]) for p in range(n_pat)]

    o_new = jnp.dot(jnp.concatenate(p_ns, axis=0).astype(BF16), vn8, preferred_element_type=F32)
    for h in range(N_HEADS):
        rs = slice(h * SUBLANES, (h + 1) * SUBLANES)
        p_h = jnp.concatenate([p_cs[p][rs] for p in range(n_pat)], axis=0).astype(BF16)
        o_h = lax.dot_general(p_h, vt_ref[0, h].astype(BF16), (((1,), (1,)), ((), ())),
                              preferred_element_type=F32)
        out = None
        for p in range(n_pat):
            new_part = o_new[p * rows + h * SUBLANES:p * rows + (h + 1) * SUBLANES, head_lanes(h)]
            term = wts[p][rs] * (o_h[p * SUBLANES:(p + 1) * SUBLANES] + new_part)
            out = term if out is None else out + term
        o_ref[0, :, head_lanes(h)] = out[0:t_len]


def _attn_sample(q, k_new, v_new, cache_k, cache_v):
    b, t_len, _ = q.shape
    w_buf = cache_k.shape[1]
    assert w_buf >= MAX_WINDOW and t_len <= SUBLANES
    new = pl.BlockSpec((1, t_len, D_ATTN), lambda bi: (bi, 0, 0))
    cache = pl.BlockSpec((1, N_HEADS, HEAD_DIM, w_buf), lambda bi: (bi, 0, 0, 0))
    row_minor = lambda c: jnp.transpose(c, (0, 2, 3, 1))
    return pl.pallas_call(
        functools.partial(_attn_sample_kernel, t_len=t_len, w_buf=w_buf),
        grid=(b,),
        in_specs=[new, new, new, cache, cache],
        out_specs=new,
        out_shape=jax.ShapeDtypeStruct((b, t_len, D_ATTN), F32),
        compiler_params=pltpu.CompilerParams(
            dimension_semantics=("arbitrary",), vmem_limit_bytes=VMEM_LIMIT_BYTES),
        name="attn_sample",
    )(q, k_new, v_new, row_minor(cache_k), row_minor(cache_v))


def _mix_ffn_kernel(x_ref, yl_ref, ya_ref, ga_ref, wo_ref, gf_ref, wg_ref, wu_ref, wd_ref, gn_ref,
                    y_ref, *, final_norm):
    ya = ya_ref[...]
    ya_n = (ya * _rms_scale(ya) * ga_ref[...]).astype(BF16)
    mix = (jnp.dot(yl_ref[...], wo_ref[0:D_LRU, :], preferred_element_type=F32)
           + jnp.dot(ya_n, wo_ref[D_LRU:D_LRU + D_ATTN, :], preferred_element_type=F32))
    x1 = x_ref[...] + mix
    xn = (x1 * _rms_scale(x1) * gf_ref[...]).astype(BF16)
    hg = jnp.dot(xn, wg_ref[...], preferred_element_type=F32)
    hu = jnp.dot(xn, wu_ref[...], preferred_element_type=F32)
    act = (hg * _sigmoid(hg) * hu).astype(BF16)
    x2 = x1 + jnp.dot(act, wd_ref[...], preferred_element_type=F32)
    if final_norm:
        x2 = x2 * _rms_scale(x2) * gn_ref[...]
    y_ref[...] = x2


def _mix_ffn(x2d, yl, ya, pw, g_final, tm, final_norm):
    n = x2d.shape[0]
    row = lambda i: (i, 0)
    const = lambda i: (0, 0)
    once = pl.Buffered(1)
    return pl.pallas_call(
        functools.partial(_mix_ffn_kernel, final_norm=final_norm),
        grid=(n // tm,),
        in_specs=[
            pl.BlockSpec((tm, D_MODEL), row),
            pl.BlockSpec((tm, D_LRU), row),
            pl.BlockSpec((tm, D_ATTN), row),
            pl.BlockSpec((1, D_ATTN), const),
            pl.BlockSpec((D_LRU + D_ATTN, D_MODEL), const, pipeline_mode=once),
            pl.BlockSpec((1, D_MODEL), const),
            pl.BlockSpec((D_MODEL, D_FF), const, pipeline_mode=once),
            pl.BlockSpec((D_MODEL, D_FF), const, pipeline_mode=once),
            pl.BlockSpec((D_FF, D_MODEL), const, pipeline_mode=once),
            pl.BlockSpec((1, D_MODEL), const),
        ],
        out_specs=pl.BlockSpec((tm, D_MODEL), row),
        out_shape=jax.ShapeDtypeStruct((n, D_MODEL), F32),
        compiler_params=pltpu.CompilerParams(
            dimension_semantics=("arbitrary",), vmem_limit_bytes=VMEM_LIMIT_BYTES),
        name="mix_ffn",
    )(x2d, yl, ya, pw["g_attn"], pw["w_out"], pw["g_ffn"], pw["w_gate"], pw["w_up"], pw["w_down"],
      g_final)


def _block_diag_gate_weights(w_a, w_x):
    half_blocks = N_LRU_BLOCKS // 2
    half = D_LRU // 2

    def bd(w):
        eye = jnp.eye(half_blocks, dtype=w.dtype)
        return jnp.einsum("nij,nm->nimj", w, eye).reshape(half, half)

    halves = [jnp.concatenate([bd(w_a[c * half_blocks:(c + 1) * half_blocks]),
                               bd(w_x[c * half_blocks:(c + 1) * half_blocks])], axis=1)
              for c in range(2)]
    return jnp.stack(halves, axis=0).astype(BF16)


def _row(v):
    return v.reshape(1, -1).astype(F32)


def kernel(x_prompt, x_sample, state_conv, state_lru, cache_k, cache_v, norm_mix, w_in, conv_w, conv_b,
           lru_w_a, lru_b_a, lru_w_x, lru_b_x, lru_lambda, out_norm_lru, out_norm_attn, w_out, norm_ffn,
           w_gate, w_up, w_down, norm_final):
    depth = w_in.shape[0]
    bp, sp, _ = x_prompt.shape
    bs, ts, _ = x_sample.shape
    w_p = min(MAX_WINDOW, sp)
    assert ts >= CONV_WIDTH - 1
    xp = x_prompt.reshape(bp * sp, D_MODEL)
    xs = x_sample.reshape(bs * ts, D_MODEL)
    g_final = _row(norm_final)
    conv_p, lru_p, kw_p, vw_p = [], [], [], []
    conv_s, lru_s, kn_s, vn_s = [], [], [], []
    for l in range(depth):
        last = l == depth - 1
        w_in_bf = w_in[l].astype(BF16)
        g_mix = _row(norm_mix[l])
        lw = dict(
            conv_w=conv_w[l].astype(F32), conv_b=_row(conv_b[l]),
            w_gates=_block_diag_gate_weights(lru_w_a[l], lru_w_x[l]),
            b_a=_row(lru_b_a[l]), b_x=_row(lru_b_x[l]),
            neg_c_softplus=_row(-LRU_C * jax.nn.softplus(-lru_lambda[l].astype(F32))),
            g_lru=_row(out_norm_lru[l]))
        pw = dict(
            g_attn=_row(out_norm_attn[l]), w_out=w_out[l].astype(BF16), g_ffn=_row(norm_ffn[l]),
            w_gate=w_gate[l].astype(BF16), w_up=w_up[l].astype(BF16), w_down=w_down[l].astype(BF16))

        u, gate, q, k, v, k_win, v_win = _inproj_window(xp, g_mix, w_in_bf, tm=512, seq_len=sp, window=w_p)
        shp = lambda a: a.reshape(bp, sp, -1)
        yl, c_new, h_last = _lru_prompt(
            shp(u), shp(gate), jnp.zeros((bp, CONV_WIDTH - 1, D_LRU), F32),
            jnp.zeros((bp, 1, D_LRU), F32), lw, tt=1024)
        ya = _attn_prompt(shp(q), shp(k), shp(v))
        xp = _mix_ffn(xp, yl.reshape(bp * sp, D_LRU), ya.reshape(bp * sp, D_ATTN), pw, g_final,
                      tm=512, final_norm=last)
        conv_p.append(c_new)
        lru_p.append(h_last.reshape(bp, D_LRU))
        win = lambda a: a.reshape(bp, N_HEADS, HEAD_DIM, w_p).transpose(0, 3, 1, 2)
        kw_p.append(win(k_win))
        vw_p.append(win(v_win))

        u, gate, q, k, v = _inproj(xs, g_mix, w_in_bf, tm=bs * ts)
        tmaj = lambda a: a.reshape(bs, ts, -1).transpose(1, 0, 2)
        yl_t, c_new_t, h_last = _lru_sample(
            tmaj(u), tmaj(gate), state_conv[l].astype(F32).transpose(1, 0, 2),
            state_lru[l].astype(F32), lw)
        seq = lambda a: a.reshape(bs, ts, D_ATTN)
        ya = _attn_sample(seq(q), seq(k), seq(v), cache_k[l].astype(F32), cache_v[l].astype(F32))
        xs = _mix_ffn(xs, yl_t.transpose(1, 0, 2).reshape(bs * ts, D_LRU), ya.reshape(bs * ts, D_ATTN),
                      pw, g_final, tm=bs * ts, final_norm=last)
        conv_s.append(c_new_t.transpose(1, 0, 2))
        lru_s.append(h_last)
        kn_s.append(k.reshape(bs, ts, N_HEADS, HEAD_DIM))
        vn_s.append(v.reshape(bs, ts, N_HEADS, HEAD_DIM))
    return (xp.reshape(bp, sp, D_MODEL), xs.reshape(bs, ts, D_MODEL),
            jnp.stack(conv_p, 0), jnp.stack(lru_p, 0), jnp.stack(kw_p, 0), jnp.stack(vw_p, 0),
            jnp.stack(conv_s, 0), jnp.stack(lru_s, 0), jnp.stack(kn_s, 0), jnp.stack(vn_s, 0))
```

```python
import functools
import math

import jax
import jax.numpy as jnp
from jax import lax
from jax.experimental import pallas as pl
from jax.experimental.pallas import tpu as pltpu

F32 = jnp.float32
BF16 = jnp.bfloat16

D_MODEL = 1024
D_LRU = 512
D_ATTN = 512
N_HEADS = 8
HEAD_DIM = 64
N_LRU_BLOCKS = 8
LRU_BLOCK = D_LRU // N_LRU_BLOCKS
CONV_WIDTH = 4
LRU_C = 8.0
DILATED_PATTERNS = ((128, 1), (512, 4), (2048, 16))
MAX_WINDOW = 2048
D_FF = 2816
D_IN = 2 * D_LRU + 3 * D_ATTN
RMS_EPS = 1e-6
Q_SCALE = HEAD_DIM ** -0.5

LANES = 128
SUBLANES = 8
VMEM_LIMIT_BYTES = 56 * 1024 * 1024

KEYS_PER_PATTERN = 129
QBLK = 128
SUPER = 2048
HEADS_PER_STEP = LANES // HEAD_DIM
NEG_BIG = -1e30
LOG2_E = math.log2(math.e)
MERGE_DIL = 4
LRU_CHUNK = 256
SEG_LEN = LRU_CHUNK // SUBLANES
FF_CHUNK = 256
assert D_FF % FF_CHUNK == 0
HEAD_SHIFT = HEAD_DIM.bit_length() - 1
HEADS_SHIFT = N_HEADS.bit_length() - 1
assert all(d & (d - 1) == 0 for _, d in DILATED_PATTERNS) and 1 << HEAD_SHIFT == HEAD_DIM
assert all(w // d + 1 == KEYS_PER_PATTERN for w, d in DILATED_PATTERNS)


def _rms_scale(x):
    return lax.rsqrt(jnp.mean(x * x, axis=-1, keepdims=True) + RMS_EPS)


def _gelu_tanh(x):
    c = math.sqrt(2.0 / math.pi)
    return 0.5 * x * (1.0 + jnp.tanh(c * (x + 0.044715 * (x * x * x))))


def _normed(x_ref, g_ref):
    x = x_ref[...]
    return (x * _rms_scale(x) * g_ref[...]).astype(BF16)


def _project(xn, w_ref, first_col, width=D_LRU):
    return jnp.dot(xn, w_ref[:, first_col:first_col + width], preferred_element_type=F32)


def _project_all(x_ref, g_ref, w_ref, u_ref, gate_ref, q_ref, k_ref, v_ref):
    xn = _normed(x_ref, g_ref)
    u_ref[...] = _project(xn, w_ref, 0)
    gate_ref[...] = _project(xn, w_ref, D_LRU)
    o = 2 * D_LRU
    q_ref[...] = _project(xn, w_ref, o, D_ATTN) * Q_SCALE
    k = _project(xn, w_ref, o + D_ATTN, D_ATTN)
    v = _project(xn, w_ref, o + 2 * D_ATTN, D_ATTN)
    k_ref[...] = k
    v_ref[...] = v
    return k, v


def _inproj_kernel(x_ref, g_ref, w_ref, u_ref, gate_ref, q_ref, k_ref, v_ref):
    _project_all(x_ref, g_ref, w_ref, u_ref, gate_ref, q_ref, k_ref, v_ref)


def _inproj_window_kernel(x_ref, g_ref, w_ref, u_ref, gate_ref, q_ref, k_ref, v_ref, kw_ref, vw_ref):
    k, v = _project_all(x_ref, g_ref, w_ref, u_ref, gate_ref, q_ref, k_ref, v_ref)
    kw_ref[0] = k.T
    vw_ref[0] = v.T


def _inproj(x2d, g, w_bf, tm):
    n = x2d.shape[0]
    assert n % tm == 0
    const = lambda i: (0, 0)
    row = lambda i: (i, 0)
    return pl.pallas_call(
        _inproj_kernel,
        grid=(n // tm,),
        in_specs=[
            pl.BlockSpec((tm, D_MODEL), row),
            pl.BlockSpec((1, D_MODEL), const),
            pl.BlockSpec((D_MODEL, D_IN), const, pipeline_mode=pl.Buffered(1)),
        ],
        out_specs=[pl.BlockSpec((tm, D_LRU), row)] * 5,
        out_shape=[jax.ShapeDtypeStruct((n, D_LRU), F32)] * 5,
        compiler_params=pltpu.CompilerParams(
            dimension_semantics=("arbitrary",), vmem_limit_bytes=VMEM_LIMIT_BYTES),
        name="inproj",
    )(x2d, g, w_bf)


def _inproj_window(x2d, g, w_bf, tm, seq_len, window):
    n = x2d.shape[0]
    assert n % seq_len == 0 and seq_len % tm == 0 and window % tm == 0
    per_seq, per_win, n_seq = seq_len // tm, window // tm, n // seq_len
    const = lambda i: (0, 0)
    row = lambda i: (i, 0)
    win_block = lambda i: (i // per_seq, 0, jnp.maximum(i % per_seq - (per_seq - per_win), 0))
    return pl.pallas_call(
        _inproj_window_kernel,
        grid=(n // tm,),
        in_specs=[
            pl.BlockSpec((tm, D_MODEL), row),
            pl.BlockSpec((1, D_MODEL), const),
            pl.BlockSpec((D_MODEL, D_IN), const, pipeline_mode=pl.Buffered(1)),
        ],
        out_specs=[pl.BlockSpec((tm, D_LRU), row)] * 5 + [pl.BlockSpec((1, D_ATTN, tm), win_block)] * 2,
        out_shape=[jax.ShapeDtypeStruct((n, D_LRU), F32)] * 5
        + [jax.ShapeDtypeStruct((n_seq, D_ATTN, window), F32)] * 2,
        compiler_params=pltpu.CompilerParams(
            dimension_semantics=("arbitrary",), vmem_limit_bytes=VMEM_LIMIT_BYTES),
        name="inproj_window",
    )(x2d, g, w_bf)


def _sigmoid(x):
    return jax.nn.sigmoid(x)


def _lru_gates(uc, wg_ref, ba, bx_bias, neg_c_softplus):
    half = D_LRU // 2
    ucb = uc.astype(BF16)
    pre0 = jnp.dot(ucb[:, 0:half], wg_ref[0], preferred_element_type=F32)
    pre1 = jnp.dot(ucb[:, half:D_LRU], wg_ref[1], preferred_element_type=F32)
    r_pre = jnp.concatenate([pre0[:, 0:half], pre1[:, 0:half]], axis=1) + ba
    i_pre = jnp.concatenate([pre0[:, half:], pre1[:, half:]], axis=1) + bx_bias
    r = _sigmoid(r_pre)
    i = _sigmoid(i_pre)
    log_a = r * neg_c_softplus
    a = jnp.exp(log_a)
    z = -jnp.tanh(log_a) * (a * a + 1.0)
    root = jnp.where(z > 0.0, z * lax.rsqrt(z), 0.0)
    b = root * (i * uc)
    return a, b


def _shift_rows(x, s, fill):
    t = x.shape[0]
    if s % SUBLANES == 0:
        return jnp.concatenate([jnp.full((s, x.shape[1]), fill, x.dtype), x[0:t - s]], axis=0)
    rolled = pltpu.roll(x, s, axis=0)
    row = lax.broadcasted_iota(jnp.int32, x.shape, 0)
    return jnp.where(row < s, fill, rolled)


def _scan_rows(a, b):
    t = a.shape[0]
    s = 1
    while s < t:
        b = a * _shift_rows(b, s, 0.0) + b
        if 2 * s < t:
            a = a * _shift_rows(a, s, 1.0)
        s *= 2
    return b


def _lru_prompt_kernel(u_ref, gate_ref, cprev_ref, h0_ref, cw_ref, cb_ref, wg_ref, ba_ref, bx_ref,
                       ncs_ref, gl_ref, y_ref, conv_ref, hlast_ref, tail_ref, h_ref, ubuf_ref, gbuf_ref,
                       ybuf_ref, *, n_chunks):
    ti = pl.program_id(1)
    lane_chunks = D_LRU // LANES

    def to_segment_major(src_ref, r0, buf_ref):
        for s in range(SUBLANES):
            rows = pl.ds(pl.multiple_of(r0 + s * SEG_LEN, SEG_LEN), SEG_LEN)
            for c in range(lane_chunks):
                buf_ref[c, pl.ds(s, SEG_LEN, stride=SUBLANES), :] = src_ref[0, rows, c * LANES:(c + 1) * LANES]
        return jnp.concatenate([buf_ref[c] for c in range(lane_chunks)], axis=1)
    n_t = pl.num_programs(1)

    @pl.when(ti == 0)
    def _():
        tail_ref[...] = jnp.zeros_like(tail_ref)
        tail_ref[SUBLANES - (CONV_WIDTH - 1):SUBLANES, :] = cprev_ref[0]
        h_ref[...] = jnp.broadcast_to(h0_ref[0], h_ref.shape)

    cw = cw_ref[...]
    cb = cb_ref[...]
    ba = ba_ref[...]
    bxb = bx_ref[...]
    ncs = ncs_ref[...]
    gl = gl_ref[...]
    row8 = lax.broadcasted_iota(jnp.int32, (SUBLANES, D_LRU), 0)
    seg0 = row8 == 0
    n_prev = CONV_WIDTH - 1

    def vrow(x, tau):
        return x[SUBLANES * tau:SUBLANES * (tau + 1)]

    def chunk(ci, carry):
        r0 = pl.multiple_of(ci * LRU_CHUNK, LRU_CHUNK)
        u = to_segment_major(u_ref, r0, ubuf_ref)
        tail = tail_ref[...]
        pre = []
        for k in range(n_prev, 0, -1):
            from_prev_seg = pltpu.roll(vrow(u, SEG_LEN - k), 1, axis=0)
            pre.append(jnp.where(seg0, tail[SUBLANES - k:SUBLANES - k + 1], from_prev_seg))
            tail_ref[SUBLANES - k:SUBLANES - k + 1, :] = u[SUBLANES * (SEG_LEN - k + 1) - 1:
                                                           SUBLANES * (SEG_LEN - k + 1)]
        ext = jnp.concatenate(pre + [u], axis=0)
        uc = cb
        for j in range(CONV_WIDTH):
            uc = uc + cw[j:j + 1, :] * ext[SUBLANES * j:SUBLANES * j + LRU_CHUNK]
        a, b = _lru_gates(uc, wg_ref, ba, bxb, ncs)

        h = vrow(b, 0)
        p = vrow(a, 0)
        hs, ps = [h], [p]
        for tau in range(1, SEG_LEN):
            a_t = vrow(a, tau)
            h = a_t * h + vrow(b, tau)
            p = a_t * p
            hs.append(h)
            ps.append(p)
        h_in = h_ref[0:1, :]
        seg_end = _scan_rows(p, jnp.where(seg0, h + p * h_in, h))
        carry_in = jnp.where(seg0, h_in, pltpu.roll(seg_end, 1, axis=0))
        h_ref[...] = jnp.broadcast_to(seg_end[SUBLANES - 1:SUBLANES], h_ref.shape)
        h_all = jnp.concatenate([hs[t] + ps[t] * carry_in for t in range(SEG_LEN)], axis=0)

        y = h_all * _gelu_tanh(to_segment_major(gate_ref, r0, gbuf_ref))
        yn = y * _rms_scale(y) * gl
        for c in range(lane_chunks):
            ybuf_ref[c] = yn[:, c * LANES:(c + 1) * LANES]
        for s in range(SUBLANES):
            rows = jnp.concatenate([ybuf_ref[c, pl.ds(s, SEG_LEN, stride=SUBLANES), :]
                                    for c in range(lane_chunks)], axis=1)
            y_ref[0, pl.ds(pl.multiple_of(r0 + s * SEG_LEN, SEG_LEN), SEG_LEN), :] = rows.astype(BF16)
        return carry

    lax.fori_loop(0, n_chunks, chunk, 0)

    @pl.when(ti == n_t - 1)
    def _():
        conv_ref[0] = tail_ref[SUBLANES - (CONV_WIDTH - 1):SUBLANES, :]
        hlast_ref[0] = h_ref[0:1, :]


def _lru_prompt(u, gate, conv_prev, h0, lw, tt):
    b, s, _ = u.shape
    assert s % tt == 0 and tt % LRU_CHUNK == 0
    seq = lambda bi, ti: (bi, ti, 0)
    per_b = lambda bi, ti: (bi, 0, 0)
    c2 = lambda bi, ti: (0, 0)
    c3 = lambda bi, ti: (0, 0, 0)
    vec = pl.BlockSpec((1, D_LRU), c2)
    return pl.pallas_call(
        functools.partial(_lru_prompt_kernel, n_chunks=tt // LRU_CHUNK),
        grid=(b, s // tt),
        in_specs=[
            pl.BlockSpec((1, tt, D_LRU), seq),
            pl.BlockSpec((1, tt, D_LRU), seq),
            pl.BlockSpec((1, CONV_WIDTH - 1, D_LRU), per_b),
            pl.BlockSpec((1, 1, D_LRU), per_b),
            pl.BlockSpec((CONV_WIDTH, D_LRU), c2),
            vec,
            pl.BlockSpec((2, D_LRU // 2, D_LRU), c3),
            vec, vec, vec, vec,
        ],
        out_specs=[
            pl.BlockSpec((1, tt, D_LRU), seq),
            pl.BlockSpec((1, CONV_WIDTH - 1, D_LRU), per_b),
            pl.BlockSpec((1, 1, D_LRU), per_b),
        ],
        out_shape=[
            jax.ShapeDtypeStruct((b, s, D_LRU), BF16),
            jax.ShapeDtypeStruct((b, CONV_WIDTH - 1, D_LRU), F32),
            jax.ShapeDtypeStruct((b, 1, D_LRU), F32),
        ],
        scratch_shapes=[pltpu.VMEM((SUBLANES, D_LRU), F32), pltpu.VMEM((SUBLANES, D_LRU), F32)]
        + [pltpu.VMEM((D_LRU // LANES, LRU_CHUNK, LANES), F32)] * 3,
        compiler_params=pltpu.CompilerParams(
            dimension_semantics=("arbitrary", "arbitrary"), vmem_limit_bytes=VMEM_LIMIT_BYTES),
        name="lru_prompt",
    )(u, gate, conv_prev, h0, lw["conv_w"], lw["conv_b"], lw["w_gates"], lw["b_a"], lw["b_x"],
      lw["neg_c_softplus"], lw["g_lru"])


def _lru_sample_kernel(u_ref, gate_ref, cprev_ref, h0_ref, cw_ref, cb_ref, wg_ref, ba_ref, bx_ref,
                       ncs_ref, gl_ref, y_ref, conv_ref, hlast_ref):
    t_len, nb, _ = u_ref.shape
    cw = cw_ref[...]
    hist = [cprev_ref[j] for j in range(CONV_WIDTH - 1)] + [u_ref[t] for t in range(t_len)]
    ucs = []
    for t in range(t_len):
        uc = cb_ref[...]
        for j in range(CONV_WIDTH):
            uc = uc + cw[j:j + 1, :] * hist[t + j]
        ucs.append(uc)
    uc_all = jnp.concatenate(ucs, axis=0)
    a, b = _lru_gates(uc_all, wg_ref, ba_ref[...], bx_ref[...], ncs_ref[...])
    h = h0_ref[...]
    gl = gl_ref[...]
    for t in range(t_len):
        h = a[t * nb:(t + 1) * nb] * h + b[t * nb:(t + 1) * nb]
        y = h * _gelu_tanh(gate_ref[t])
        y_ref[t] = (y * _rms_scale(y) * gl).astype(BF16)
    hlast_ref[...] = h
    for j in range(CONV_WIDTH - 1):
        conv_ref[j] = hist[t_len + j]


def _lru_sample(u_t, gate_t, conv_prev_t, h0, lw):
    t_len, nb, _ = u_t.shape
    return pl.pallas_call(
        _lru_sample_kernel,
        out_shape=[
            jax.ShapeDtypeStruct((t_len, nb, D_LRU), BF16),
            jax.ShapeDtypeStruct((CONV_WIDTH - 1, nb, D_LRU), F32),
            jax.ShapeDtypeStruct((nb, D_LRU), F32),
        ],
        name="lru_sample",
    )(u_t, gate_t, conv_prev_t, h0, lw["conv_w"], lw["conv_b"], lw["w_gates"], lw["b_a"], lw["b_x"],
      lw["neg_c_softplus"], lw["g_lru"])


def _attn_prompt_kernel(q_ref, k_ref, v_ref, o_ref,
                        qp1, qp4, qp16, kb1, kb4, kb16, vb1, vb4, vb16, res_a, res_l, res_m, bias_ref,
                        stage_ref):
    sb = pl.program_id(2)
    qps = (qp1, qp4, qp16)
    kbs = (kb1, kb4, kb16)
    vbs = (vb1, vb4, vb16)
    dils = tuple(d for _, d in DILATED_PATTERNS)
    assert dils == (1, 4, 16)
    lane = lax.broadcasted_iota(jnp.int32, (QBLK, LANES), 1)
    head0 = lane < HEAD_DIM

    qi = lax.broadcasted_iota(jnp.int32, (QBLK, 2 * QBLK), 0)
    ki = lax.broadcasted_iota(jnp.int32, (QBLK, 2 * QBLK), 1)
    delta = qi + QBLK - ki
    band = (delta >= 0) & (delta <= KEYS_PER_PATTERN - 1)
    bias_ref[0] = jnp.where(band, 0.0, NEG_BIG)
    bias_ref[1] = jnp.where(band & (ki >= QBLK), 0.0, NEG_BIG)

    for kb, vb, d in zip(kbs, vbs, dils):
        cls_len = SUPER // d

        @pl.when(sb == 0)
        def _():
            kb[:, 0:QBLK, :] = jnp.zeros((d, QBLK, LANES), BF16)
            vb[:, 0:QBLK, 0:LANES] = jnp.zeros((d, QBLK, LANES), BF16)
            vb[:, :, LANES:2 * LANES] = jnp.ones((d, QBLK + cls_len, LANES), BF16)

        @pl.when(sb > 0)
        def _():
            kb[:, 0:QBLK, :] = kb[:, cls_len:cls_len + QBLK, :]
            vb[:, 0:QBLK, 0:LANES] = vb[:, cls_len:cls_len + QBLK, 0:LANES]

    def put(x, p, c, which):
        n_rows = x.shape[0]
        if which == "q":
            qps[p][c] = (x * LOG2_E).astype(BF16)
        elif which == "k":
            kbs[p][c, QBLK:QBLK + n_rows, :] = x.astype(BF16)
        else:
            vbs[p][c, QBLK:QBLK + n_rows, 0:LANES] = x.astype(BF16)

    def tile(p, c, i):
        d = dils[p]
        r0 = i * QBLK
        q_t = qps[p][c, pl.ds(r0, QBLK), :]
        k_t = kbs[p][c, pl.ds(r0, 2 * QBLK), :]
        bias = bias_ref[(sb == 0).astype(jnp.int32)] if i == 0 else bias_ref[0]
        probs, maxes = [], []
        for h in range(HEADS_PER_STEP):
            hm = head0 if h == 0 else jnp.logical_not(head0)
            qh = jnp.where(hm, q_t, jnp.zeros_like(q_t))
            s = lax.dot_general(qh, k_t, (((1,), (1,)), ((), ())), preferred_element_type=F32)
            s = s + bias
            m = jnp.max(s, axis=-1, keepdims=True)
            probs.append(jnp.exp2(s - m).astype(BF16))
            maxes.append(jnp.broadcast_to(m, (QBLK, LANES)))
        v_t = vbs[p][c, pl.ds(r0, 2 * QBLK), :]
        pv = jnp.dot(jnp.concatenate(probs, axis=0), v_t, preferred_element_type=F32)
        pv0, pv1 = pv[0:QBLK], pv[QBLK:2 * QBLK]
        if d == 1:
            dst = pl.ds(r0, QBLK)
        elif d == MERGE_DIL:
            dst = pl.ds(c * (SUPER // MERGE_DIL) + r0, QBLK)
        else:
            dst = pl.ds((c % MERGE_DIL) * (SUPER // MERGE_DIL) + c // MERGE_DIL, QBLK, stride=MERGE_DIL)
        res_a[p, dst, :] = jnp.where(head0, pv0[:, 0:LANES], pv1[:, 0:LANES])
        res_l[p, dst, :] = jnp.where(head0, pv0[:, LANES:2 * LANES], pv1[:, LANES:2 * LANES])
        res_m[p, dst, :] = jnp.where(head0, maxes[0], maxes[1])

    sources = (("q", q_ref), ("k", k_ref), ("v", v_ref))
    for which, ref in sources:
        put(ref[0], 0, 0, which)
    for i in range(SUPER // QBLK):
        tile(0, 0, i)
    for idx, (which, ref) in enumerate(sources):
        for c in range(4):
            x = ref[0, pl.ds(c, SUPER // 4, stride=4), :]
            stage_ref[idx, c] = x
            put(x, 1, c, which)
        for r in range(16):
            put(stage_ref[idx, r % 4, pl.ds(r // 4, QBLK, stride=4), :], 2, r, which)

    merge_rows = 256
    for c in range(MERGE_DIL):
        for i in range(SUPER // MERGE_DIL // QBLK):
            tile(1, c, i)
        for a in range(dils[2] // MERGE_DIL):
            tile(2, a * MERGE_DIL + c, 0)
        for n0 in range(0, SUPER // MERGE_DIL, merge_rows):
            in_time = pl.ds(MERGE_DIL * n0 + c, merge_rows, stride=MERGE_DIL)
            in_class = pl.ds(c * (SUPER // MERGE_DIL) + n0, merge_rows)
            rows = (in_time, in_class, in_class)
            ms = [res_m[p, rows[p], :] for p in range(3)]
            m = jnp.maximum(jnp.maximum(ms[0], ms[1]), ms[2])
            ws = [jnp.exp2(x - m) for x in ms]
            den = ws[0] * res_l[0, rows[0], :] + ws[1] * res_l[1, rows[1], :] + ws[2] * res_l[2, rows[2], :]
            num = ws[0] * res_a[0, rows[0], :] + ws[1] * res_a[1, rows[1], :] + ws[2] * res_a[2, rows[2], :]
            o_ref[0, in_time, :] = num * (1.0 / den)


def _attn_prompt(q, k, v):
    b, s, _ = q.shape
    assert s % SUPER == 0
    blk = pl.BlockSpec((1, SUPER, LANES), lambda bi, hp, sb: (bi, sb, hp))
    scratch = []
    for _, d in DILATED_PATTERNS:
        scratch.append(pltpu.VMEM((d, SUPER // d, LANES), BF16))
    for _, d in DILATED_PATTERNS:
        scratch.append(pltpu.VMEM((d, QBLK + SUPER // d, LANES), BF16))
    for _, d in DILATED_PATTERNS:
        scratch.append(pltpu.VMEM((d, QBLK + SUPER // d, 2 * LANES), BF16))
    scratch += [pltpu.VMEM((3, SUPER, LANES), F32)] * 3
    scratch += [pltpu.VMEM((2, QBLK, 2 * QBLK), F32), pltpu.VMEM((3, 4, SUPER // 4, LANES), F32)]
    return pl.pallas_call(
        _attn_prompt_kernel,
        grid=(b, D_ATTN // LANES, s // SUPER),
        in_specs=[blk, blk, blk],
        out_specs=blk,
        out_shape=jax.ShapeDtypeStruct((b, s, D_ATTN), F32),
        scratch_shapes=scratch,
        compiler_params=pltpu.CompilerParams(
            dimension_semantics=("arbitrary", "arbitrary", "arbitrary"),
            vmem_limit_bytes=VMEM_LIMIT_BYTES),
        name="attn_prompt",
    )(q, k, v)


def _attn_sample_kernel(q_ref, kn_ref, vn_ref, kt_ref, vt_ref, o_ref, *, t_len, w_buf):
    rows = N_HEADS * SUBLANES
    n_pat = len(DILATED_PATTERNS)
    pad = jnp.zeros((SUBLANES - t_len, D_ATTN), F32)
    q8 = jnp.concatenate([q_ref[0] * LOG2_E, pad], axis=0)
    kn8 = jnp.concatenate([kn_ref[0], pad], axis=0).astype(BF16)
    vn8 = jnp.concatenate([vn_ref[0], pad], axis=0).astype(BF16)
    head_lanes = lambda h: slice(h * HEAD_DIM, (h + 1) * HEAD_DIM)

    s_c = jnp.concatenate(
        [jnp.dot(q8[:, head_lanes(h)].astype(BF16), kt_ref[0, h].astype(BF16), preferred_element_type=F32)
         for h in range(N_HEADS)], axis=0)
    rid = lax.broadcasted_iota(jnp.int32, (rows, D_ATTN), 0)
    lid = lax.broadcasted_iota(jnp.int32, (rows, D_ATTN), 1)
    qbd = jnp.where((rid >> 3) == (lid >> HEAD_SHIFT), jnp.concatenate([q8] * N_HEADS, axis=0), 0.0)
    s_n = lax.dot_general(qbd.astype(BF16), kn8, (((1,), (1,)), ((), ())), preferred_element_type=F32)

    t_c = lax.broadcasted_iota(jnp.int32, (rows, w_buf), 0) & (SUBLANES - 1)
    d_c = w_buf + t_c - lax.broadcasted_iota(jnp.int32, (rows, w_buf), 1)
    t_n = lax.broadcasted_iota(jnp.int32, (rows, SUBLANES), 0) & (SUBLANES - 1)
    j_n = lax.broadcasted_iota(jnp.int32, (rows, SUBLANES), 1)
    d_n = t_n - j_n
    new_ok = (d_n >= 0) & (j_n < t_len)

    p_cs, p_ns, lses, ls = [], [], [], []
    for window, dil in DILATED_PATTERNS:
        ok_c = ((d_c & (dil - 1)) == 0) & (d_c <= window)
        ok_n = new_ok & ((d_n & (dil - 1)) == 0) & (d_n <= window)
        sc = jnp.where(ok_c, s_c, NEG_BIG)
        sn = jnp.where(ok_n, s_n, NEG_BIG)
        m = jnp.maximum(jnp.max(sc, axis=-1, keepdims=True), jnp.max(sn, axis=-1, keepdims=True))
        pc = jnp.exp2(sc - m)
        pn = jnp.exp2(sn - m)
        l = jnp.sum(pc, axis=-1, keepdims=True) + jnp.sum(pn, axis=-1, keepdims=True)
        p_cs.append(pc)
        p_ns.append(pn)
        ls.append(l)
        lses.append(m + jnp.log2(l))
    m = jnp.maximum(jnp.maximum(lses[0], lses[1]), lses[2])
    es = [jnp.exp2(x - m) for x in lses]
    inv_den = 1.0 / (es[0] + es[1] + es[2])
    wts = [es[p] * inv_den * (1.0 / ls[p]) for p in range(n_pat)]

    o_new = jnp.dot(jnp.concatenate(p_ns, axis=0).astype(BF16), vn8, preferred_element_type=F32)
    for h in range(N_HEADS):
        rs = slice(h * SUBLANES, (h + 1) * SUBLANES)
        p_h = jnp.concatenate([p_cs[p][rs] for p in range(n_pat)], axis=0).astype(BF16)
        o_h = lax.dot_general(p_h, vt_ref[0, h].astype(BF16), (((1,), (1,)), ((), ())),
                              preferred_element_type=F32)
        out = None
        for p in range(n_pat):
            new_part = o_new[p * rows + h * SUBLANES:p * rows + (h + 1) * SUBLANES, head_lanes(h)]
            term = wts[p][rs] * (o_h[p * SUBLANES:(p + 1) * SUBLANES] + new_part)
            out = term if out is None else out + term
        o_ref[0, :, head_lanes(h)] = out[0:t_len]


def _attn_sample(q, k_new, v_new, cache_k, cache_v):
    b, t_len, _ = q.shape
    w_buf = cache_k.shape[1]
    assert w_buf >= MAX_WINDOW and t_len <= SUBLANES
    new = pl.BlockSpec((1, t_len, D_ATTN), lambda bi: (bi, 0, 0))
    cache = pl.BlockSpec((1, N_HEADS, HEAD_DIM, w_buf), lambda bi: (bi, 0, 0, 0))
    row_minor = lambda c: jnp.transpose(c, (0, 2, 3, 1))
    return pl.pallas_call(
        functools.partial(_attn_sample_kernel, t_len=t_len, w_buf=w_buf),
        grid=(b,),
        in_specs=[new, new, new, cache, cache],
        out_specs=new,
        out_shape=jax.ShapeDtypeStruct((b, t_len, D_ATTN), F32),
        compiler_params=pltpu.CompilerParams(
            dimension_semantics=("arbitrary",), vmem_limit_bytes=VMEM_LIMIT_BYTES),
        name="attn_sample",
    )(q, k_new, v_new, row_minor(cache_k), row_minor(cache_v))


def _mix_ffn_kernel(x_ref, yl_ref, ya_ref, ga_ref, wo_ref, gf_ref, wg_ref, wu_ref, wd_ref, gn_ref,
                    y_ref, *, final_norm):
    ya = ya_ref[...]
    ya_n = (ya * _rms_scale(ya) * ga_ref[...]).astype(BF16)
    mix = (jnp.dot(yl_ref[...], wo_ref[0:D_LRU, :], preferred_element_type=F32)
           + jnp.dot(ya_n, wo_ref[D_LRU:D_LRU + D_ATTN, :], preferred_element_type=F32))
    x1 = x_ref[...] + mix
    xn = (x1 * _rms_scale(x1) * gf_ref[...]).astype(BF16)
    x2 = x1
    for c0 in range(0, D_FF, FF_CHUNK):
        cols = slice(c0, c0 + FF_CHUNK)
        hg = jnp.dot(xn, wg_ref[:, cols], preferred_element_type=F32)
        hu = jnp.dot(xn, wu_ref[:, cols], preferred_element_type=F32)
        act = (hg * _sigmoid(hg) * hu).astype(BF16)
        x2 = x2 + jnp.dot(act, wd_ref[cols, :], preferred_element_type=F32)
    if final_norm:
        x2 = x2 * _rms_scale(x2) * gn_ref[...]
    y_ref[...] = x2


def _mix_ffn(x2d, yl, ya, pw, g_final, tm, final_norm):
    n = x2d.shape[0]
    row = lambda i: (i, 0)
    const = lambda i: (0, 0)
    once = pl.Buffered(1)
    return pl.pallas_call(
        functools.partial(_mix_ffn_kernel, final_norm=final_norm),
        grid=(n // tm,),
        in_specs=[
            pl.BlockSpec((tm, D_MODEL), row),
            pl.BlockSpec((tm, D_LRU), row),
            pl.BlockSpec((tm, D_ATTN), row),
            pl.BlockSpec((1, D_ATTN), const),
            pl.BlockSpec((D_LRU + D_ATTN, D_MODEL), const, pipeline_mode=once),
            pl.BlockSpec((1, D_MODEL), const),
            pl.BlockSpec((D_MODEL, D_FF), const, pipeline_mode=once),
            pl.BlockSpec((D_MODEL, D_FF), const, pipeline_mode=once),
            pl.BlockSpec((D_FF, D_MODEL), const, pipeline_mode=once),
            pl.BlockSpec((1, D_MODEL), const),
        ],
        out_specs=pl.BlockSpec((tm, D_MODEL), row),
        out_shape=jax.ShapeDtypeStruct((n, D_MODEL), F32),
        compiler_params=pltpu.CompilerParams(
            dimension_semantics=("arbitrary",), vmem_limit_bytes=VMEM_LIMIT_BYTES),
        name="mix_ffn",
    )(x2d, yl, ya, pw["g_attn"], pw["w_out"], pw["g_ffn"], pw["w_gate"], pw["w_up"], pw["w_down"],
      g_final)


def _block_diag_gate_weights(w_a, w_x):
    half_blocks = N_LRU_BLOCKS // 2
    half = D_LRU // 2

    def bd(w):
        eye = jnp.eye(half_blocks, dtype=w.dtype)
        return jnp.einsum("nij,nm->nimj", w, eye).reshape(half, half)

    halves = [jnp.concatenate([bd(w_a[c * half_blocks:(c + 1) * half_blocks]),
                               bd(w_x[c * half_blocks:(c + 1) * half_blocks])], axis=1)
              for c in range(2)]
    return jnp.stack(halves, axis=0).astype(BF16)


def _row(v):
    return v.reshape(1, -1).astype(F32)


def kernel(x_prompt, x_sample, state_conv, state_lru, cache_k, cache_v, norm_mix, w_in, conv_w, conv_b,
           lru_w_a, lru_b_a, lru_w_x, lru_b_x, lru_lambda, out_norm_lru, out_norm_attn, w_out, norm_ffn,
           w_gate, w_up, w_down, norm_final):
    depth = w_in.shape[0]
    bp, sp, _ = x_prompt.shape
    bs, ts, _ = x_sample.shape
    w_p = min(MAX_WINDOW, sp)
    assert ts >= CONV_WIDTH - 1
    xp = x_prompt.reshape(bp * sp, D_MODEL)
    xs = x_sample.reshape(bs * ts, D_MODEL)
    g_final = _row(norm_final)
    conv_p, lru_p, kw_p, vw_p = [], [], [], []
    conv_s, lru_s, kn_s, vn_s = [], [], [], []
    for l in range(depth):
        last = l == depth - 1
        w_in_bf = w_in[l].astype(BF16)
        g_mix = _row(norm_mix[l])
        lw = dict(
            conv_w=conv_w[l].astype(F32), conv_b=_row(conv_b[l]),
            w_gates=_block_diag_gate_weights(lru_w_a[l], lru_w_x[l]),
            b_a=_row(lru_b_a[l]), b_x=_row(lru_b_x[l]),
            neg_c_softplus=_row(-LRU_C * jax.nn.softplus(-lru_lambda[l].astype(F32))),
            g_lru=_row(out_norm_lru[l]))
        pw = dict(
            g_attn=_row(out_norm_attn[l]), w_out=w_out[l].astype(BF16), g_ffn=_row(norm_ffn[l]),
            w_gate=w_gate[l].astype(BF16), w_up=w_up[l].astype(BF16), w_down=w_down[l].astype(BF16))

        u, gate, q, k, v, k_win, v_win = _inproj_window(xp, g_mix, w_in_bf, tm=1024, seq_len=sp, window=w_p)
        shp = lambda a: a.reshape(bp, sp, -1)
        yl, c_new, h_last = _lru_prompt(
            shp(u), shp(gate), jnp.zeros((bp, CONV_WIDTH - 1, D_LRU), F32),
            jnp.zeros((bp, 1, D_LRU), F32), lw, tt=2048)
        ya = _attn_prompt(shp(q), shp(k), shp(v))
        xp = _mix_ffn(xp, yl.reshape(bp * sp, D_LRU), ya.reshape(bp * sp, D_ATTN), pw, g_final,
                      tm=1024, final_norm=last)
        conv_p.append(c_new)
        lru_p.append(h_last.reshape(bp, D_LRU))
        win = lambda a: a.reshape(bp, N_HEADS, HEAD_DIM, w_p).transpose(0, 3, 1, 2)
        kw_p.append(win(k_win))
        vw_p.append(win(v_win))

        u, gate, q, k, v = _inproj(xs, g_mix, w_in_bf, tm=bs * ts)
        tmaj = lambda a: a.reshape(bs, ts, -1).transpose(1, 0, 2)
        yl_t, c_new_t, h_last = _lru_sample(
            tmaj(u), tmaj(gate), state_conv[l].astype(F32).transpose(1, 0, 2),
            state_lru[l].astype(F32), lw)
        seq = lambda a: a.reshape(bs, ts, D_ATTN)
        ya = _attn_sample(seq(q), seq(k), seq(v), cache_k[l].astype(F32), cache_v[l].astype(F32))
        xs = _mix_ffn(xs, yl_t.transpose(1, 0, 2).reshape(bs * ts, D_LRU), ya.reshape(bs * ts, D_ATTN),
                      pw, g_final, tm=bs * ts, final_norm=last)
        conv_s.append(c_new_t.transpose(1, 0, 2))
        lru_s.append(h_last)
        kn_s.append(k.reshape(bs, ts, N_HEADS, HEAD_DIM))
        vn_s.append(v.reshape(bs, ts, N_HEADS, HEAD_DIM))
    return (xp.reshape(bp, sp, D_MODEL), xs.reshape(bs, ts, D_MODEL),
            jnp.stack(conv_p, 0), jnp.stack(lru_p, 0), jnp.stack(kw_p, 0), jnp.stack(vw_p, 0),
            jnp.stack(conv_s, 0), jnp.stack(lru_s, 0), jnp.stack(kn_s, 0), jnp.stack(vn_s, 0))
```

```python
import functools
import math

import jax
import jax.numpy as jnp
from jax import lax
from jax.experimental import pallas as pl
from jax.experimental.pallas import tpu as pltpu

F32 = jnp.float32
BF16 = jnp.bfloat16

D_MODEL = 1024
D_LRU = 512
D_ATTN = 512
N_HEADS = 8
HEAD_DIM = 64
N_LRU_BLOCKS = 8
LRU_BLOCK = D_LRU // N_LRU_BLOCKS
CONV_WIDTH = 4
LRU_C = 8.0
DILATED_PATTERNS = ((128, 1), (512, 4), (2048, 16))
MAX_WINDOW = 2048
D_FF = 2816
D_IN = 2 * D_LRU + 3 * D_ATTN
RMS_EPS = 1e-6
Q_SCALE = HEAD_DIM ** -0.5

LANES = 128
SUBLANES = 8
VMEM_LIMIT_BYTES = 56 * 1024 * 1024

KEYS_PER_PATTERN = 129
QBLK = 128
SUPER = 2048
HEADS_PER_STEP = LANES // HEAD_DIM
NEG_BIG = -1e30
LOG2_E = math.log2(math.e)
MERGE_DIL = 4
LRU_CHUNK = 256
SEG_LEN = LRU_CHUNK // SUBLANES
FF_CHUNK = 256
assert D_FF % FF_CHUNK == 0
HEAD_SHIFT = HEAD_DIM.bit_length() - 1
SUBLANE_SHIFT = SUBLANES.bit_length() - 1
assert all(d & (d - 1) == 0 for _, d in DILATED_PATTERNS) and 1 << HEAD_SHIFT == HEAD_DIM
assert all(w // d + 1 == KEYS_PER_PATTERN for w, d in DILATED_PATTERNS)


def _rms_scale(x):
    return lax.rsqrt(jnp.mean(x * x, axis=-1, keepdims=True) + RMS_EPS)


def _gelu_tanh(x):
    c = math.sqrt(2.0 / math.pi)
    t = jnp.tanh(x * (c + (0.044715 * c) * (x * x)))
    half = 0.5 * x
    return half + half * t


def _normed(x_ref, g_ref):
    x = x_ref[...]
    return (x * _rms_scale(x) * g_ref[...]).astype(BF16)


def _project(xn, w_ref, first_col, width=D_LRU):
    return jnp.dot(xn, w_ref[:, first_col:first_col + width], preferred_element_type=F32)


def _project_all(x_ref, g_ref, w_ref, u_ref, gate_ref, q_ref, k_ref, v_ref):
    xn = _normed(x_ref, g_ref)
    u_ref[...] = _project(xn, w_ref, 0)
    gate_ref[...] = _project(xn, w_ref, D_LRU)
    o = 2 * D_LRU
    q_ref[...] = _project(xn, w_ref, o, D_ATTN) * Q_SCALE
    k = _project(xn, w_ref, o + D_ATTN, D_ATTN)
    v = _project(xn, w_ref, o + 2 * D_ATTN, D_ATTN)
    k_ref[...] = k
    v_ref[...] = v
    return k, v


def _inproj_kernel(x_ref, g_ref, w_ref, u_ref, gate_ref, q_ref, k_ref, v_ref):
    _project_all(x_ref, g_ref, w_ref, u_ref, gate_ref, q_ref, k_ref, v_ref)


def _inproj_window_kernel(x_ref, g_ref, w_ref, u_ref, gate_ref, q_ref, k_ref, v_ref, kw_ref, vw_ref):
    k, v = _project_all(x_ref, g_ref, w_ref, u_ref, gate_ref, q_ref, k_ref, v_ref)
    kw_ref[0] = k.T
    vw_ref[0] = v.T


def _inproj(x2d, g, w_bf, tm):
    n = x2d.shape[0]
    assert n % tm == 0
    const = lambda i: (0, 0)
    row = lambda i: (i, 0)
    return pl.pallas_call(
        _inproj_kernel,
        grid=(n // tm,),
        in_specs=[
            pl.BlockSpec((tm, D_MODEL), row),
            pl.BlockSpec((1, D_MODEL), const),
            pl.BlockSpec((D_MODEL, D_IN), const, pipeline_mode=pl.Buffered(1)),
        ],
        out_specs=[pl.BlockSpec((tm, D_LRU), row)] * 5,
        out_shape=[jax.ShapeDtypeStruct((n, D_LRU), F32)] * 5,
        compiler_params=pltpu.CompilerParams(
            dimension_semantics=("arbitrary",), vmem_limit_bytes=VMEM_LIMIT_BYTES),
        name="inproj",
    )(x2d, g, w_bf)


def _inproj_window(x2d, g, w_bf, tm, seq_len, window):
    n = x2d.shape[0]
    assert n % seq_len == 0 and seq_len % tm == 0 and window % tm == 0
    per_seq, per_win, n_seq = seq_len // tm, window // tm, n // seq_len
    const = lambda i: (0, 0)
    row = lambda i: (i, 0)
    win_block = lambda i: (i // per_seq, 0, jnp.maximum(i % per_seq - (per_seq - per_win), 0))
    return pl.pallas_call(
        _inproj_window_kernel,
        grid=(n // tm,),
        in_specs=[
            pl.BlockSpec((tm, D_MODEL), row),
            pl.BlockSpec((1, D_MODEL), const),
            pl.BlockSpec((D_MODEL, D_IN), const, pipeline_mode=pl.Buffered(1)),
        ],
        out_specs=[pl.BlockSpec((tm, D_LRU), row)] * 5 + [pl.BlockSpec((1, D_ATTN, tm), win_block)] * 2,
        out_shape=[jax.ShapeDtypeStruct((n, D_LRU), F32)] * 5
        + [jax.ShapeDtypeStruct((n_seq, D_ATTN, window), F32)] * 2,
        compiler_params=pltpu.CompilerParams(
            dimension_semantics=("arbitrary",), vmem_limit_bytes=VMEM_LIMIT_BYTES),
        name="inproj_window",
    )(x2d, g, w_bf)


def _sigmoid(x):
    return jax.nn.sigmoid(x)


def _lru_gates(uc, wg_ref, ba, bx_bias, neg_c_softplus):
    half = D_LRU // 2
    ucb = uc.astype(BF16)
    pre0 = jnp.dot(ucb[:, 0:half], wg_ref[0], preferred_element_type=F32)
    pre1 = jnp.dot(ucb[:, half:D_LRU], wg_ref[1], preferred_element_type=F32)
    r_pre = jnp.concatenate([pre0[:, 0:half], pre1[:, 0:half]], axis=1) + ba
    i_pre = jnp.concatenate([pre0[:, half:], pre1[:, half:]], axis=1) + bx_bias
    r = _sigmoid(r_pre)
    i = _sigmoid(i_pre)
    log_a = r * neg_c_softplus
    a = jnp.exp(log_a)
    z = -jnp.tanh(log_a) * (a * a + 1.0)
    root = jnp.where(z > 0.0, z * lax.rsqrt(z), 0.0)
    b = root * (i * uc)
    return a, b


def _shift_rows(x, s, fill):
    t = x.shape[0]
    if s % SUBLANES == 0:
        return jnp.concatenate([jnp.full((s, x.shape[1]), fill, x.dtype), x[0:t - s]], axis=0)
    rolled = pltpu.roll(x, s, axis=0)
    row = lax.broadcasted_iota(jnp.int32, x.shape, 0)
    return jnp.where(row < s, fill, rolled)


def _scan_rows(a, b):
    t = a.shape[0]
    s = 1
    while s < t:
        b = a * _shift_rows(b, s, 0.0) + b
        if 2 * s < t:
            a = a * _shift_rows(a, s, 1.0)
        s *= 2
    return b


def _lru_prompt_kernel(u_ref, gate_ref, cprev_ref, h0_ref, cw_ref, cb_ref, wg_ref, ba_ref, bx_ref,
                       ncs_ref, gl_ref, y_ref, conv_ref, hlast_ref, tail_ref, h_ref, ubuf_ref, gbuf_ref,
                       ybuf_ref, *, n_chunks):
    ti = pl.program_id(1)
    lane_chunks = D_LRU // LANES

    def to_segment_major(src_ref, r0, buf_ref):
        for s in range(SUBLANES):
            rows = pl.ds(pl.multiple_of(r0 + s * SEG_LEN, SEG_LEN), SEG_LEN)
            for c in range(lane_chunks):
                buf_ref[c, pl.ds(s, SEG_LEN, stride=SUBLANES), :] = src_ref[0, rows, c * LANES:(c + 1) * LANES]
        return jnp.concatenate([buf_ref[c] for c in range(lane_chunks)], axis=1)
    n_t = pl.num_programs(1)

    @pl.when(ti == 0)
    def _():
        tail_ref[...] = jnp.zeros_like(tail_ref)
        tail_ref[SUBLANES - (CONV_WIDTH - 1):SUBLANES, :] = cprev_ref[0]
        h_ref[...] = jnp.broadcast_to(h0_ref[0], h_ref.shape)

    cw = cw_ref[...]
    cb = cb_ref[...]
    ba = ba_ref[...]
    bxb = bx_ref[...]
    ncs = ncs_ref[...]
    gl = gl_ref[...]
    row8 = lax.broadcasted_iota(jnp.int32, (SUBLANES, D_LRU), 0)
    seg0 = row8 == 0
    n_prev = CONV_WIDTH - 1

    def vrow(x, tau):
        return x[SUBLANES * tau:SUBLANES * (tau + 1)]

    def chunk(ci, carry):
        r0 = pl.multiple_of(ci * LRU_CHUNK, LRU_CHUNK)
        u = to_segment_major(u_ref, r0, ubuf_ref)
        tail = tail_ref[...]
        pre = []
        for k in range(n_prev, 0, -1):
            from_prev_seg = pltpu.roll(vrow(u, SEG_LEN - k), 1, axis=0)
            pre.append(jnp.where(seg0, tail[SUBLANES - k:SUBLANES - k + 1], from_prev_seg))
            tail_ref[SUBLANES - k:SUBLANES - k + 1, :] = u[SUBLANES * (SEG_LEN - k + 1) - 1:
                                                           SUBLANES * (SEG_LEN - k + 1)]
        ext = jnp.concatenate(pre + [u], axis=0)
        uc = cb
        for j in range(CONV_WIDTH):
            uc = uc + cw[j:j + 1, :] * ext[SUBLANES * j:SUBLANES * j + LRU_CHUNK]
        a, b = _lru_gates(uc, wg_ref, ba, bxb, ncs)

        h = vrow(b, 0)
        p = vrow(a, 0)
        for tau in range(1, SEG_LEN):
            a_t = vrow(a, tau)
            h = a_t * h + vrow(b, tau)
            p = a_t * p
        h_in = h_ref[0:1, :]
        seg_end = _scan_rows(p, jnp.where(seg0, h + p * h_in, h))
        h = jnp.where(seg0, h_in, pltpu.roll(seg_end, 1, axis=0))
        h_ref[...] = jnp.broadcast_to(seg_end[SUBLANES - 1:SUBLANES], h_ref.shape)
        hs = []
        for tau in range(SEG_LEN):
            h = vrow(a, tau) * h + vrow(b, tau)
            hs.append(h)
        h_all = jnp.concatenate(hs, axis=0)

        y = h_all * _gelu_tanh(to_segment_major(gate_ref, r0, gbuf_ref))
        yn = y * _rms_scale(y) * gl
        for c in range(lane_chunks):
            ybuf_ref[c] = yn[:, c * LANES:(c + 1) * LANES]
        for s in range(SUBLANES):
            rows = jnp.concatenate([ybuf_ref[c, pl.ds(s, SEG_LEN, stride=SUBLANES), :]
                                    for c in range(lane_chunks)], axis=1)
            y_ref[0, pl.ds(pl.multiple_of(r0 + s * SEG_LEN, SEG_LEN), SEG_LEN), :] = rows.astype(BF16)
        return carry

    lax.fori_loop(0, n_chunks, chunk, 0)

    @pl.when(ti == n_t - 1)
    def _():
        conv_ref[0] = tail_ref[SUBLANES - (CONV_WIDTH - 1):SUBLANES, :]
        hlast_ref[0] = h_ref[0:1, :]


def _lru_prompt(u, gate, conv_prev, h0, lw, tt):
    b, s, _ = u.shape
    assert s % tt == 0 and tt % LRU_CHUNK == 0
    seq = lambda bi, ti: (bi, ti, 0)
    per_b = lambda bi, ti: (bi, 0, 0)
    c2 = lambda bi, ti: (0, 0)
    c3 = lambda bi, ti: (0, 0, 0)
    vec = pl.BlockSpec((1, D_LRU), c2)
    return pl.pallas_call(
        functools.partial(_lru_prompt_kernel, n_chunks=tt // LRU_CHUNK),
        grid=(b, s // tt),
        in_specs=[
            pl.BlockSpec((1, tt, D_LRU), seq),
            pl.BlockSpec((1, tt, D_LRU), seq),
            pl.BlockSpec((1, CONV_WIDTH - 1, D_LRU), per_b),
            pl.BlockSpec((1, 1, D_LRU), per_b),
            pl.BlockSpec((CONV_WIDTH, D_LRU), c2),
            vec,
            pl.BlockSpec((2, D_LRU // 2, D_LRU), c3),
            vec, vec, vec, vec,
        ],
        out_specs=[
            pl.BlockSpec((1, tt, D_LRU), seq),
            pl.BlockSpec((1, CONV_WIDTH - 1, D_LRU), per_b),
            pl.BlockSpec((1, 1, D_LRU), per_b),
        ],
        out_shape=[
            jax.ShapeDtypeStruct((b, s, D_LRU), BF16),
            jax.ShapeDtypeStruct((b, CONV_WIDTH - 1, D_LRU), F32),
            jax.ShapeDtypeStruct((b, 1, D_LRU), F32),
        ],
        scratch_shapes=[pltpu.VMEM((SUBLANES, D_LRU), F32), pltpu.VMEM((SUBLANES, D_LRU), F32)]
        + [pltpu.VMEM((D_LRU // LANES, LRU_CHUNK, LANES), F32)] * 3,
        compiler_params=pltpu.CompilerParams(
            dimension_semantics=("arbitrary", "arbitrary"), vmem_limit_bytes=VMEM_LIMIT_BYTES),
        name="lru_prompt",
    )(u, gate, conv_prev, h0, lw["conv_w"], lw["conv_b"], lw["w_gates"], lw["b_a"], lw["b_x"],
      lw["neg_c_softplus"], lw["g_lru"])


def _lru_sample_kernel(u_ref, gate_ref, cprev_ref, h0_ref, cw_ref, cb_ref, wg_ref, ba_ref, bx_ref,
                       ncs_ref, gl_ref, y_ref, conv_ref, hlast_ref):
    t_len, nb, _ = u_ref.shape
    cw = cw_ref[...]
    hist = [cprev_ref[j] for j in range(CONV_WIDTH - 1)] + [u_ref[t] for t in range(t_len)]
    ucs = []
    for t in range(t_len):
        uc = cb_ref[...]
        for j in range(CONV_WIDTH):
            uc = uc + cw[j:j + 1, :] * hist[t + j]
        ucs.append(uc)
    uc_all = jnp.concatenate(ucs, axis=0)
    a, b = _lru_gates(uc_all, wg_ref, ba_ref[...], bx_ref[...], ncs_ref[...])
    h = h0_ref[...]
    gl = gl_ref[...]
    for t in range(t_len):
        h = a[t * nb:(t + 1) * nb] * h + b[t * nb:(t + 1) * nb]
        y = h * _gelu_tanh(gate_ref[t])
        y_ref[t] = (y * _rms_scale(y) * gl).astype(BF16)
    hlast_ref[...] = h
    for j in range(CONV_WIDTH - 1):
        conv_ref[j] = hist[t_len + j]


def _lru_sample(u_t, gate_t, conv_prev_t, h0, lw):
    t_len, nb, _ = u_t.shape
    return pl.pallas_call(
        _lru_sample_kernel,
        out_shape=[
            jax.ShapeDtypeStruct((t_len, nb, D_LRU), BF16),
            jax.ShapeDtypeStruct((CONV_WIDTH - 1, nb, D_LRU), F32),
            jax.ShapeDtypeStruct((nb, D_LRU), F32),
        ],
        name="lru_sample",
    )(u_t, gate_t, conv_prev_t, h0, lw["conv_w"], lw["conv_b"], lw["w_gates"], lw["b_a"], lw["b_x"],
      lw["neg_c_softplus"], lw["g_lru"])


def _attn_prompt_kernel(q_ref, k_ref, v_ref, o_ref,
                        qp1, qp4, qp16, kb1, kb4, kb16, vb1, vb4, vb16, res_a, res_l, res_m, bias_ref,
                        stage_ref):
    sb = pl.program_id(2)
    qps = (qp1, qp4, qp16)
    kbs = (kb1, kb4, kb16)
    vbs = (vb1, vb4, vb16)
    dils = tuple(d for _, d in DILATED_PATTERNS)
    assert dils == (1, 4, 16)
    lane = lax.broadcasted_iota(jnp.int32, (QBLK, LANES), 1)
    head0 = lane < HEAD_DIM

    qi = lax.broadcasted_iota(jnp.int32, (QBLK, 2 * QBLK), 0)
    ki = lax.broadcasted_iota(jnp.int32, (QBLK, 2 * QBLK), 1)
    delta = qi + QBLK - ki
    band = (delta >= 0) & (delta <= KEYS_PER_PATTERN - 1)
    bias_ref[0] = jnp.where(band, 0.0, NEG_BIG)
    bias_ref[1] = jnp.where(band & (ki >= QBLK), 0.0, NEG_BIG)

    for kb, vb, d in zip(kbs, vbs, dils):
        cls_len = SUPER // d

        @pl.when(sb == 0)
        def _():
            kb[:, 0:QBLK, :] = jnp.zeros((d, QBLK, LANES), BF16)
            vb[:, 0:QBLK, 0:LANES] = jnp.zeros((d, QBLK, LANES), BF16)
            vb[:, :, LANES:2 * LANES] = jnp.ones((d, QBLK + cls_len, LANES), BF16)

        @pl.when(sb > 0)
        def _():
            kb[:, 0:QBLK, :] = kb[:, cls_len:cls_len + QBLK, :]
            vb[:, 0:QBLK, 0:LANES] = vb[:, cls_len:cls_len + QBLK, 0:LANES]

    def put(x, p, c, which):
        n_rows = x.shape[0]
        if which == "q":
            qps[p][c] = (x * LOG2_E).astype(BF16)
        elif which == "k":
            kbs[p][c, QBLK:QBLK + n_rows, :] = x.astype(BF16)
        else:
            vbs[p][c, QBLK:QBLK + n_rows, 0:LANES] = x.astype(BF16)

    def tile(p, c, i):
        d = dils[p]
        r0 = i * QBLK
        q_t = qps[p][c, pl.ds(r0, QBLK), :]
        k_t = kbs[p][c, pl.ds(r0, 2 * QBLK), :]
        bias = bias_ref[(sb == 0).astype(jnp.int32)] if i == 0 else bias_ref[0]
        probs, maxes = [], []
        for h in range(HEADS_PER_STEP):
            hm = head0 if h == 0 else jnp.logical_not(head0)
            qh = jnp.where(hm, q_t, jnp.zeros_like(q_t))
            s = lax.dot_general(qh, k_t, (((1,), (1,)), ((), ())), preferred_element_type=F32)
            s = s + bias
            m = jnp.max(s, axis=-1, keepdims=True)
            probs.append(jnp.exp2(s - m).astype(BF16))
            maxes.append(jnp.broadcast_to(m, (QBLK, LANES)))
        v_t = vbs[p][c, pl.ds(r0, 2 * QBLK), :]
        pv = jnp.dot(jnp.concatenate(probs, axis=0), v_t, preferred_element_type=F32)
        pv0, pv1 = pv[0:QBLK], pv[QBLK:2 * QBLK]
        if d == 1:
            dst = pl.ds(r0, QBLK)
        elif d == MERGE_DIL:
            dst = pl.ds(c * (SUPER // MERGE_DIL) + r0, QBLK)
        else:
            dst = pl.ds((c % MERGE_DIL) * (SUPER // MERGE_DIL) + c // MERGE_DIL, QBLK, stride=MERGE_DIL)
        res_a[p, dst, :] = jnp.where(head0, pv0[:, 0:LANES], pv1[:, 0:LANES])
        res_l[p, dst, :] = jnp.where(head0, pv0[:, LANES:2 * LANES], pv1[:, LANES:2 * LANES])
        res_m[p, dst, :] = jnp.where(head0, maxes[0], maxes[1])

    sources = (("q", q_ref), ("k", k_ref), ("v", v_ref))
    for which, ref in sources:
        put(ref[0], 0, 0, which)
    for i in range(SUPER // QBLK):
        tile(0, 0, i)
    for idx, (which, ref) in enumerate(sources):
        for c in range(4):
            x = ref[0, pl.ds(c, SUPER // 4, stride=4), :]
            stage_ref[idx, c] = x
            put(x, 1, c, which)
        for r in range(16):
            put(stage_ref[idx, r % 4, pl.ds(r // 4, QBLK, stride=4), :], 2, r, which)

    merge_rows = 256
    for c in range(MERGE_DIL):
        for i in range(SUPER // MERGE_DIL // QBLK):
            tile(1, c, i)
        for a in range(dils[2] // MERGE_DIL):
            tile(2, a * MERGE_DIL + c, 0)
        for n0 in range(0, SUPER // MERGE_DIL, merge_rows):
            in_time = pl.ds(MERGE_DIL * n0 + c, merge_rows, stride=MERGE_DIL)
            in_class = pl.ds(c * (SUPER // MERGE_DIL) + n0, merge_rows)
            rows = (in_time, in_class, in_class)
            ms = [res_m[p, rows[p], :] for p in range(3)]
            m = jnp.maximum(jnp.maximum(ms[0], ms[1]), ms[2])
            ws = [jnp.exp2(x - m) for x in ms]
            den = ws[0] * res_l[0, rows[0], :] + ws[1] * res_l[1, rows[1], :] + ws[2] * res_l[2, rows[2], :]
            num = ws[0] * res_a[0, rows[0], :] + ws[1] * res_a[1, rows[1], :] + ws[2] * res_a[2, rows[2], :]
            o_ref[0, in_time, :] = num * (1.0 / den)


def _attn_prompt(q, k, v):
    b, s, _ = q.shape
    assert s % SUPER == 0
    blk = pl.BlockSpec((1, SUPER, LANES), lambda bi, hp, sb: (bi, sb, hp))
    scratch = []
    for _, d in DILATED_PATTERNS:
        scratch.append(pltpu.VMEM((d, SUPER // d, LANES), BF16))
    for _, d in DILATED_PATTERNS:
        scratch.append(pltpu.VMEM((d, QBLK + SUPER // d, LANES), BF16))
    for _, d in DILATED_PATTERNS:
        scratch.append(pltpu.VMEM((d, QBLK + SUPER // d, 2 * LANES), BF16))
    scratch += [pltpu.VMEM((3, SUPER, LANES), F32)] * 3
    scratch += [pltpu.VMEM((2, QBLK, 2 * QBLK), F32), pltpu.VMEM((3, 4, SUPER // 4, LANES), F32)]
    return pl.pallas_call(
        _attn_prompt_kernel,
        grid=(b, D_ATTN // LANES, s // SUPER),
        in_specs=[blk, blk, blk],
        out_specs=blk,
        out_shape=jax.ShapeDtypeStruct((b, s, D_ATTN), F32),
        scratch_shapes=scratch,
        compiler_params=pltpu.CompilerParams(
            dimension_semantics=("arbitrary", "arbitrary", "arbitrary"),
            vmem_limit_bytes=VMEM_LIMIT_BYTES),
        name="attn_prompt",
    )(q, k, v)


def _attn_sample_kernel(q_ref, kn_ref, vn_ref, kt_ref, vt_ref, o_ref, *, t_len, w_buf):
    rows = N_HEADS * SUBLANES
    n_pat = len(DILATED_PATTERNS)
    pad = jnp.zeros((SUBLANES - t_len, D_ATTN), F32)
    q8 = jnp.concatenate([q_ref[0] * LOG2_E, pad], axis=0)
    kn8 = jnp.concatenate([kn_ref[0], pad], axis=0).astype(BF16)
    vn8 = jnp.concatenate([vn_ref[0], pad], axis=0).astype(BF16)
    head_lanes = lambda h: slice(h * HEAD_DIM, (h + 1) * HEAD_DIM)

    s_c = jnp.concatenate(
        [jnp.dot(q8[:, head_lanes(h)].astype(BF16), kt_ref[0, h].astype(BF16), preferred_element_type=F32)
         for h in range(N_HEADS)], axis=0)
    rid = lax.broadcasted_iota(jnp.int32, (rows, D_ATTN), 0)
    lid = lax.broadcasted_iota(jnp.int32, (rows, D_ATTN), 1)
    qbd = jnp.where((rid >> SUBLANE_SHIFT) == (lid >> HEAD_SHIFT), jnp.concatenate([q8] * N_HEADS, axis=0), 0.0)
    s_n = lax.dot_general(qbd.astype(BF16), kn8, (((1,), (1,)), ((), ())), preferred_element_type=F32)

    t_c = lax.broadcasted_iota(jnp.int32, (rows, w_buf), 0) & (SUBLANES - 1)
    d_c = w_buf + t_c - lax.broadcasted_iota(jnp.int32, (rows, w_buf), 1)
    t_n = lax.broadcasted_iota(jnp.int32, (rows, SUBLANES), 0) & (SUBLANES - 1)
    j_n = lax.broadcasted_iota(jnp.int32, (rows, SUBLANES), 1)
    d_n = t_n - j_n
    new_ok = (d_n >= 0) & (j_n < t_len)

    p_cs, p_ns, lses, ls = [], [], [], []
    for window, dil in DILATED_PATTERNS:
        ok_c = ((d_c & (dil - 1)) == 0) & (d_c <= window)
        ok_n = new_ok & ((d_n & (dil - 1)) == 0) & (d_n <= window)
        sc = jnp.where(ok_c, s_c, NEG_BIG)
        sn = jnp.where(ok_n, s_n, NEG_BIG)
        m = jnp.maximum(jnp.max(sc, axis=-1, keepdims=True), jnp.max(sn, axis=-1, keepdims=True))
        pc = jnp.exp2(sc - m)
        pn = jnp.exp2(sn - m)
        l = jnp.sum(pc, axis=-1, keepdims=True) + jnp.sum(pn, axis=-1, keepdims=True)
        p_cs.append(pc)
        p_ns.append(pn)
        ls.append(l)
        lses.append(m + jnp.log2(l))
    m = jnp.maximum(jnp.maximum(lses[0], lses[1]), lses[2])
    es = [jnp.exp2(x - m) for x in lses]
    inv_den = 1.0 / (es[0] + es[1] + es[2])
    wts = [es[p] * inv_den * (1.0 / ls[p]) for p in range(n_pat)]

    o_new = jnp.dot(jnp.concatenate(p_ns, axis=0).astype(BF16), vn8, preferred_element_type=F32)
    for h in range(N_HEADS):
        rs = slice(h * SUBLANES, (h + 1) * SUBLANES)
        p_h = jnp.concatenate([p_cs[p][rs] for p in range(n_pat)], axis=0).astype(BF16)
        o_h = lax.dot_general(p_h, vt_ref[0, h].astype(BF16), (((1,), (1,)), ((), ())),
                              preferred_element_type=F32)
        out = None
        for p in range(n_pat):
            new_part = o_new[p * rows + h * SUBLANES:p * rows + (h + 1) * SUBLANES, head_lanes(h)]
            term = wts[p][rs] * (o_h[p * SUBLANES:(p + 1) * SUBLANES] + new_part)
            out = term if out is None else out + term
        o_ref[0, :, head_lanes(h)] = out[0:t_len]


def _attn_sample(q, k_new, v_new, cache_k, cache_v):
    b, t_len, _ = q.shape
    w_buf = cache_k.shape[1]
    assert w_buf >= MAX_WINDOW and t_len <= SUBLANES
    new = pl.BlockSpec((1, t_len, D_ATTN), lambda bi: (bi, 0, 0))
    cache = pl.BlockSpec((1, N_HEADS, HEAD_DIM, w_buf), lambda bi: (bi, 0, 0, 0))
    row_minor = lambda c: jnp.transpose(c, (0, 2, 3, 1))
    return pl.pallas_call(
        functools.partial(_attn_sample_kernel, t_len=t_len, w_buf=w_buf),
        grid=(b,),
        in_specs=[new, new, new, cache, cache],
        out_specs=new,
        out_shape=jax.ShapeDtypeStruct((b, t_len, D_ATTN), F32),
        compiler_params=pltpu.CompilerParams(
            dimension_semantics=("arbitrary",), vmem_limit_bytes=VMEM_LIMIT_BYTES),
        name="attn_sample",
    )(q, k_new, v_new, row_minor(cache_k), row_minor(cache_v))


def _mix_ffn_kernel(x_ref, yl_ref, ya_ref, ga_ref, wo_ref, gf_ref, wg_ref, wu_ref, wd_ref, gn_ref,
                    y_ref, *, final_norm):
    ya = ya_ref[...]
    ya_n = (ya * _rms_scale(ya) * ga_ref[...]).astype(BF16)
    mix = (jnp.dot(yl_ref[...], wo_ref[0:D_LRU, :], preferred_element_type=F32)
           + jnp.dot(ya_n, wo_ref[D_LRU:D_LRU + D_ATTN, :], preferred_element_type=F32))
    x1 = x_ref[...] + mix
    xn = (x1 * _rms_scale(x1) * gf_ref[...]).astype(BF16)
    x2 = x1
    for c0 in range(0, D_FF, FF_CHUNK):
        cols = slice(c0, c0 + FF_CHUNK)
        hg = jnp.dot(xn, wg_ref[:, cols], preferred_element_type=F32)
        hu = jnp.dot(xn, wu_ref[:, cols], preferred_element_type=F32)
        act = (hg * _sigmoid(hg) * hu).astype(BF16)
        x2 = x2 + jnp.dot(act, wd_ref[cols, :], preferred_element_type=F32)
    if final_norm:
        x2 = x2 * _rms_scale(x2) * gn_ref[...]
    y_ref[...] = x2


def _mix_ffn(x2d, yl, ya, pw, g_final, tm, final_norm):
    n = x2d.shape[0]
    row = lambda i: (i, 0)
    const = lambda i: (0, 0)
    once = pl.Buffered(1)
    return pl.pallas_call(
        functools.partial(_mix_ffn_kernel, final_norm=final_norm),
        grid=(n // tm,),
        in_specs=[
            pl.BlockSpec((tm, D_MODEL), row),
            pl.BlockSpec((tm, D_LRU), row),
            pl.BlockSpec((tm, D_ATTN), row),
            pl.BlockSpec((1, D_ATTN), const),
            pl.BlockSpec((D_LRU + D_ATTN, D_MODEL), const, pipeline_mode=once),
            pl.BlockSpec((1, D_MODEL), const),
            pl.BlockSpec((D_MODEL, D_FF), const, pipeline_mode=once),
            pl.BlockSpec((D_MODEL, D_FF), const, pipeline_mode=once),
            pl.BlockSpec((D_FF, D_MODEL), const, pipeline_mode=once),
            pl.BlockSpec((1, D_MODEL), const),
        ],
        out_specs=pl.BlockSpec((tm, D_MODEL), row),
        out_shape=jax.ShapeDtypeStruct((n, D_MODEL), F32),
        compiler_params=pltpu.CompilerParams(
            dimension_semantics=("arbitrary",), vmem_limit_bytes=VMEM_LIMIT_BYTES),
        name="mix_ffn",
    )(x2d, yl, ya, pw["g_attn"], pw["w_out"], pw["g_ffn"], pw["w_gate"], pw["w_up"], pw["w_down"],
      g_final)


def _block_diag_gate_weights(w_a, w_x):
    half_blocks = N_LRU_BLOCKS // 2
    half = D_LRU // 2

    def bd(w):
        eye = jnp.eye(half_blocks, dtype=w.dtype)
        return jnp.einsum("nij,nm->nimj", w, eye).reshape(half, half)

    halves = [jnp.concatenate([bd(w_a[c * half_blocks:(c + 1) * half_blocks]),
                               bd(w_x[c * half_blocks:(c + 1) * half_blocks])], axis=1)
              for c in range(2)]
    return jnp.stack(halves, axis=0).astype(BF16)


def _row(v):
    return v.reshape(1, -1).astype(F32)


def kernel(x_prompt, x_sample, state_conv, state_lru, cache_k, cache_v, norm_mix, w_in, conv_w, conv_b,
           lru_w_a, lru_b_a, lru_w_x, lru_b_x, lru_lambda, out_norm_lru, out_norm_attn, w_out, norm_ffn,
           w_gate, w_up, w_down, norm_final):
    depth = w_in.shape[0]
    bp, sp, _ = x_prompt.shape
    bs, ts, _ = x_sample.shape
    w_p = min(MAX_WINDOW, sp)
    assert ts >= CONV_WIDTH - 1
    xp = x_prompt.reshape(bp * sp, D_MODEL)
    xs = x_sample.reshape(bs * ts, D_MODEL)
    g_final = _row(norm_final)
    conv_p, lru_p, kw_p, vw_p = [], [], [], []
    conv_s, lru_s, kn_s, vn_s = [], [], [], []
    for l in range(depth):
        last = l == depth - 1
        w_in_bf = w_in[l].astype(BF16)
        g_mix = _row(norm_mix[l])
        lw = dict(
            conv_w=conv_w[l].astype(F32), conv_b=_row(conv_b[l]),
            w_gates=_block_diag_gate_weights(lru_w_a[l], lru_w_x[l]),
            b_a=_row(lru_b_a[l]), b_x=_row(lru_b_x[l]),
            neg_c_softplus=_row(-LRU_C * jax.nn.softplus(-lru_lambda[l].astype(F32))),
            g_lru=_row(out_norm_lru[l]))
        pw = dict(
            g_attn=_row(out_norm_attn[l]), w_out=w_out[l].astype(BF16), g_ffn=_row(norm_ffn[l]),
            w_gate=w_gate[l].astype(BF16), w_up=w_up[l].astype(BF16), w_down=w_down[l].astype(BF16))

        u, gate, q, k, v, k_win, v_win = _inproj_window(xp, g_mix, w_in_bf, tm=1024, seq_len=sp, window=w_p)
        shp = lambda a: a.reshape(bp, sp, -1)
        yl, c_new, h_last = _lru_prompt(
            shp(u), shp(gate), jnp.zeros((bp, CONV_WIDTH - 1, D_LRU), F32),
            jnp.zeros((bp, 1, D_LRU), F32), lw, tt=2048)
        ya = _attn_prompt(shp(q), shp(k), shp(v))
        xp = _mix_ffn(xp, yl.reshape(bp * sp, D_LRU), ya.reshape(bp * sp, D_ATTN), pw, g_final,
                      tm=1024, final_norm=last)
        conv_p.append(c_new)
        lru_p.append(h_last.reshape(bp, D_LRU))
        win = lambda a: a.reshape(bp, N_HEADS, HEAD_DIM, w_p).transpose(0, 3, 1, 2)
        kw_p.append(win(k_win))
        vw_p.append(win(v_win))

        u, gate, q, k, v = _inproj(xs, g_mix, w_in_bf, tm=bs * ts)
        tmaj = lambda a: a.reshape(bs, ts, -1).transpose(1, 0, 2)
        yl_t, c_new_t, h_last = _lru_sample(
            tmaj(u), tmaj(gate), state_conv[l].astype(F32).transpose(1, 0, 2),
            state_lru[l].astype(F32), lw)
        seq = lambda a: a.reshape(bs, ts, D_ATTN)
        ya = _attn_sample(seq(q), seq(k), seq(v), cache_k[l].astype(F32), cache_v[l].astype(F32))
        xs = _mix_ffn(xs, yl_t.transpose(1, 0, 2).reshape(bs * ts, D_LRU), ya.reshape(bs * ts, D_ATTN),
                      pw, g_final, tm=bs * ts, final_norm=last)
        conv_s.append(c_new_t.transpose(1, 0, 2))
        lru_s.append(h_last)
        kn_s.append(k.reshape(bs, ts, N_HEADS, HEAD_DIM))
        vn_s.append(v.reshape(bs, ts, N_HEADS, HEAD_DIM))
    return (xp.reshape(bp, sp, D_MODEL), xs.reshape(bs, ts, D_MODEL),
            jnp.stack(conv_p, 0), jnp.stack(lru_p, 0), jnp.stack(kw_p, 0), jnp.stack(vw_p, 0),
            jnp.stack(conv_s, 0), jnp.stack(lru_s, 0), jnp.stack(kn_s, 0), jnp.stack(vn_s, 0))
```

```python
import functools
import math

import jax
import jax.numpy as jnp
from jax import lax
from jax.experimental import pallas as pl
from jax.experimental.pallas import tpu as pltpu

F32 = jnp.float32
BF16 = jnp.bfloat16

D_MODEL = 1024
D_LRU = 512
D_ATTN = 512
N_HEADS = 8
HEAD_DIM = 64
N_LRU_BLOCKS = 8
LRU_BLOCK = D_LRU // N_LRU_BLOCKS
CONV_WIDTH = 4
LRU_C = 8.0
DILATED_PATTERNS = ((128, 1), (512, 4), (2048, 16))
MAX_WINDOW = 2048
D_FF = 2816
D_IN = 2 * D_LRU + 3 * D_ATTN
RMS_EPS = 1e-6
Q_SCALE = HEAD_DIM ** -0.5

LANES = 128
SUBLANES = 8
VMEM_LIMIT_BYTES = 56 * 1024 * 1024

KEYS_PER_PATTERN = 129
QBLK = 128
SUPER = 2048
HEADS_PER_STEP = LANES // HEAD_DIM
NEG_BIG = -1e30
LOG2_E = math.log2(math.e)
MERGE_DIL = 4
LRU_CHUNK = 1024
SEG_LEN = LRU_CHUNK // SUBLANES
FF_CHUNK = 256
assert D_FF % FF_CHUNK == 0
HEAD_SHIFT = HEAD_DIM.bit_length() - 1
SUBLANE_SHIFT = SUBLANES.bit_length() - 1
assert all(d & (d - 1) == 0 for _, d in DILATED_PATTERNS) and 1 << HEAD_SHIFT == HEAD_DIM
assert all(w // d + 1 == KEYS_PER_PATTERN for w, d in DILATED_PATTERNS)


def _rms_scale(x):
    return lax.rsqrt(jnp.mean(x * x, axis=-1, keepdims=True) + RMS_EPS)


def _gelu_tanh(x):
    c = math.sqrt(2.0 / math.pi)
    t = jnp.tanh(x * (c + (0.044715 * c) * (x * x)))
    half = 0.5 * x
    return half + half * t


def _normed(x_ref, g_ref):
    x = x_ref[...]
    return (x * _rms_scale(x) * g_ref[...]).astype(BF16)


def _project(xn, w_ref, first_col, width=D_LRU):
    return jnp.dot(xn, w_ref[:, first_col:first_col + width], preferred_element_type=F32)


def _project_all(x_ref, g_ref, w_ref, u_ref, gate_ref, q_ref, k_ref, v_ref):
    xn = _normed(x_ref, g_ref)
    u_ref[...] = _project(xn, w_ref, 0)
    gate_ref[...] = _project(xn, w_ref, D_LRU)
    o = 2 * D_LRU
    q_ref[...] = _project(xn, w_ref, o, D_ATTN) * Q_SCALE
    k = _project(xn, w_ref, o + D_ATTN, D_ATTN)
    v = _project(xn, w_ref, o + 2 * D_ATTN, D_ATTN)
    k_ref[...] = k
    v_ref[...] = v
    return k, v


def _inproj_kernel(x_ref, g_ref, w_ref, u_ref, gate_ref, q_ref, k_ref, v_ref):
    _project_all(x_ref, g_ref, w_ref, u_ref, gate_ref, q_ref, k_ref, v_ref)


def _inproj_window_kernel(x_ref, g_ref, w_ref, u_ref, gate_ref, q_ref, k_ref, v_ref, kw_ref, vw_ref):
    k, v = _project_all(x_ref, g_ref, w_ref, u_ref, gate_ref, q_ref, k_ref, v_ref)
    kw_ref[0] = k.T
    vw_ref[0] = v.T


def _inproj(x2d, g, w_bf, tm):
    n = x2d.shape[0]
    assert n % tm == 0
    const = lambda i: (0, 0)
    row = lambda i: (i, 0)
    return pl.pallas_call(
        _inproj_kernel,
        grid=(n // tm,),
        in_specs=[
            pl.BlockSpec((tm, D_MODEL), row),
            pl.BlockSpec((1, D_MODEL), const),
            pl.BlockSpec((D_MODEL, D_IN), const, pipeline_mode=pl.Buffered(1)),
        ],
        out_specs=[pl.BlockSpec((tm, D_LRU), row)] * 5,
        out_shape=[jax.ShapeDtypeStruct((n, D_LRU), F32)] * 5,
        compiler_params=pltpu.CompilerParams(
            dimension_semantics=("arbitrary",), vmem_limit_bytes=VMEM_LIMIT_BYTES),
        name="inproj",
    )(x2d, g, w_bf)


def _inproj_window(x2d, g, w_bf, tm, seq_len, window):
    n = x2d.shape[0]
    assert n % seq_len == 0 and seq_len % tm == 0 and window % tm == 0
    per_seq, per_win, n_seq = seq_len // tm, window // tm, n // seq_len
    const = lambda i: (0, 0)
    row = lambda i: (i, 0)
    win_block = lambda i: (i // per_seq, 0, jnp.maximum(i % per_seq - (per_seq - per_win), 0))
    return pl.pallas_call(
        _inproj_window_kernel,
        grid=(n // tm,),
        in_specs=[
            pl.BlockSpec((tm, D_MODEL), row),
            pl.BlockSpec((1, D_MODEL), const),
            pl.BlockSpec((D_MODEL, D_IN), const, pipeline_mode=pl.Buffered(1)),
        ],
        out_specs=[pl.BlockSpec((tm, D_LRU), row)] * 5 + [pl.BlockSpec((1, D_ATTN, tm), win_block)] * 2,
        out_shape=[jax.ShapeDtypeStruct((n, D_LRU), F32)] * 5
        + [jax.ShapeDtypeStruct((n_seq, D_ATTN, window), F32)] * 2,
        compiler_params=pltpu.CompilerParams(
            dimension_semantics=("arbitrary",), vmem_limit_bytes=VMEM_LIMIT_BYTES),
        name="inproj_window",
    )(x2d, g, w_bf)


def _sigmoid(x):
    return jax.nn.sigmoid(x)


def _lru_gates(uc, wg_ref, ba, bx_bias, neg_c_softplus):
    half = D_LRU // 2
    ucb = uc.astype(BF16)
    pre0 = jnp.dot(ucb[:, 0:half], wg_ref[0], preferred_element_type=F32)
    pre1 = jnp.dot(ucb[:, half:D_LRU], wg_ref[1], preferred_element_type=F32)
    r_pre = jnp.concatenate([pre0[:, 0:half], pre1[:, 0:half]], axis=1) + ba
    i_pre = jnp.concatenate([pre0[:, half:], pre1[:, half:]], axis=1) + bx_bias
    r = 1.0 / (1.0 + jnp.exp2(r_pre))
    i = 1.0 / (1.0 + jnp.exp2(i_pre))
    log_a = r * neg_c_softplus
    a = jnp.exp(log_a)
    z = -jnp.tanh(log_a) * (a * a + 1.0)
    root = jnp.where(z > 0.0, z * lax.rsqrt(z), 0.0)
    b = root * (i * uc)
    return a, b


def _shift_rows(x, s, fill):
    t = x.shape[0]
    if s % SUBLANES == 0:
        return jnp.concatenate([jnp.full((s, x.shape[1]), fill, x.dtype), x[0:t - s]], axis=0)
    rolled = pltpu.roll(x, s, axis=0)
    row = lax.broadcasted_iota(jnp.int32, x.shape, 0)
    return jnp.where(row < s, fill, rolled)


def _scan_rows(a, b):
    t = a.shape[0]
    s = 1
    while s < t:
        b = a * _shift_rows(b, s, 0.0) + b
        if 2 * s < t:
            a = a * _shift_rows(a, s, 1.0)
        s *= 2
    return b


def _lru_prompt_kernel(u_ref, gate_ref, cprev_ref, h0_ref, cw_ref, cb_ref, wg_ref, ba_ref, bx_ref,
                       ncs_ref, gl_ref, y_ref, conv_ref, hlast_ref, tail_ref, h_ref, ubuf_ref, gbuf_ref,
                       ybuf_ref, *, n_chunks):
    ti = pl.program_id(1)
    lane_chunks = D_LRU // LANES

    def to_segment_major(src_ref, r0, buf_ref):
        for s in range(SUBLANES):
            rows = pl.ds(pl.multiple_of(r0 + s * SEG_LEN, SEG_LEN), SEG_LEN)
            for c in range(lane_chunks):
                buf_ref[c, pl.ds(s, SEG_LEN, stride=SUBLANES), :] = src_ref[0, rows, c * LANES:(c + 1) * LANES]
        return jnp.concatenate([buf_ref[c] for c in range(lane_chunks)], axis=1)
    n_t = pl.num_programs(1)

    @pl.when(ti == 0)
    def _():
        tail_ref[...] = jnp.zeros_like(tail_ref)
        tail_ref[SUBLANES - (CONV_WIDTH - 1):SUBLANES, :] = cprev_ref[0]
        h_ref[...] = jnp.broadcast_to(h0_ref[0], h_ref.shape)

    cw = cw_ref[...]
    cb = cb_ref[...]
    ba = ba_ref[...]
    bxb = bx_ref[...]
    ncs = ncs_ref[...]
    gl = gl_ref[...]
    row8 = lax.broadcasted_iota(jnp.int32, (SUBLANES, D_LRU), 0)
    seg0 = row8 == 0
    n_prev = CONV_WIDTH - 1

    def vrow(x, tau):
        return x[SUBLANES * tau:SUBLANES * (tau + 1)]

    def chunk(ci, carry):
        r0 = pl.multiple_of(ci * LRU_CHUNK, LRU_CHUNK)
        u = to_segment_major(u_ref, r0, ubuf_ref)
        tail = tail_ref[...]
        pre = []
        for k in range(n_prev, 0, -1):
            from_prev_seg = pltpu.roll(vrow(u, SEG_LEN - k), 1, axis=0)
            pre.append(jnp.where(seg0, tail[SUBLANES - k:SUBLANES - k + 1], from_prev_seg))
            tail_ref[SUBLANES - k:SUBLANES - k + 1, :] = u[SUBLANES * (SEG_LEN - k + 1) - 1:
                                                           SUBLANES * (SEG_LEN - k + 1)]
        ext = jnp.concatenate(pre + [u], axis=0)
        uc = cb
        for j in range(CONV_WIDTH):
            uc = uc + cw[j:j + 1, :] * ext[SUBLANES * j:SUBLANES * j + LRU_CHUNK]
        a, b = _lru_gates(uc, wg_ref, ba, bxb, ncs)

        h = vrow(b, 0)
        p = vrow(a, 0)
        for tau in range(1, SEG_LEN):
            a_t = vrow(a, tau)
            h = a_t * h + vrow(b, tau)
            p = a_t * p
        h_in = h_ref[0:1, :]
        seg_end = _scan_rows(p, jnp.where(seg0, h + p * h_in, h))
        h = jnp.where(seg0, h_in, pltpu.roll(seg_end, 1, axis=0))
        h_ref[...] = jnp.broadcast_to(seg_end[SUBLANES - 1:SUBLANES], h_ref.shape)
        hs = []
        for tau in range(SEG_LEN):
            h = vrow(a, tau) * h + vrow(b, tau)
            hs.append(h)
        h_all = jnp.concatenate(hs, axis=0)

        y = h_all * _gelu_tanh(to_segment_major(gate_ref, r0, gbuf_ref))
        yn = y * _rms_scale(y) * gl
        for c in range(lane_chunks):
            ybuf_ref[c] = yn[:, c * LANES:(c + 1) * LANES]
        for s in range(SUBLANES):
            rows = jnp.concatenate([ybuf_ref[c, pl.ds(s, SEG_LEN, stride=SUBLANES), :]
                                    for c in range(lane_chunks)], axis=1)
            y_ref[0, pl.ds(pl.multiple_of(r0 + s * SEG_LEN, SEG_LEN), SEG_LEN), :] = rows.astype(BF16)
        return carry

    lax.fori_loop(0, n_chunks, chunk, 0)

    @pl.when(ti == n_t - 1)
    def _():
        conv_ref[0] = tail_ref[SUBLANES - (CONV_WIDTH - 1):SUBLANES, :]
        hlast_ref[0] = h_ref[0:1, :]


def _lru_prompt(u, gate, conv_prev, h0, lw, tt):
    b, s, _ = u.shape
    assert s % tt == 0 and tt % LRU_CHUNK == 0
    seq = lambda bi, ti: (bi, ti, 0)
    per_b = lambda bi, ti: (bi, 0, 0)
    c2 = lambda bi, ti: (0, 0)
    c3 = lambda bi, ti: (0, 0, 0)
    vec = pl.BlockSpec((1, D_LRU), c2)
    return pl.pallas_call(
        functools.partial(_lru_prompt_kernel, n_chunks=tt // LRU_CHUNK),
        grid=(b, s // tt),
        in_specs=[
            pl.BlockSpec((1, tt, D_LRU), seq),
            pl.BlockSpec((1, tt, D_LRU), seq),
            pl.BlockSpec((1, CONV_WIDTH - 1, D_LRU), per_b),
            pl.BlockSpec((1, 1, D_LRU), per_b),
            pl.BlockSpec((CONV_WIDTH, D_LRU), c2),
            vec,
            pl.BlockSpec((2, D_LRU // 2, D_LRU), c3),
            vec, vec, vec, vec,
        ],
        out_specs=[
            pl.BlockSpec((1, tt, D_LRU), seq),
            pl.BlockSpec((1, CONV_WIDTH - 1, D_LRU), per_b),
            pl.BlockSpec((1, 1, D_LRU), per_b),
        ],
        out_shape=[
            jax.ShapeDtypeStruct((b, s, D_LRU), BF16),
            jax.ShapeDtypeStruct((b, CONV_WIDTH - 1, D_LRU), F32),
            jax.ShapeDtypeStruct((b, 1, D_LRU), F32),
        ],
        scratch_shapes=[pltpu.VMEM((SUBLANES, D_LRU), F32), pltpu.VMEM((SUBLANES, D_LRU), F32)]
        + [pltpu.VMEM((D_LRU // LANES, LRU_CHUNK, LANES), F32)] * 3,
        compiler_params=pltpu.CompilerParams(
            dimension_semantics=("arbitrary", "arbitrary"), vmem_limit_bytes=VMEM_LIMIT_BYTES),
        name="lru_prompt",
    )(u, gate, conv_prev, h0, lw["conv_w"], lw["conv_b"], lw["w_gates"], lw["b_a"], lw["b_x"],
      lw["neg_c_softplus"], lw["g_lru"])


def _lru_sample_kernel(u_ref, gate_ref, cprev_ref, h0_ref, cw_ref, cb_ref, wg_ref, ba_ref, bx_ref,
                       ncs_ref, gl_ref, y_ref, conv_ref, hlast_ref):
    t_len, nb, _ = u_ref.shape
    cw = cw_ref[...]
    hist = [cprev_ref[j] for j in range(CONV_WIDTH - 1)] + [u_ref[t] for t in range(t_len)]
    ucs = []
    for t in range(t_len):
        uc = cb_ref[...]
        for j in range(CONV_WIDTH):
            uc = uc + cw[j:j + 1, :] * hist[t + j]
        ucs.append(uc)
    uc_all = jnp.concatenate(ucs, axis=0)
    a, b = _lru_gates(uc_all, wg_ref, ba_ref[...], bx_ref[...], ncs_ref[...])
    h = h0_ref[...]
    gl = gl_ref[...]
    for t in range(t_len):
        h = a[t * nb:(t + 1) * nb] * h + b[t * nb:(t + 1) * nb]
        y = h * _gelu_tanh(gate_ref[t])
        y_ref[t] = (y * _rms_scale(y) * gl).astype(BF16)
    hlast_ref[...] = h
    for j in range(CONV_WIDTH - 1):
        conv_ref[j] = hist[t_len + j]


def _lru_sample(u_t, gate_t, conv_prev_t, h0, lw):
    t_len, nb, _ = u_t.shape
    return pl.pallas_call(
        _lru_sample_kernel,
        out_shape=[
            jax.ShapeDtypeStruct((t_len, nb, D_LRU), BF16),
            jax.ShapeDtypeStruct((CONV_WIDTH - 1, nb, D_LRU), F32),
            jax.ShapeDtypeStruct((nb, D_LRU), F32),
        ],
        name="lru_sample",
    )(u_t, gate_t, conv_prev_t, h0, lw["conv_w"], lw["conv_b"], lw["w_gates"], lw["b_a"], lw["b_x"],
      lw["neg_c_softplus"], lw["g_lru"])


def _attn_prompt_kernel(q_ref, k_ref, v_ref, o_ref,
                        qp1, qp4, qp16, kb1, kb4, kb16, vb1, vb4, vb16, res_a, res_l, res_m, bias_ref,
                        stage_ref):
    sb = pl.program_id(2)
    qps = (qp1, qp4, qp16)
    kbs = (kb1, kb4, kb16)
    vbs = (vb1, vb4, vb16)
    dils = tuple(d for _, d in DILATED_PATTERNS)
    assert dils == (1, 4, 16)
    lane = lax.broadcasted_iota(jnp.int32, (QBLK, LANES), 1)
    head0 = lane < HEAD_DIM

    qi = lax.broadcasted_iota(jnp.int32, (QBLK, 2 * QBLK), 0)
    ki = lax.broadcasted_iota(jnp.int32, (QBLK, 2 * QBLK), 1)
    delta = qi + QBLK - ki
    band = (delta >= 0) & (delta <= KEYS_PER_PATTERN - 1)
    bias_ref[0] = jnp.where(band, 0.0, NEG_BIG)
    bias_ref[1] = jnp.where(band & (ki >= QBLK), 0.0, NEG_BIG)

    for kb, vb, d in zip(kbs, vbs, dils):
        cls_len = SUPER // d

        @pl.when(sb == 0)
        def _():
            kb[:, 0:QBLK, :] = jnp.zeros((d, QBLK, LANES), BF16)
            vb[:, 0:QBLK, 0:LANES] = jnp.zeros((d, QBLK, LANES), BF16)
            vb[:, :, LANES:2 * LANES] = jnp.ones((d, QBLK + cls_len, LANES), BF16)

        @pl.when(sb > 0)
        def _():
            kb[:, 0:QBLK, :] = kb[:, cls_len:cls_len + QBLK, :]
            vb[:, 0:QBLK, 0:LANES] = vb[:, cls_len:cls_len + QBLK, 0:LANES]

    def put(x, p, c, which):
        n_rows = x.shape[0]
        if which == "q":
            qps[p][c] = (x * LOG2_E).astype(BF16)
        elif which == "k":
            kbs[p][c, QBLK:QBLK + n_rows, :] = x.astype(BF16)
        else:
            vbs[p][c, QBLK:QBLK + n_rows, 0:LANES] = x.astype(BF16)

    def tile(p, c, i):
        d = dils[p]
        r0 = i * QBLK
        q_t = qps[p][c, pl.ds(r0, QBLK), :]
        k_t = kbs[p][c, pl.ds(r0, 2 * QBLK), :]
        bias = bias_ref[(sb == 0).astype(jnp.int32)] if i == 0 else bias_ref[0]
        probs, maxes = [], []
        for h in range(HEADS_PER_STEP):
            hm = head0 if h == 0 else jnp.logical_not(head0)
            qh = jnp.where(hm, q_t, jnp.zeros_like(q_t))
            s = lax.dot_general(qh, k_t, (((1,), (1,)), ((), ())), preferred_element_type=F32)
            s = s + bias
            m = jnp.max(s, axis=-1, keepdims=True)
            probs.append(jnp.exp2(s - m).astype(BF16))
            maxes.append(jnp.broadcast_to(m, (QBLK, LANES)))
        v_t = vbs[p][c, pl.ds(r0, 2 * QBLK), :]
        pv = jnp.dot(jnp.concatenate(probs, axis=0), v_t, preferred_element_type=F32)
        pv0, pv1 = pv[0:QBLK], pv[QBLK:2 * QBLK]
        if d == 1:
            dst = pl.ds(r0, QBLK)
        elif d == MERGE_DIL:
            dst = pl.ds(c * (SUPER // MERGE_DIL) + r0, QBLK)
        else:
            dst = pl.ds((c % MERGE_DIL) * (SUPER // MERGE_DIL) + c // MERGE_DIL, QBLK, stride=MERGE_DIL)
        res_a[p, dst, :] = jnp.where(head0, pv0[:, 0:LANES], pv1[:, 0:LANES])
        res_l[p, dst, :] = jnp.where(head0, pv0[:, LANES:2 * LANES], pv1[:, LANES:2 * LANES])
        res_m[p, dst, :] = jnp.where(head0, maxes[0], maxes[1])

    sources = (("q", q_ref), ("k", k_ref), ("v", v_ref))
    for which, ref in sources:
        put(ref[0], 0, 0, which)
    for i in range(SUPER // QBLK):
        tile(0, 0, i)
    for idx, (which, ref) in enumerate(sources):
        for c in range(4):
            x = ref[0, pl.ds(c, SUPER // 4, stride=4), :]
            stage_ref[idx, c] = x
            put(x, 1, c, which)
        for r in range(16):
            put(stage_ref[idx, r % 4, pl.ds(r // 4, QBLK, stride=4), :], 2, r, which)

    merge_rows = 256
    for c in range(MERGE_DIL):
        for i in range(SUPER // MERGE_DIL // QBLK):
            tile(1, c, i)
        for a in range(dils[2] // MERGE_DIL):
            tile(2, a * MERGE_DIL + c, 0)
        for n0 in range(0, SUPER // MERGE_DIL, merge_rows):
            in_time = pl.ds(MERGE_DIL * n0 + c, merge_rows, stride=MERGE_DIL)
            in_class = pl.ds(c * (SUPER // MERGE_DIL) + n0, merge_rows)
            rows = (in_time, in_class, in_class)
            ms = [res_m[p, rows[p], :] for p in range(3)]
            m = jnp.maximum(jnp.maximum(ms[0], ms[1]), ms[2])
            ws = [jnp.exp2(x - m) for x in ms]
            den = ws[0] * res_l[0, rows[0], :] + ws[1] * res_l[1, rows[1], :] + ws[2] * res_l[2, rows[2], :]
            num = ws[0] * res_a[0, rows[0], :] + ws[1] * res_a[1, rows[1], :] + ws[2] * res_a[2, rows[2], :]
            o_ref[0, in_time, :] = num * (1.0 / den)


def _attn_prompt(q, k, v):
    b, s, _ = q.shape
    assert s % SUPER == 0
    blk = pl.BlockSpec((1, SUPER, LANES), lambda bi, hp, sb: (bi, sb, hp))
    scratch = []
    for _, d in DILATED_PATTERNS:
        scratch.append(pltpu.VMEM((d, SUPER // d, LANES), BF16))
    for _, d in DILATED_PATTERNS:
        scratch.append(pltpu.VMEM((d, QBLK + SUPER // d, LANES), BF16))
    for _, d in DILATED_PATTERNS:
        scratch.append(pltpu.VMEM((d, QBLK + SUPER // d, 2 * LANES), BF16))
    scratch += [pltpu.VMEM((3, SUPER, LANES), F32)] * 3
    scratch += [pltpu.VMEM((2, QBLK, 2 * QBLK), F32), pltpu.VMEM((3, 4, SUPER // 4, LANES), F32)]
    return pl.pallas_call(
        _attn_prompt_kernel,
        grid=(b, D_ATTN // LANES, s // SUPER),
        in_specs=[blk, blk, blk],
        out_specs=blk,
        out_shape=jax.ShapeDtypeStruct((b, s, D_ATTN), F32),
        scratch_shapes=scratch,
        compiler_params=pltpu.CompilerParams(
            dimension_semantics=("arbitrary", "arbitrary", "arbitrary"),
            vmem_limit_bytes=VMEM_LIMIT_BYTES),
        name="attn_prompt",
    )(q, k, v)


def _attn_sample_kernel(q_ref, kn_ref, vn_ref, kt_ref, vt_ref, o_ref, *, t_len, w_buf):
    rows = N_HEADS * SUBLANES
    n_pat = len(DILATED_PATTERNS)
    pad = jnp.zeros((SUBLANES - t_len, D_ATTN), F32)
    q8 = jnp.concatenate([q_ref[0] * LOG2_E, pad], axis=0)
    kn8 = jnp.concatenate([kn_ref[0], pad], axis=0).astype(BF16)
    vn8 = jnp.concatenate([vn_ref[0], pad], axis=0).astype(BF16)
    head_lanes = lambda h: slice(h * HEAD_DIM, (h + 1) * HEAD_DIM)

    s_c = jnp.concatenate(
        [jnp.dot(q8[:, head_lanes(h)].astype(BF16), kt_ref[0, h].astype(BF16), preferred_element_type=F32)
         for h in range(N_HEADS)], axis=0)
    rid = lax.broadcasted_iota(jnp.int32, (rows, D_ATTN), 0)
    lid = lax.broadcasted_iota(jnp.int32, (rows, D_ATTN), 1)
    qbd = jnp.where((rid >> SUBLANE_SHIFT) == (lid >> HEAD_SHIFT), jnp.concatenate([q8] * N_HEADS, axis=0), 0.0)
    s_n = lax.dot_general(qbd.astype(BF16), kn8, (((1,), (1,)), ((), ())), preferred_element_type=F32)

    t_c = lax.broadcasted_iota(jnp.int32, (rows, w_buf), 0) & (SUBLANES - 1)
    d_c = w_buf + t_c - lax.broadcasted_iota(jnp.int32, (rows, w_buf), 1)
    t_n = lax.broadcasted_iota(jnp.int32, (rows, SUBLANES), 0) & (SUBLANES - 1)
    j_n = lax.broadcasted_iota(jnp.int32, (rows, SUBLANES), 1)
    d_n = t_n - j_n
    new_ok = (d_n >= 0) & (j_n < t_len)

    p_cs, p_ns, lses, ls = [], [], [], []
    for window, dil in DILATED_PATTERNS:
        ok_c = ((d_c & (dil - 1)) == 0) & (d_c <= window)
        ok_n = new_ok & ((d_n & (dil - 1)) == 0) & (d_n <= window)
        sc = jnp.where(ok_c, s_c, NEG_BIG)
        sn = jnp.where(ok_n, s_n, NEG_BIG)
        m = jnp.maximum(jnp.max(sc, axis=-1, keepdims=True), jnp.max(sn, axis=-1, keepdims=True))
        pc = jnp.exp2(sc - m)
        pn = jnp.exp2(sn - m)
        l = jnp.sum(pc, axis=-1, keepdims=True) + jnp.sum(pn, axis=-1, keepdims=True)
        p_cs.append(pc)
        p_ns.append(pn)
        ls.append(l)
        lses.append(m + jnp.log2(l))
    m = jnp.maximum(jnp.maximum(lses[0], lses[1]), lses[2])
    es = [jnp.exp2(x - m) for x in lses]
    inv_den = 1.0 / (es[0] + es[1] + es[2])
    wts = [es[p] * inv_den * (1.0 / ls[p]) for p in range(n_pat)]

    o_new = jnp.dot(jnp.concatenate(p_ns, axis=0).astype(BF16), vn8, preferred_element_type=F32)
    for h in range(N_HEADS):
        rs = slice(h * SUBLANES, (h + 1) * SUBLANES)
        p_h = jnp.concatenate([p_cs[p][rs] for p in range(n_pat)], axis=0).astype(BF16)
        o_h = lax.dot_general(p_h, vt_ref[0, h].astype(BF16), (((1,), (1,)), ((), ())),
                              preferred_element_type=F32)
        out = None
        for p in range(n_pat):
            new_part = o_new[p * rows + h * SUBLANES:p * rows + (h + 1) * SUBLANES, head_lanes(h)]
            term = wts[p][rs] * (o_h[p * SUBLANES:(p + 1) * SUBLANES] + new_part)
            out = term if out is None else out + term
        o_ref[0, :, head_lanes(h)] = out[0:t_len]


def _attn_sample(q, k_new, v_new, cache_k, cache_v):
    b, t_len, _ = q.shape
    w_buf = cache_k.shape[1]
    assert w_buf >= MAX_WINDOW and t_len <= SUBLANES
    new = pl.BlockSpec((1, t_len, D_ATTN), lambda bi: (bi, 0, 0))
    cache = pl.BlockSpec((1, N_HEADS, HEAD_DIM, w_buf), lambda bi: (bi, 0, 0, 0))
    row_minor = lambda c: jnp.transpose(c, (0, 2, 3, 1))
    return pl.pallas_call(
        functools.partial(_attn_sample_kernel, t_len=t_len, w_buf=w_buf),
        grid=(b,),
        in_specs=[new, new, new, cache, cache],
        out_specs=new,
        out_shape=jax.ShapeDtypeStruct((b, t_len, D_ATTN), F32),
        compiler_params=pltpu.CompilerParams(
            dimension_semantics=("arbitrary",), vmem_limit_bytes=VMEM_LIMIT_BYTES),
        name="attn_sample",
    )(q, k_new, v_new, row_minor(cache_k), row_minor(cache_v))


def _mix_ffn_kernel(x_ref, yl_ref, ya_ref, ga_ref, wo_ref, gf_ref, wg_ref, wu_ref, wd_ref, gn_ref,
                    y_ref, *, final_norm):
    ya = ya_ref[...]
    ya_n = (ya * _rms_scale(ya) * ga_ref[...]).astype(BF16)
    mix = (jnp.dot(yl_ref[...], wo_ref[0:D_LRU, :], preferred_element_type=F32)
           + jnp.dot(ya_n, wo_ref[D_LRU:D_LRU + D_ATTN, :], preferred_element_type=F32))
    x1 = x_ref[...] + mix
    xn = (x1 * _rms_scale(x1) * gf_ref[...]).astype(BF16)
    x2 = x1
    for c0 in range(0, D_FF, FF_CHUNK):
        cols = slice(c0, c0 + FF_CHUNK)
        hg = jnp.dot(xn, wg_ref[:, cols], preferred_element_type=F32)
        hu = jnp.dot(xn, wu_ref[:, cols], preferred_element_type=F32)
        act = (hg * _sigmoid(hg) * hu).astype(BF16)
        x2 = x2 + jnp.dot(act, wd_ref[cols, :], preferred_element_type=F32)
    if final_norm:
        x2 = x2 * _rms_scale(x2) * gn_ref[...]
    y_ref[...] = x2


def _mix_ffn(x2d, yl, ya, pw, g_final, tm, final_norm):
    n = x2d.shape[0]
    row = lambda i: (i, 0)
    const = lambda i: (0, 0)
    once = pl.Buffered(1)
    return pl.pallas_call(
        functools.partial(_mix_ffn_kernel, final_norm=final_norm),
        grid=(n // tm,),
        in_specs=[
            pl.BlockSpec((tm, D_MODEL), row),
            pl.BlockSpec((tm, D_LRU), row),
            pl.BlockSpec((tm, D_ATTN), row),
            pl.BlockSpec((1, D_ATTN), const),
            pl.BlockSpec((D_LRU + D_ATTN, D_MODEL), const, pipeline_mode=once),
            pl.BlockSpec((1, D_MODEL), const),
            pl.BlockSpec((D_MODEL, D_FF), const, pipeline_mode=once),
            pl.BlockSpec((D_MODEL, D_FF), const, pipeline_mode=once),
            pl.BlockSpec((D_FF, D_MODEL), const, pipeline_mode=once),
            pl.BlockSpec((1, D_MODEL), const),
        ],
        out_specs=pl.BlockSpec((tm, D_MODEL), row),
        out_shape=jax.ShapeDtypeStruct((n, D_MODEL), F32),
        compiler_params=pltpu.CompilerParams(
            dimension_semantics=("arbitrary",), vmem_limit_bytes=VMEM_LIMIT_BYTES),
        name="mix_ffn",
    )(x2d, yl, ya, pw["g_attn"], pw["w_out"], pw["g_ffn"], pw["w_gate"], pw["w_up"], pw["w_down"],
      g_final)


def _block_diag_gate_weights(w_a, w_x):
    half_blocks = N_LRU_BLOCKS // 2
    half = D_LRU // 2

    def bd(w):
        eye = jnp.eye(half_blocks, dtype=w.dtype)
        return jnp.einsum("nij,nm->nimj", w, eye).reshape(half, half)

    halves = [jnp.concatenate([bd(w_a[c * half_blocks:(c + 1) * half_blocks]),
                               bd(w_x[c * half_blocks:(c + 1) * half_blocks])], axis=1)
              for c in range(2)]
    return jnp.stack(halves, axis=0).astype(BF16)


def _row(v):
    return v.reshape(1, -1).astype(F32)


def kernel(x_prompt, x_sample, state_conv, state_lru, cache_k, cache_v, norm_mix, w_in, conv_w, conv_b,
           lru_w_a, lru_b_a, lru_w_x, lru_b_x, lru_lambda, out_norm_lru, out_norm_attn, w_out, norm_ffn,
           w_gate, w_up, w_down, norm_final):
    depth = w_in.shape[0]
    bp, sp, _ = x_prompt.shape
    bs, ts, _ = x_sample.shape
    w_p = min(MAX_WINDOW, sp)
    assert ts >= CONV_WIDTH - 1
    xp = x_prompt.reshape(bp * sp, D_MODEL)
    xs = x_sample.reshape(bs * ts, D_MODEL)
    g_final = _row(norm_final)
    conv_p, lru_p, kw_p, vw_p = [], [], [], []
    conv_s, lru_s, kn_s, vn_s = [], [], [], []
    for l in range(depth):
        last = l == depth - 1
        w_in_bf = w_in[l].astype(BF16)
        g_mix = _row(norm_mix[l])
        lw = dict(
            conv_w=conv_w[l].astype(F32), conv_b=_row(conv_b[l]),
            w_gates=_block_diag_gate_weights(lru_w_a[l].astype(F32) * -LOG2_E, lru_w_x[l].astype(F32) * -LOG2_E),
            b_a=_row(lru_b_a[l]) * -LOG2_E, b_x=_row(lru_b_x[l]) * -LOG2_E,
            neg_c_softplus=_row(-LRU_C * jax.nn.softplus(-lru_lambda[l].astype(F32))),
            g_lru=_row(out_norm_lru[l]))
        pw = dict(
            g_attn=_row(out_norm_attn[l]), w_out=w_out[l].astype(BF16), g_ffn=_row(norm_ffn[l]),
            w_gate=w_gate[l].astype(BF16), w_up=w_up[l].astype(BF16), w_down=w_down[l].astype(BF16))

        u, gate, q, k, v, k_win, v_win = _inproj_window(xp, g_mix, w_in_bf, tm=1024, seq_len=sp, window=w_p)
        shp = lambda a: a.reshape(bp, sp, -1)
        yl, c_new, h_last = _lru_prompt(
            shp(u), shp(gate), jnp.zeros((bp, CONV_WIDTH - 1, D_LRU), F32),
            jnp.zeros((bp, 1, D_LRU), F32), lw, tt=2048)
        ya = _attn_prompt(shp(q), shp(k), shp(v))
        xp = _mix_ffn(xp, yl.reshape(bp * sp, D_LRU), ya.reshape(bp * sp, D_ATTN), pw, g_final,
                      tm=1024, final_norm=last)
        conv_p.append(c_new)
        lru_p.append(h_last.reshape(bp, D_LRU))
        win = lambda a: a.reshape(bp, N_HEADS, HEAD_DIM, w_p).transpose(0, 3, 1, 2)
        kw_p.append(win(k_win))
        vw_p.append(win(v_win))

        u, gate, q, k, v = _inproj(xs, g_mix, w_in_bf, tm=bs * ts)
        tmaj = lambda a: a.reshape(bs, ts, -1).transpose(1, 0, 2)
        yl_t, c_new_t, h_last = _lru_sample(
            tmaj(u), tmaj(gate), state_conv[l].astype(F32).transpose(1, 0, 2),
            state_lru[l].astype(F32), lw)
        seq = lambda a: a.reshape(bs, ts, D_ATTN)
        ya = _attn_sample(seq(q), seq(k), seq(v), cache_k[l].astype(F32), cache_v[l].astype(F32))
        xs = _mix_ffn(xs, yl_t.transpose(1, 0, 2).reshape(bs * ts, D_LRU), ya.reshape(bs * ts, D_ATTN),
                      pw, g_final, tm=bs * ts, final_norm=last)
        conv_s.append(c_new_t.transpose(1, 0, 2))
        lru_s.append(h_last)
        kn_s.append(k.reshape(bs, ts, N_HEADS, HEAD_DIM))
        vn_s.append(v.reshape(bs, ts, N_HEADS, HEAD_DIM))
    return (xp.reshape(bp, sp, D_MODEL), xs.reshape(bs, ts, D_MODEL),
            jnp.stack(conv_p, 0), jnp.stack(lru_p, 0), jnp.stack(kw_p, 0), jnp.stack(vw_p, 0),
            jnp.stack(conv_s, 0), jnp.stack(lru_s, 0), jnp.stack(kn_s, 0), jnp.stack(vn_s, 0))
```

```python
import functools
import math

import jax
import jax.numpy as jnp
from jax import lax
from jax.experimental import pallas as pl
from jax.experimental.pallas import tpu as pltpu

F32 = jnp.float32
BF16 = jnp.bfloat16

D_MODEL = 1024
D_LRU = 512
D_ATTN = 512
N_HEADS = 8
HEAD_DIM = 64
N_LRU_BLOCKS = 8
LRU_BLOCK = D_LRU // N_LRU_BLOCKS
CONV_WIDTH = 4
LRU_C = 8.0
DILATED_PATTERNS = ((128, 1), (512, 4), (2048, 16))
MAX_WINDOW = 2048
D_FF = 2816
D_IN = 2 * D_LRU + 3 * D_ATTN
RMS_EPS = 1e-6
LOG2_E = math.log2(math.e)
Q_SCALE = HEAD_DIM ** -0.5 * LOG2_E

LANES = 128
SUBLANES = 8
VMEM_LIMIT_BYTES = 56 * 1024 * 1024

KEYS_PER_PATTERN = 129
QBLK = 128
SUPER = 2048
HEADS_PER_STEP = LANES // HEAD_DIM
NEG_BIG = -1e30
MERGE_DIL = 4
LRU_CHUNK = 2048
SEG_LEN = LRU_CHUNK // SUBLANES
FF_CHUNK = 256
assert D_FF % FF_CHUNK == 0
HEAD_SHIFT = HEAD_DIM.bit_length() - 1
SUBLANE_SHIFT = SUBLANES.bit_length() - 1
assert all(d & (d - 1) == 0 for _, d in DILATED_PATTERNS) and 1 << HEAD_SHIFT == HEAD_DIM
assert all(w // d + 1 == KEYS_PER_PATTERN for w, d in DILATED_PATTERNS)


def _rms_scale(x):
    return lax.rsqrt(jnp.mean(x * x, axis=-1, keepdims=True) + RMS_EPS)


def _normed(x_ref, g_ref):
    x = x_ref[...]
    return (x * _rms_scale(x) * g_ref[...]).astype(BF16)


def _project(xn, w_ref, first_col, width=D_LRU):
    return jnp.dot(xn, w_ref[:, first_col:first_col + width], preferred_element_type=F32)


def _project_all(x_ref, g_ref, w_ref, u_ref, gate_ref, q_ref, k_ref, v_ref):
    xn = _normed(x_ref, g_ref)
    u_ref[...] = _project(xn, w_ref, 0)
    gate_ref[...] = _project(xn, w_ref, D_LRU)
    o = 2 * D_LRU
    q_ref[...] = _project(xn, w_ref, o, D_ATTN) * Q_SCALE
    k = _project(xn, w_ref, o + D_ATTN, D_ATTN)
    v = _project(xn, w_ref, o + 2 * D_ATTN, D_ATTN)
    k_ref[...] = k
    v_ref[...] = v
    return k, v


def _inproj_kernel(x_ref, g_ref, w_ref, u_ref, gate_ref, q_ref, k_ref, v_ref):
    _project_all(x_ref, g_ref, w_ref, u_ref, gate_ref, q_ref, k_ref, v_ref)


def _inproj_window_kernel(x_ref, g_ref, w_ref, u_ref, gate_ref, q_ref, k_ref, v_ref, kw_ref, vw_ref):
    k, v = _project_all(x_ref, g_ref, w_ref, u_ref, gate_ref, q_ref, k_ref, v_ref)
    kw_ref[0] = k.T
    vw_ref[0] = v.T


def _inproj(x2d, g, w_bf, tm):
    n = x2d.shape[0]
    assert n % tm == 0
    const = lambda i: (0, 0)
    row = lambda i: (i, 0)
    return pl.pallas_call(
        _inproj_kernel,
        grid=(n // tm,),
        in_specs=[
            pl.BlockSpec((tm, D_MODEL), row),
            pl.BlockSpec((1, D_MODEL), const),
            pl.BlockSpec((D_MODEL, D_IN), const, pipeline_mode=pl.Buffered(1)),
        ],
        out_specs=[pl.BlockSpec((tm, D_LRU), row)] * 5,
        out_shape=[jax.ShapeDtypeStruct((n, D_LRU), F32)] * 5,
        compiler_params=pltpu.CompilerParams(
            dimension_semantics=("arbitrary",), vmem_limit_bytes=VMEM_LIMIT_BYTES),
        name="inproj",
    )(x2d, g, w_bf)


def _inproj_window(x2d, g, w_bf, tm, seq_len, window):
    n = x2d.shape[0]
    assert n % seq_len == 0 and seq_len % tm == 0 and window % tm == 0
    per_seq, per_win, n_seq = seq_len // tm, window // tm, n // seq_len
    const = lambda i: (0, 0)
    row = lambda i: (i, 0)
    win_block = lambda i: (i // per_seq, 0, jnp.maximum(i % per_seq - (per_seq - per_win), 0))
    return pl.pallas_call(
        _inproj_window_kernel,
        grid=(n // tm,),
        in_specs=[
            pl.BlockSpec((tm, D_MODEL), row),
            pl.BlockSpec((1, D_MODEL), const),
            pl.BlockSpec((D_MODEL, D_IN), const, pipeline_mode=pl.Buffered(1)),
        ],
        out_specs=[pl.BlockSpec((tm, D_LRU), row)] * 5 + [pl.BlockSpec((1, D_ATTN, tm), win_block)] * 2,
        out_shape=[jax.ShapeDtypeStruct((n, D_LRU), F32)] * 5
        + [jax.ShapeDtypeStruct((n_seq, D_ATTN, window), F32)] * 2,
        compiler_params=pltpu.CompilerParams(
            dimension_semantics=("arbitrary",), vmem_limit_bytes=VMEM_LIMIT_BYTES),
        name="inproj_window",
    )(x2d, g, w_bf)


def _gelu_tanh(x):
    c = math.sqrt(2.0 / math.pi)
    t = jnp.tanh(x * (c + (0.044715 * c) * (x * x)))
    half = 0.5 * x
    return half + half * t


def _sigmoid(x):
    return jax.nn.sigmoid(x)


def _lru_gates(uc, wg_ref, ba, bx_bias, neg_c_softplus):
    half = D_LRU // 2
    ucb = uc.astype(BF16)
    pre0 = jnp.dot(ucb[:, 0:half], wg_ref[0], preferred_element_type=F32)
    pre1 = jnp.dot(ucb[:, half:D_LRU], wg_ref[1], preferred_element_type=F32)
    r_pre = jnp.concatenate([pre0[:, 0:half], pre1[:, 0:half]], axis=1) + ba
    i_pre = jnp.concatenate([pre0[:, half:], pre1[:, half:]], axis=1) + bx_bias
    r = 1.0 / (1.0 + jnp.exp2(r_pre))
    i = 1.0 / (1.0 + jnp.exp2(i_pre))
    log_a = r * neg_c_softplus
    a = jnp.exp(log_a)
    z = -jnp.tanh(log_a) * (a * a + 1.0)
    root = jnp.where(z > 0.0, z * lax.rsqrt(z), 0.0)
    b = root * (i * uc)
    return a, b


def _shift_rows(x, s, fill):
    t = x.shape[0]
    if s % SUBLANES == 0:
        return jnp.concatenate([jnp.full((s, x.shape[1]), fill, x.dtype), x[0:t - s]], axis=0)
    rolled = pltpu.roll(x, s, axis=0)
    row = lax.broadcasted_iota(jnp.int32, x.shape, 0)
    return jnp.where(row < s, fill, rolled)


def _scan_rows(a, b):
    t = a.shape[0]
    s = 1
    while s < t:
        b = a * _shift_rows(b, s, 0.0) + b
        if 2 * s < t:
            a = a * _shift_rows(a, s, 1.0)
        s *= 2
    return b


def _lru_prompt_kernel(u_ref, gate_ref, cprev_ref, h0_ref, cw_ref, cb_ref, wg_ref, ba_ref, bx_ref,
                       ncs_ref, gl_ref, y_ref, conv_ref, hlast_ref, tail_ref, h_ref, ubuf_ref, gbuf_ref,
                       ybuf_ref, *, n_chunks):
    ti = pl.program_id(1)
    lane_chunks = D_LRU // LANES

    def to_segment_major(src_ref, r0, buf_ref):
        for s in range(SUBLANES):
            rows = pl.ds(pl.multiple_of(r0 + s * SEG_LEN, SEG_LEN), SEG_LEN)
            for c in range(lane_chunks):
                buf_ref[c, pl.ds(s, SEG_LEN, stride=SUBLANES), :] = src_ref[0, rows, c * LANES:(c + 1) * LANES]
        return jnp.concatenate([buf_ref[c] for c in range(lane_chunks)], axis=1)
    n_t = pl.num_programs(1)

    @pl.when(ti == 0)
    def _():
        tail_ref[...] = jnp.zeros_like(tail_ref)
        tail_ref[SUBLANES - (CONV_WIDTH - 1):SUBLANES, :] = cprev_ref[0]
        h_ref[...] = jnp.broadcast_to(h0_ref[0], h_ref.shape)

    cw = cw_ref[...]
    cb = cb_ref[...]
    ba = ba_ref[...]
    bxb = bx_ref[...]
    ncs = ncs_ref[...]
    gl = gl_ref[...]
    row8 = lax.broadcasted_iota(jnp.int32, (SUBLANES, D_LRU), 0)
    seg0 = row8 == 0
    n_prev = CONV_WIDTH - 1

    def vrow(x, tau):
        return x[SUBLANES * tau:SUBLANES * (tau + 1)]

    def chunk(ci, carry):
        r0 = pl.multiple_of(ci * LRU_CHUNK, LRU_CHUNK)
        u = to_segment_major(u_ref, r0, ubuf_ref)
        tail = tail_ref[...]
        pre = []
        for k in range(n_prev, 0, -1):
            from_prev_seg = pltpu.roll(vrow(u, SEG_LEN - k), 1, axis=0)
            pre.append(jnp.where(seg0, tail[SUBLANES - k:SUBLANES - k + 1], from_prev_seg))
            tail_ref[SUBLANES - k:SUBLANES - k + 1, :] = u[SUBLANES * (SEG_LEN - k + 1) - 1:
                                                           SUBLANES * (SEG_LEN - k + 1)]
        ext = jnp.concatenate(pre + [u], axis=0)
        uc = cb
        for j in range(CONV_WIDTH):
            uc = uc + cw[j:j + 1, :] * ext[SUBLANES * j:SUBLANES * j + LRU_CHUNK]
        a, b = _lru_gates(uc, wg_ref, ba, bxb, ncs)

        h = vrow(b, 0)
        p = vrow(a, 0)
        for tau in range(1, SEG_LEN):
            a_t = vrow(a, tau)
            h = a_t * h + vrow(b, tau)
            p = a_t * p
        h_in = h_ref[0:1, :]
        seg_end = _scan_rows(p, jnp.where(seg0, h + p * h_in, h))
        h = jnp.where(seg0, h_in, pltpu.roll(seg_end, 1, axis=0))
        h_ref[...] = jnp.broadcast_to(seg_end[SUBLANES - 1:SUBLANES], h_ref.shape)
        hs = []
        for tau in range(SEG_LEN):
            h = vrow(a, tau) * h + vrow(b, tau)
            hs.append(h)
        h_all = jnp.concatenate(hs, axis=0)

        y = h_all * _gelu_tanh(to_segment_major(gate_ref, r0, gbuf_ref))
        yn = y * _rms_scale(y) * gl
        for c in range(lane_chunks):
            ybuf_ref[c] = yn[:, c * LANES:(c + 1) * LANES]
        for s in range(SUBLANES):
            rows = jnp.concatenate([ybuf_ref[c, pl.ds(s, SEG_LEN, stride=SUBLANES), :]
                                    for c in range(lane_chunks)], axis=1)
            y_ref[0, pl.ds(pl.multiple_of(r0 + s * SEG_LEN, SEG_LEN), SEG_LEN), :] = rows.astype(BF16)
        return carry

    lax.fori_loop(0, n_chunks, chunk, 0)

    @pl.when(ti == n_t - 1)
    def _():
        conv_ref[0] = tail_ref[SUBLANES - (CONV_WIDTH - 1):SUBLANES, :]
        hlast_ref[0] = h_ref[0:1, :]


def _lru_prompt(u, gate, conv_prev, h0, lw, tt):
    b, s, _ = u.shape
    assert s % tt == 0 and tt % LRU_CHUNK == 0
    seq = lambda bi, ti: (bi, ti, 0)
    per_b = lambda bi, ti: (bi, 0, 0)
    c2 = lambda bi, ti: (0, 0)
    c3 = lambda bi, ti: (0, 0, 0)
    vec = pl.BlockSpec((1, D_LRU), c2)
    return pl.pallas_call(
        functools.partial(_lru_prompt_kernel, n_chunks=tt // LRU_CHUNK),
        grid=(b, s // tt),
        in_specs=[
            pl.BlockSpec((1, tt, D_LRU), seq),
            pl.BlockSpec((1, tt, D_LRU), seq),
            pl.BlockSpec((1, CONV_WIDTH - 1, D_LRU), per_b),
            pl.BlockSpec((1, 1, D_LRU), per_b),
            pl.BlockSpec((CONV_WIDTH, D_LRU), c2),
            vec,
            pl.BlockSpec((2, D_LRU // 2, D_LRU), c3),
            vec, vec, vec, vec,
        ],
        out_specs=[
            pl.BlockSpec((1, tt, D_LRU), seq),
            pl.BlockSpec((1, CONV_WIDTH - 1, D_LRU), per_b),
            pl.BlockSpec((1, 1, D_LRU), per_b),
        ],
        out_shape=[
            jax.ShapeDtypeStruct((b, s, D_LRU), BF16),
            jax.ShapeDtypeStruct((b, CONV_WIDTH - 1, D_LRU), F32),
            jax.ShapeDtypeStruct((b, 1, D_LRU), F32),
        ],
        scratch_shapes=[pltpu.VMEM((SUBLANES, D_LRU), F32), pltpu.VMEM((SUBLANES, D_LRU), F32)]
        + [pltpu.VMEM((D_LRU // LANES, LRU_CHUNK, LANES), F32)] * 3,
        compiler_params=pltpu.CompilerParams(
            dimension_semantics=("arbitrary", "arbitrary"), vmem_limit_bytes=VMEM_LIMIT_BYTES),
        name="lru_prompt",
    )(u, gate, conv_prev, h0, lw["conv_w"], lw["conv_b"], lw["w_gates"], lw["b_a"], lw["b_x"],
      lw["neg_c_softplus"], lw["g_lru"])


def _lru_sample_kernel(u_ref, gate_ref, cprev_ref, h0_ref, cw_ref, cb_ref, wg_ref, ba_ref, bx_ref,
                       ncs_ref, gl_ref, y_ref, conv_ref, hlast_ref):
    t_len, nb, _ = u_ref.shape
    cw = cw_ref[...]
    hist = [cprev_ref[j] for j in range(CONV_WIDTH - 1)] + [u_ref[t] for t in range(t_len)]
    ucs = []
    for t in range(t_len):
        uc = cb_ref[...]
        for j in range(CONV_WIDTH):
            uc = uc + cw[j:j + 1, :] * hist[t + j]
        ucs.append(uc)
    uc_all = jnp.concatenate(ucs, axis=0)
    a, b = _lru_gates(uc_all, wg_ref, ba_ref[...], bx_ref[...], ncs_ref[...])
    h = h0_ref[...]
    gl = gl_ref[...]
    for t in range(t_len):
        h = a[t * nb:(t + 1) * nb] * h + b[t * nb:(t + 1) * nb]
        y = h * _gelu_tanh(gate_ref[t])
        y_ref[t] = (y * _rms_scale(y) * gl).astype(BF16)
    hlast_ref[...] = h
    for j in range(CONV_WIDTH - 1):
        conv_ref[j] = hist[t_len + j]


def _lru_sample(u_t, gate_t, conv_prev_t, h0, lw):
    t_len, nb, _ = u_t.shape
    return pl.pallas_call(
        _lru_sample_kernel,
        out_shape=[
            jax.ShapeDtypeStruct((t_len, nb, D_LRU), BF16),
            jax.ShapeDtypeStruct((CONV_WIDTH - 1, nb, D_LRU), F32),
            jax.ShapeDtypeStruct((nb, D_LRU), F32),
        ],
        name="lru_sample",
    )(u_t, gate_t, conv_prev_t, h0, lw["conv_w"], lw["conv_b"], lw["w_gates"], lw["b_a"], lw["b_x"],
      lw["neg_c_softplus"], lw["g_lru"])


def _attn_prompt_kernel(q_ref, k_ref, v_ref, o_ref,
                        qp1, qp4, qp16, kb1, kb4, kb16, vb1, vb4, vb16, res_a, res_l, res_m, bias_ref,
                        stage_ref):
    sb = pl.program_id(2)
    qps = (qp1, qp4, qp16)
    kbs = (kb1, kb4, kb16)
    vbs = (vb1, vb4, vb16)
    dils = tuple(d for _, d in DILATED_PATTERNS)
    assert dils == (1, 4, 16)
    lane = lax.broadcasted_iota(jnp.int32, (QBLK, LANES), 1)
    head0 = lane < HEAD_DIM

    qi = lax.broadcasted_iota(jnp.int32, (QBLK, 2 * QBLK), 0)
    ki = lax.broadcasted_iota(jnp.int32, (QBLK, 2 * QBLK), 1)
    delta = qi + QBLK - ki
    band = (delta >= 0) & (delta <= KEYS_PER_PATTERN - 1)
    bias_ref[0] = jnp.where(band, 0.0, NEG_BIG)
    bias_ref[1] = jnp.where(band & (ki >= QBLK), 0.0, NEG_BIG)

    for kb, vb, d in zip(kbs, vbs, dils):
        cls_len = SUPER // d

        @pl.when(sb == 0)
        def _():
            kb[:, 0:QBLK, :] = jnp.zeros((d, QBLK, LANES), BF16)
            vb[:, 0:QBLK, 0:LANES] = jnp.zeros((d, QBLK, LANES), BF16)
            vb[:, :, LANES:2 * LANES] = jnp.ones((d, QBLK + cls_len, LANES), BF16)

        @pl.when(sb > 0)
        def _():
            kb[:, 0:QBLK, :] = kb[:, cls_len:cls_len + QBLK, :]
            vb[:, 0:QBLK, 0:LANES] = vb[:, cls_len:cls_len + QBLK, 0:LANES]

    def put(x, p, c, which):
        n_rows = x.shape[0]
        if which == "q":
            qps[p][c] = x.astype(BF16)
        elif which == "k":
            kbs[p][c, QBLK:QBLK + n_rows, :] = x.astype(BF16)
        else:
            vbs[p][c, QBLK:QBLK + n_rows, 0:LANES] = x.astype(BF16)

    def tile(p, c, i):
        d = dils[p]
        r0 = i * QBLK
        q_t = qps[p][c, pl.ds(r0, QBLK), :]
        k_t = kbs[p][c, pl.ds(r0, 2 * QBLK), :]
        bias = bias_ref[(sb == 0).astype(jnp.int32)] if i == 0 else bias_ref[0]
        probs, maxes = [], []
        for h in range(HEADS_PER_STEP):
            hm = head0 if h == 0 else jnp.logical_not(head0)
            qh = jnp.where(hm, q_t, jnp.zeros_like(q_t))
            s = lax.dot_general(qh, k_t, (((1,), (1,)), ((), ())), preferred_element_type=F32)
            s = s + bias
            m = jnp.max(s, axis=-1, keepdims=True)
            probs.append(jnp.exp2(s - m).astype(BF16))
            maxes.append(jnp.broadcast_to(m, (QBLK, LANES)))
        v_t = vbs[p][c, pl.ds(r0, 2 * QBLK), :]
        pv = jnp.dot(jnp.concatenate(probs, axis=0), v_t, preferred_element_type=F32)
        pv0, pv1 = pv[0:QBLK], pv[QBLK:2 * QBLK]
        if d == 1:
            dst = pl.ds(r0, QBLK)
        elif d == MERGE_DIL:
            dst = pl.ds(c * (SUPER // MERGE_DIL) + r0, QBLK)
        else:
            dst = pl.ds((c % MERGE_DIL) * (SUPER // MERGE_DIL) + c // MERGE_DIL, QBLK, stride=MERGE_DIL)
        res_a[p, dst, :] = jnp.where(head0, pv0[:, 0:LANES], pv1[:, 0:LANES])
        res_l[p, dst, :] = jnp.where(head0, pv0[:, LANES:2 * LANES], pv1[:, LANES:2 * LANES])
        res_m[p, dst, :] = jnp.where(head0, maxes[0], maxes[1])

    sources = (("q", q_ref), ("k", k_ref), ("v", v_ref))
    for which, ref in sources:
        put(ref[0], 0, 0, which)
    for i in range(SUPER // QBLK):
        tile(0, 0, i)
    for idx, (which, ref) in enumerate(sources):
        for c in range(4):
            x = ref[0, pl.ds(c, SUPER // 4, stride=4), :]
            stage_ref[idx, c] = x
            put(x, 1, c, which)
        for r in range(16):
            put(stage_ref[idx, r % 4, pl.ds(r // 4, QBLK, stride=4), :], 2, r, which)

    merge_rows = 256
    for c in range(MERGE_DIL):
        for i in range(SUPER // MERGE_DIL // QBLK):
            tile(1, c, i)
        for a in range(dils[2] // MERGE_DIL):
            tile(2, a * MERGE_DIL + c, 0)
        for n0 in range(0, SUPER // MERGE_DIL, merge_rows):
            in_time = pl.ds(MERGE_DIL * n0 + c, merge_rows, stride=MERGE_DIL)
            in_class = pl.ds(c * (SUPER // MERGE_DIL) + n0, merge_rows)
            rows = (in_time, in_class, in_class)
            ms = [res_m[p, rows[p], :] for p in range(3)]
            m = jnp.maximum(jnp.maximum(ms[0], ms[1]), ms[2])
            ws = [jnp.exp2(x - m) for x in ms]
            den = ws[0] * res_l[0, rows[0], :] + ws[1] * res_l[1, rows[1], :] + ws[2] * res_l[2, rows[2], :]
            num = ws[0] * res_a[0, rows[0], :] + ws[1] * res_a[1, rows[1], :] + ws[2] * res_a[2, rows[2], :]
            o_ref[0, in_time, :] = num * (1.0 / den)


def _attn_prompt(q, k, v):
    b, s, _ = q.shape
    assert s % SUPER == 0
    blk = pl.BlockSpec((1, SUPER, LANES), lambda bi, hp, sb: (bi, sb, hp))
    scratch = []
    for _, d in DILATED_PATTERNS:
        scratch.append(pltpu.VMEM((d, SUPER // d, LANES), BF16))
    for _, d in DILATED_PATTERNS:
        scratch.append(pltpu.VMEM((d, QBLK + SUPER // d, LANES), BF16))
    for _, d in DILATED_PATTERNS:
        scratch.append(pltpu.VMEM((d, QBLK + SUPER // d, 2 * LANES), BF16))
    scratch += [pltpu.VMEM((3, SUPER, LANES), F32)] * 3
    scratch += [pltpu.VMEM((2, QBLK, 2 * QBLK), F32), pltpu.VMEM((3, 4, SUPER // 4, LANES), F32)]
    return pl.pallas_call(
        _attn_prompt_kernel,
        grid=(b, D_ATTN // LANES, s // SUPER),
        in_specs=[blk, blk, blk],
        out_specs=blk,
        out_shape=jax.ShapeDtypeStruct((b, s, D_ATTN), F32),
        scratch_shapes=scratch,
        compiler_params=pltpu.CompilerParams(
            dimension_semantics=("arbitrary", "arbitrary", "arbitrary"),
            vmem_limit_bytes=VMEM_LIMIT_BYTES),
        name="attn_prompt",
    )(q, k, v)


def _attn_sample_kernel(q_ref, kn_ref, vn_ref, kt_ref, vt_ref, o_ref, *, t_len, w_buf):
    rows = N_HEADS * SUBLANES
    n_pat = len(DILATED_PATTERNS)
    pad = jnp.zeros((SUBLANES - t_len, D_ATTN), F32)
    q8 = jnp.concatenate([q_ref[0], pad], axis=0)
    kn8 = jnp.concatenate([kn_ref[0], pad], axis=0).astype(BF16)
    vn8 = jnp.concatenate([vn_ref[0], pad], axis=0).astype(BF16)
    head_lanes = lambda h: slice(h * HEAD_DIM, (h + 1) * HEAD_DIM)

    s_c = jnp.concatenate(
        [jnp.dot(q8[:, head_lanes(h)].astype(BF16), kt_ref[0, h].astype(BF16), preferred_element_type=F32)
         for h in range(N_HEADS)], axis=0)
    rid = lax.broadcasted_iota(jnp.int32, (rows, D_ATTN), 0)
    lid = lax.broadcasted_iota(jnp.int32, (rows, D_ATTN), 1)
    qbd = jnp.where((rid >> SUBLANE_SHIFT) == (lid >> HEAD_SHIFT), jnp.concatenate([q8] * N_HEADS, axis=0), 0.0)
    s_n = lax.dot_general(qbd.astype(BF16), kn8, (((1,), (1,)), ((), ())), preferred_element_type=F32)

    t_c = lax.broadcasted_iota(jnp.int32, (rows, w_buf), 0) & (SUBLANES - 1)
    d_c = w_buf + t_c - lax.broadcasted_iota(jnp.int32, (rows, w_buf), 1)
    t_n = lax.broadcasted_iota(jnp.int32, (rows, SUBLANES), 0) & (SUBLANES - 1)
    j_n = lax.broadcasted_iota(jnp.int32, (rows, SUBLANES), 1)
    d_n = t_n - j_n
    new_ok = (d_n >= 0) & (j_n < t_len)

    p_cs, p_ns, lses, ls = [], [], [], []
    for window, dil in DILATED_PATTERNS:
        ok_c = ((d_c & (dil - 1)) == 0) & (d_c <= window)
        ok_n = new_ok & ((d_n & (dil - 1)) == 0) & (d_n <= window)
        sc = jnp.where(ok_c, s_c, NEG_BIG)
        sn = jnp.where(ok_n, s_n, NEG_BIG)
        m = jnp.maximum(jnp.max(sc, axis=-1, keepdims=True), jnp.max(sn, axis=-1, keepdims=True))
        pc = jnp.exp2(sc - m)
        pn = jnp.exp2(sn - m)
        l = jnp.sum(pc, axis=-1, keepdims=True) + jnp.sum(pn, axis=-1, keepdims=True)
        p_cs.append(pc)
        p_ns.append(pn)
        ls.append(l)
        lses.append(m + jnp.log2(l))
    m = jnp.maximum(jnp.maximum(lses[0], lses[1]), lses[2])
    es = [jnp.exp2(x - m) for x in lses]
    inv_den = 1.0 / (es[0] + es[1] + es[2])
    wts = [es[p] * inv_den * (1.0 / ls[p]) for p in range(n_pat)]

    o_new = jnp.dot(jnp.concatenate(p_ns, axis=0).astype(BF16), vn8, preferred_element_type=F32)
    for h in range(N_HEADS):
        rs = slice(h * SUBLANES, (h + 1) * SUBLANES)
        p_h = jnp.concatenate([p_cs[p][rs] for p in range(n_pat)], axis=0).astype(BF16)
        o_h = lax.dot_general(p_h, vt_ref[0, h].astype(BF16), (((1,), (1,)), ((), ())),
                              preferred_element_type=F32)
        out = None
        for p in range(n_pat):
            new_part = o_new[p * rows + h * SUBLANES:p * rows + (h + 1) * SUBLANES, head_lanes(h)]
            term = wts[p][rs] * (o_h[p * SUBLANES:(p + 1) * SUBLANES] + new_part)
            out = term if out is None else out + term
        o_ref[0, :, head_lanes(h)] = out[0:t_len]


def _attn_sample(q, k_new, v_new, cache_k, cache_v):
    b, t_len, _ = q.shape
    w_buf = cache_k.shape[1]
    assert w_buf >= MAX_WINDOW and t_len <= SUBLANES
    new = pl.BlockSpec((1, t_len, D_ATTN), lambda bi: (bi, 0, 0))
    cache = pl.BlockSpec((1, N_HEADS, HEAD_DIM, w_buf), lambda bi: (bi, 0, 0, 0))
    row_minor = lambda c: jnp.transpose(c, (0, 2, 3, 1))
    return pl.pallas_call(
        functools.partial(_attn_sample_kernel, t_len=t_len, w_buf=w_buf),
        grid=(b,),
        in_specs=[new, new, new, cache, cache],
        out_specs=new,
        out_shape=jax.ShapeDtypeStruct((b, t_len, D_ATTN), F32),
        compiler_params=pltpu.CompilerParams(
            dimension_semantics=("arbitrary",), vmem_limit_bytes=VMEM_LIMIT_BYTES),
        name="attn_sample",
    )(q, k_new, v_new, row_minor(cache_k), row_minor(cache_v))


def _mix_ffn_kernel(x_ref, yl_ref, ya_ref, ga_ref, wo_ref, gf_ref, wg_ref, wu_ref, wd_ref, gn_ref,
                    y_ref, *, final_norm):
    ya = ya_ref[...]
    ya_n = (ya * _rms_scale(ya) * ga_ref[...]).astype(BF16)
    mix = (jnp.dot(yl_ref[...], wo_ref[0:D_LRU, :], preferred_element_type=F32)
           + jnp.dot(ya_n, wo_ref[D_LRU:D_LRU + D_ATTN, :], preferred_element_type=F32))
    x1 = x_ref[...] + mix
    xn = (x1 * _rms_scale(x1) * gf_ref[...]).astype(BF16)
    x2 = x1
    for c0 in range(0, D_FF, FF_CHUNK):
        cols = slice(c0, c0 + FF_CHUNK)
        hg = jnp.dot(xn, wg_ref[:, cols], preferred_element_type=F32)
        hu = jnp.dot(xn, wu_ref[:, cols], preferred_element_type=F32)
        act = (hg * _sigmoid(hg) * hu).astype(BF16)
        x2 = x2 + jnp.dot(act, wd_ref[cols, :], preferred_element_type=F32)
    if final_norm:
        x2 = x2 * _rms_scale(x2) * gn_ref[...]
    y_ref[...] = x2


def _mix_ffn(x2d, yl, ya, pw, g_final, tm, final_norm):
    n = x2d.shape[0]
    row = lambda i: (i, 0)
    const = lambda i: (0, 0)
    once = pl.Buffered(1)
    return pl.pallas_call(
        functools.partial(_mix_ffn_kernel, final_norm=final_norm),
        grid=(n // tm,),
        in_specs=[
            pl.BlockSpec((tm, D_MODEL), row),
            pl.BlockSpec((tm, D_LRU), row),
            pl.BlockSpec((tm, D_ATTN), row),
            pl.BlockSpec((1, D_ATTN), const),
            pl.BlockSpec((D_LRU + D_ATTN, D_MODEL), const, pipeline_mode=once),
            pl.BlockSpec((1, D_MODEL), const),
            pl.BlockSpec((D_MODEL, D_FF), const, pipeline_mode=once),
            pl.BlockSpec((D_MODEL, D_FF), const, pipeline_mode=once),
            pl.BlockSpec((D_FF, D_MODEL), const, pipeline_mode=once),
            pl.BlockSpec((1, D_MODEL), const),
        ],
        out_specs=pl.BlockSpec((tm, D_MODEL), row),
        out_shape=jax.ShapeDtypeStruct((n, D_MODEL), F32),
        compiler_params=pltpu.CompilerParams(
            dimension_semantics=("arbitrary",), vmem_limit_bytes=VMEM_LIMIT_BYTES),
        name="mix_ffn",
    )(x2d, yl, ya, pw["g_attn"], pw["w_out"], pw["g_ffn"], pw["w_gate"], pw["w_up"], pw["w_down"],
      g_final)


def _block_diag_gate_weights(w_a, w_x):
    half_blocks = N_LRU_BLOCKS // 2
    half = D_LRU // 2

    def bd(w):
        eye = jnp.eye(half_blocks, dtype=w.dtype)
        return jnp.einsum("nij,nm->nimj", w, eye).reshape(half, half)

    halves = [jnp.concatenate([bd(w_a[c * half_blocks:(c + 1) * half_blocks]),
                               bd(w_x[c * half_blocks:(c + 1) * half_blocks])], axis=1)
              for c in range(2)]
    return jnp.stack(halves, axis=0).astype(BF16)


def _row(v):
    return v.reshape(1, -1).astype(F32)


def kernel(x_prompt, x_sample, state_conv, state_lru, cache_k, cache_v, norm_mix, w_in, conv_w, conv_b,
           lru_w_a, lru_b_a, lru_w_x, lru_b_x, lru_lambda, out_norm_lru, out_norm_attn, w_out, norm_ffn,
           w_gate, w_up, w_down, norm_final):
    depth = w_in.shape[0]
    bp, sp, _ = x_prompt.shape
    bs, ts, _ = x_sample.shape
    w_p = min(MAX_WINDOW, sp)
    assert ts >= CONV_WIDTH - 1
    xp = x_prompt.reshape(bp * sp, D_MODEL)
    xs = x_sample.reshape(bs * ts, D_MODEL)
    g_final = _row(norm_final)
    conv_p, lru_p, kw_p, vw_p = [], [], [], []
    conv_s, lru_s, kn_s, vn_s = [], [], [], []
    for l in range(depth):
        last = l == depth - 1
        w_in_bf = w_in[l].astype(BF16)
        g_mix = _row(norm_mix[l])
        lw = dict(
            conv_w=conv_w[l].astype(F32), conv_b=_row(conv_b[l]),
            w_gates=_block_diag_gate_weights(lru_w_a[l].astype(F32) * -LOG2_E, lru_w_x[l].astype(F32) * -LOG2_E),
            b_a=_row(lru_b_a[l]) * -LOG2_E, b_x=_row(lru_b_x[l]) * -LOG2_E,
            neg_c_softplus=_row(-LRU_C * jax.nn.softplus(-lru_lambda[l].astype(F32))),
            g_lru=_row(out_norm_lru[l]))
        pw = dict(
            g_attn=_row(out_norm_attn[l]), w_out=w_out[l].astype(BF16), g_ffn=_row(norm_ffn[l]),
            w_gate=w_gate[l].astype(BF16), w_up=w_up[l].astype(BF16), w_down=w_down[l].astype(BF16))

        u, gate, q, k, v, k_win, v_win = _inproj_window(xp, g_mix, w_in_bf, tm=1024, seq_len=sp, window=w_p)
        shp = lambda a: a.reshape(bp, sp, -1)
        yl, c_new, h_last = _lru_prompt(
            shp(u), shp(gate), jnp.zeros((bp, CONV_WIDTH - 1, D_LRU), F32),
            jnp.zeros((bp, 1, D_LRU), F32), lw, tt=2048)
        ya = _attn_prompt(shp(q), shp(k), shp(v))
        xp = _mix_ffn(xp, yl.reshape(bp * sp, D_LRU), ya.reshape(bp * sp, D_ATTN), pw, g_final,
                      tm=1024, final_norm=last)
        conv_p.append(c_new)
        lru_p.append(h_last.reshape(bp, D_LRU))
        win = lambda a: a.reshape(bp, N_HEADS, HEAD_DIM, w_p).transpose(0, 3, 1, 2)
        kw_p.append(win(k_win))
        vw_p.append(win(v_win))

        u, gate, q, k, v = _inproj(xs, g_mix, w_in_bf, tm=bs * ts)
        tmaj = lambda a: a.reshape(bs, ts, -1).transpose(1, 0, 2)
        yl_t, c_new_t, h_last = _lru_sample(
            tmaj(u), tmaj(gate), state_conv[l].astype(F32).transpose(1, 0, 2),
            state_lru[l].astype(F32), lw)
        seq = lambda a: a.reshape(bs, ts, D_ATTN)
        ya = _attn_sample(seq(q), seq(k), seq(v), cache_k[l].astype(F32), cache_v[l].astype(F32))
        xs = _mix_ffn(xs, yl_t.transpose(1, 0, 2).reshape(bs * ts, D_LRU), ya.reshape(bs * ts, D_ATTN),
                      pw, g_final, tm=bs * ts, final_norm=last)
        conv_s.append(c_new_t.transpose(1, 0, 2))
        lru_s.append(h_last)
        kn_s.append(k.reshape(bs, ts, N_HEADS, HEAD_DIM))
        vn_s.append(v.reshape(bs, ts, N_HEADS, HEAD_DIM))
    return (xp.reshape(bp, sp, D_MODEL), xs.reshape(bs, ts, D_MODEL),
            jnp.stack(conv_p, 0), jnp.stack(lru_p, 0), jnp.stack(kw_p, 0), jnp.stack(vw_p, 0),
            jnp.stack(conv_s, 0), jnp.stack(lru_s, 0), jnp.stack(kn_s, 0), jnp.stack(vn_s, 0))
```

```python
import functools
import math

import jax
import jax.numpy as jnp
from jax import lax
from jax.experimental import pallas as pl
from jax.experimental.pallas import tpu as pltpu

F32 = jnp.float32
BF16 = jnp.bfloat16

D_MODEL = 1024
D_LRU = 512
D_ATTN = 512
N_HEADS = 8
HEAD_DIM = 64
N_LRU_BLOCKS = 8
LRU_BLOCK = D_LRU // N_LRU_BLOCKS
CONV_WIDTH = 4
LRU_C = 8.0
DILATED_PATTERNS = ((128, 1), (512, 4), (2048, 16))
MAX_WINDOW = 2048
D_FF = 2816
D_IN = 2 * D_LRU + 3 * D_ATTN
RMS_EPS = 1e-6
LOG2_E = math.log2(math.e)
Q_SCALE = HEAD_DIM ** -0.5 * LOG2_E

LANES = 128
SUBLANES = 8
VMEM_LIMIT_BYTES = 56 * 1024 * 1024

KEYS_PER_PATTERN = 129
QBLK = 128
SUPER = 2048
HEADS_PER_STEP = LANES // HEAD_DIM
NEG_BIG = -1e30
MERGE_DIL = 4
LRU_CHUNK = 2048
SEG_LEN = LRU_CHUNK // SUBLANES
FF_CHUNK = 256
assert D_FF % FF_CHUNK == 0
HEAD_SHIFT = HEAD_DIM.bit_length() - 1
SUBLANE_SHIFT = SUBLANES.bit_length() - 1
assert all(d & (d - 1) == 0 for _, d in DILATED_PATTERNS) and 1 << HEAD_SHIFT == HEAD_DIM
assert all(w // d + 1 == KEYS_PER_PATTERN for w, d in DILATED_PATTERNS)


def _rms_scale(x):
    return lax.rsqrt(jnp.mean(x * x, axis=-1, keepdims=True) + RMS_EPS)


def _normed(x_ref, g_ref):
    x = x_ref[...]
    return (x * _rms_scale(x) * g_ref[...]).astype(BF16)


def _project(xn, w_ref, first_col, width=D_LRU):
    return jnp.dot(xn, w_ref[:, first_col:first_col + width], preferred_element_type=F32)


def _project_all(x_ref, g_ref, w_ref, u_ref, gate_ref, q_ref, k_ref, v_ref):
    xn = _normed(x_ref, g_ref)
    u_ref[...] = _project(xn, w_ref, 0)
    gate_ref[...] = _project(xn, w_ref, D_LRU)
    o = 2 * D_LRU
    q_ref[...] = _project(xn, w_ref, o, D_ATTN) * Q_SCALE
    k = _project(xn, w_ref, o + D_ATTN, D_ATTN)
    v = _project(xn, w_ref, o + 2 * D_ATTN, D_ATTN)
    k_ref[...] = k
    v_ref[...] = v
    return k, v


def _inproj_kernel(x_ref, xs_ref, g_ref, w_ref, u_ref, gate_ref, q_ref, k_ref, v_ref, kw_ref, vw_ref,
                   us_ref, gates_ref, qs_ref, ks_ref, vs_ref):
    k, v = _project_all(x_ref, g_ref, w_ref, u_ref, gate_ref, q_ref, k_ref, v_ref)
    kw_ref[0] = k.T
    vw_ref[0] = v.T

    @pl.when(pl.program_id(0) == pl.num_programs(0) - 1)
    def _():
        _project_all(xs_ref, g_ref, w_ref, us_ref, gates_ref, qs_ref, ks_ref, vs_ref)


def _inproj(x2d, xs2d, g, w_bf, tm, seq_len, window):
    n, ns = x2d.shape[0], xs2d.shape[0]
    assert n % seq_len == 0 and seq_len % tm == 0 and window % tm == 0 and ns % SUBLANES == 0
    per_seq, per_win, n_seq = seq_len // tm, window // tm, n // seq_len
    const = lambda i: (0, 0)
    row = lambda i: (i, 0)
    win_block = lambda i: (i // per_seq, 0, jnp.maximum(i % per_seq - (per_seq - per_win), 0))
    return pl.pallas_call(
        _inproj_kernel,
        grid=(n // tm,),
        in_specs=[
            pl.BlockSpec((tm, D_MODEL), row),
            pl.BlockSpec((ns, D_MODEL), const),
            pl.BlockSpec((1, D_MODEL), const),
            pl.BlockSpec((D_MODEL, D_IN), const, pipeline_mode=pl.Buffered(1)),
        ],
        out_specs=[pl.BlockSpec((tm, D_LRU), row)] * 5 + [pl.BlockSpec((1, D_ATTN, tm), win_block)] * 2
        + [pl.BlockSpec((ns, D_LRU), const)] * 5,
        out_shape=[jax.ShapeDtypeStruct((n, D_LRU), F32)] * 5
        + [jax.ShapeDtypeStruct((n_seq, D_ATTN, window), F32)] * 2
        + [jax.ShapeDtypeStruct((ns, D_LRU), F32)] * 5,
        compiler_params=pltpu.CompilerParams(
            dimension_semantics=("arbitrary",), vmem_limit_bytes=VMEM_LIMIT_BYTES),
        name="inproj",
    )(x2d, xs2d, g, w_bf)


def _gelu_tanh(x):
    c = math.sqrt(2.0 / math.pi)
    t = jnp.tanh(x * (c + (0.044715 * c) * (x * x)))
    half = 0.5 * x
    return half + half * t


def _sigmoid(x):
    return jax.nn.sigmoid(x)


def _lru_gates(uc, wg_ref, ba, bx_bias, neg_c_softplus):
    half = D_LRU // 2
    ucb = uc.astype(BF16)
    pre0 = jnp.dot(ucb[:, 0:half], wg_ref[0], preferred_element_type=F32)
    pre1 = jnp.dot(ucb[:, half:D_LRU], wg_ref[1], preferred_element_type=F32)
    r_pre = jnp.concatenate([pre0[:, 0:half], pre1[:, 0:half]], axis=1) + ba
    i_pre = jnp.concatenate([pre0[:, half:], pre1[:, half:]], axis=1) + bx_bias
    r = 1.0 / (1.0 + jnp.exp2(r_pre))
    i = 1.0 / (1.0 + jnp.exp2(i_pre))
    log_a = r * neg_c_softplus
    a = jnp.exp(log_a)
    z = -jnp.tanh(log_a) * (a * a + 1.0)
    root = jnp.where(z > 0.0, z * lax.rsqrt(z), 0.0)
    b = root * (i * uc)
    return a, b


def _shift_rows(x, s, fill):
    t = x.shape[0]
    if s % SUBLANES == 0:
        return jnp.concatenate([jnp.full((s, x.shape[1]), fill, x.dtype), x[0:t - s]], axis=0)
    rolled = pltpu.roll(x, s, axis=0)
    row = lax.broadcasted_iota(jnp.int32, x.shape, 0)
    return jnp.where(row < s, fill, rolled)


def _scan_rows(a, b):
    t = a.shape[0]
    s = 1
    while s < t:
        b = a * _shift_rows(b, s, 0.0) + b
        if 2 * s < t:
            a = a * _shift_rows(a, s, 1.0)
        s *= 2
    return b


def _lru_prompt_kernel(u_ref, gate_ref, cprev_ref, h0_ref, cw_ref, cb_ref, wg_ref, ba_ref, bx_ref,
                       ncs_ref, gl_ref, y_ref, conv_ref, hlast_ref, tail_ref, h_ref, ubuf_ref, gbuf_ref,
                       ybuf_ref, *, n_chunks):
    ti = pl.program_id(1)
    lane_chunks = D_LRU // LANES

    def to_segment_major(src_ref, r0, buf_ref):
        for s in range(SUBLANES):
            rows = pl.ds(pl.multiple_of(r0 + s * SEG_LEN, SEG_LEN), SEG_LEN)
            for c in range(lane_chunks):
                buf_ref[c, pl.ds(s, SEG_LEN, stride=SUBLANES), :] = src_ref[0, rows, c * LANES:(c + 1) * LANES]
        return jnp.concatenate([buf_ref[c] for c in range(lane_chunks)], axis=1)
    n_t = pl.num_programs(1)

    @pl.when(ti == 0)
    def _():
        tail_ref[...] = jnp.zeros_like(tail_ref)
        tail_ref[SUBLANES - (CONV_WIDTH - 1):SUBLANES, :] = cprev_ref[0]
        h_ref[...] = jnp.broadcast_to(h0_ref[0], h_ref.shape)

    cw = cw_ref[...]
    cb = cb_ref[...]
    ba = ba_ref[...]
    bxb = bx_ref[...]
    ncs = ncs_ref[...]
    gl = gl_ref[...]
    row8 = lax.broadcasted_iota(jnp.int32, (SUBLANES, D_LRU), 0)
    seg0 = row8 == 0
    n_prev = CONV_WIDTH - 1

    def vrow(x, tau):
        return x[SUBLANES * tau:SUBLANES * (tau + 1)]

    def chunk(ci, carry):
        r0 = pl.multiple_of(ci * LRU_CHUNK, LRU_CHUNK)
        u = to_segment_major(u_ref, r0, ubuf_ref)
        tail = tail_ref[...]
        pre = []
        for k in range(n_prev, 0, -1):
            from_prev_seg = pltpu.roll(vrow(u, SEG_LEN - k), 1, axis=0)
            pre.append(jnp.where(seg0, tail[SUBLANES - k:SUBLANES - k + 1], from_prev_seg))
            tail_ref[SUBLANES - k:SUBLANES - k + 1, :] = u[SUBLANES * (SEG_LEN - k + 1) - 1:
                                                           SUBLANES * (SEG_LEN - k + 1)]
        ext = jnp.concatenate(pre + [u], axis=0)
        uc = cb
        for j in range(CONV_WIDTH):
            uc = uc + cw[j:j + 1, :] * ext[SUBLANES * j:SUBLANES * j + LRU_CHUNK]
        a, b = _lru_gates(uc, wg_ref, ba, bxb, ncs)

        h = vrow(b, 0)
        p = vrow(a, 0)
        for tau in range(1, SEG_LEN):
            a_t = vrow(a, tau)
            h = a_t * h + vrow(b, tau)
            p = a_t * p
        h_in = h_ref[0:1, :]
        seg_end = _scan_rows(p, jnp.where(seg0, h + p * h_in, h))
        h = jnp.where(seg0, h_in, pltpu.roll(seg_end, 1, axis=0))
        h_ref[...] = jnp.broadcast_to(seg_end[SUBLANES - 1:SUBLANES], h_ref.shape)
        hs = []
        for tau in range(SEG_LEN):
            h = vrow(a, tau) * h + vrow(b, tau)
            hs.append(h)
        h_all = jnp.concatenate(hs, axis=0)

        y = h_all * _gelu_tanh(to_segment_major(gate_ref, r0, gbuf_ref))
        yn = y * _rms_scale(y) * gl
        for c in range(lane_chunks):
            ybuf_ref[c] = yn[:, c * LANES:(c + 1) * LANES]
        for s in range(SUBLANES):
            rows = jnp.concatenate([ybuf_ref[c, pl.ds(s, SEG_LEN, stride=SUBLANES), :]
                                    for c in range(lane_chunks)], axis=1)
            y_ref[0, pl.ds(pl.multiple_of(r0 + s * SEG_LEN, SEG_LEN), SEG_LEN), :] = rows.astype(BF16)
        return carry

    lax.fori_loop(0, n_chunks, chunk, 0)

    @pl.when(ti == n_t - 1)
    def _():
        conv_ref[0] = tail_ref[SUBLANES - (CONV_WIDTH - 1):SUBLANES, :]
        hlast_ref[0] = h_ref[0:1, :]


def _lru_prompt(u, gate, conv_prev, h0, lw, tt):
    b, s, _ = u.shape
    assert s % tt == 0 and tt % LRU_CHUNK == 0
    seq = lambda bi, ti: (bi, ti, 0)
    per_b = lambda bi, ti: (bi, 0, 0)
    c2 = lambda bi, ti: (0, 0)
    c3 = lambda bi, ti: (0, 0, 0)
    vec = pl.BlockSpec((1, D_LRU), c2)
    return pl.pallas_call(
        functools.partial(_lru_prompt_kernel, n_chunks=tt // LRU_CHUNK),
        grid=(b, s // tt),
        in_specs=[
            pl.BlockSpec((1, tt, D_LRU), seq),
            pl.BlockSpec((1, tt, D_LRU), seq),
            pl.BlockSpec((1, CONV_WIDTH - 1, D_LRU), per_b),
            pl.BlockSpec((1, 1, D_LRU), per_b),
            pl.BlockSpec((CONV_WIDTH, D_LRU), c2),
            vec,
            pl.BlockSpec((2, D_LRU // 2, D_LRU), c3),
            vec, vec, vec, vec,
        ],
        out_specs=[
            pl.BlockSpec((1, tt, D_LRU), seq),
            pl.BlockSpec((1, CONV_WIDTH - 1, D_LRU), per_b),
            pl.BlockSpec((1, 1, D_LRU), per_b),
        ],
        out_shape=[
            jax.ShapeDtypeStruct((b, s, D_LRU), BF16),
            jax.ShapeDtypeStruct((b, CONV_WIDTH - 1, D_LRU), F32),
            jax.ShapeDtypeStruct((b, 1, D_LRU), F32),
        ],
        scratch_shapes=[pltpu.VMEM((SUBLANES, D_LRU), F32), pltpu.VMEM((SUBLANES, D_LRU), F32)]
        + [pltpu.VMEM((D_LRU // LANES, LRU_CHUNK, LANES), F32)] * 3,
        compiler_params=pltpu.CompilerParams(
            dimension_semantics=("arbitrary", "arbitrary"), vmem_limit_bytes=VMEM_LIMIT_BYTES),
        name="lru_prompt",
    )(u, gate, conv_prev, h0, lw["conv_w"], lw["conv_b"], lw["w_gates"], lw["b_a"], lw["b_x"],
      lw["neg_c_softplus"], lw["g_lru"])


def _lru_sample_kernel(u_ref, gate_ref, cprev_ref, h0_ref, cw_ref, cb_ref, wg_ref, ba_ref, bx_ref,
                       ncs_ref, gl_ref, y_ref, conv_ref, hlast_ref):
    t_len, nb, _ = u_ref.shape
    cw = cw_ref[...]
    hist = [cprev_ref[j] for j in range(CONV_WIDTH - 1)] + [u_ref[t] for t in range(t_len)]
    ucs = []
    for t in range(t_len):
        uc = cb_ref[...]
        for j in range(CONV_WIDTH):
            uc = uc + cw[j:j + 1, :] * hist[t + j]
        ucs.append(uc)
    uc_all = jnp.concatenate(ucs, axis=0)
    a, b = _lru_gates(uc_all, wg_ref, ba_ref[...], bx_ref[...], ncs_ref[...])
    h = h0_ref[...]
    gl = gl_ref[...]
    for t in range(t_len):
        h = a[t * nb:(t + 1) * nb] * h + b[t * nb:(t + 1) * nb]
        y = h * _gelu_tanh(gate_ref[t])
        y_ref[t] = (y * _rms_scale(y) * gl).astype(BF16)
    hlast_ref[...] = h
    for j in range(CONV_WIDTH - 1):
        conv_ref[j] = hist[t_len + j]


def _lru_sample(u_t, gate_t, conv_prev_t, h0, lw):
    t_len, nb, _ = u_t.shape
    return pl.pallas_call(
        _lru_sample_kernel,
        out_shape=[
            jax.ShapeDtypeStruct((t_len, nb, D_LRU), BF16),
            jax.ShapeDtypeStruct((CONV_WIDTH - 1, nb, D_LRU), F32),
            jax.ShapeDtypeStruct((nb, D_LRU), F32),
        ],
        name="lru_sample",
    )(u_t, gate_t, conv_prev_t, h0, lw["conv_w"], lw["conv_b"], lw["w_gates"], lw["b_a"], lw["b_x"],
      lw["neg_c_softplus"], lw["g_lru"])


def _attn_prompt_kernel(*refs, n_cast):
    q_ref, k_ref, v_ref = refs[0:3]
    cast_in = refs[3:3 + n_cast]
    o_ref = refs[3 + n_cast]
    cast_out = refs[4 + n_cast:4 + 2 * n_cast]
    (qp1, qp4, qp16, kb1, kb4, kb16, vb1, vb4, vb16, res_a, res_l, res_m, bias_ref,
     stage_ref) = refs[4 + 2 * n_cast:]
    for src, dst in zip(cast_in, cast_out):
        dst[...] = src[...].astype(BF16)
    sb = pl.program_id(2)
    qps = (qp1, qp4, qp16)
    kbs = (kb1, kb4, kb16)
    vbs = (vb1, vb4, vb16)
    dils = tuple(d for _, d in DILATED_PATTERNS)
    assert dils == (1, 4, 16)
    lane = lax.broadcasted_iota(jnp.int32, (QBLK, LANES), 1)
    head0 = lane < HEAD_DIM

    qi = lax.broadcasted_iota(jnp.int32, (QBLK, 2 * QBLK), 0)
    ki = lax.broadcasted_iota(jnp.int32, (QBLK, 2 * QBLK), 1)
    delta = qi + QBLK - ki
    band = (delta >= 0) & (delta <= KEYS_PER_PATTERN - 1)
    bias_ref[0] = jnp.where(band, 0.0, NEG_BIG)
    bias_ref[1] = jnp.where(band & (ki >= QBLK), 0.0, NEG_BIG)

    for kb, vb, d in zip(kbs, vbs, dils):
        cls_len = SUPER // d

        @pl.when(sb == 0)
        def _():
            kb[:, 0:QBLK, :] = jnp.zeros((d, QBLK, LANES), BF16)
            vb[:, 0:QBLK, 0:LANES] = jnp.zeros((d, QBLK, LANES), BF16)
            vb[:, :, LANES:2 * LANES] = jnp.ones((d, QBLK + cls_len, LANES), BF16)

        @pl.when(sb > 0)
        def _():
            kb[:, 0:QBLK, :] = kb[:, cls_len:cls_len + QBLK, :]
            vb[:, 0:QBLK, 0:LANES] = vb[:, cls_len:cls_len + QBLK, 0:LANES]

    def put(x, p, c, which):
        n_rows = x.shape[0]
        if which == "q":
            qps[p][c] = x.astype(BF16)
        elif which == "k":
            kbs[p][c, QBLK:QBLK + n_rows, :] = x.astype(BF16)
        else:
            vbs[p][c, QBLK:QBLK + n_rows, 0:LANES] = x.astype(BF16)

    def tile(p, c, i):
        d = dils[p]
        r0 = i * QBLK
        q_t = qps[p][c, pl.ds(r0, QBLK), :]
        k_t = kbs[p][c, pl.ds(r0, 2 * QBLK), :]
        bias = bias_ref[(sb == 0).astype(jnp.int32)] if i == 0 else bias_ref[0]
        probs, maxes = [], []
        for h in range(HEADS_PER_STEP):
            hm = head0 if h == 0 else jnp.logical_not(head0)
            qh = jnp.where(hm, q_t, jnp.zeros_like(q_t))
            s = lax.dot_general(qh, k_t, (((1,), (1,)), ((), ())), preferred_element_type=F32)
            s = s + bias
            m = jnp.max(s, axis=-1, keepdims=True)
            probs.append(jnp.exp2(s - m).astype(BF16))
            maxes.append(jnp.broadcast_to(m, (QBLK, LANES)))
        v_t = vbs[p][c, pl.ds(r0, 2 * QBLK), :]
        pv = jnp.dot(jnp.concatenate(probs, axis=0), v_t, preferred_element_type=F32)
        pv0, pv1 = pv[0:QBLK], pv[QBLK:2 * QBLK]
        if d == 1:
            dst = pl.ds(r0, QBLK)
        elif d == MERGE_DIL:
            dst = pl.ds(c * (SUPER // MERGE_DIL) + r0, QBLK)
        else:
            dst = pl.ds((c % MERGE_DIL) * (SUPER // MERGE_DIL) + c // MERGE_DIL, QBLK, stride=MERGE_DIL)
        res_a[p, dst, :] = jnp.where(head0, pv0[:, 0:LANES], pv1[:, 0:LANES])
        res_l[p, dst, :] = jnp.where(head0, pv0[:, LANES:2 * LANES], pv1[:, LANES:2 * LANES])
        res_m[p, dst, :] = jnp.where(head0, maxes[0], maxes[1])

    sources = (("q", q_ref), ("k", k_ref), ("v", v_ref))
    for which, ref in sources:
        put(ref[0], 0, 0, which)
    for i in range(SUPER // QBLK):
        tile(0, 0, i)
    for idx, (which, ref) in enumerate(sources):
        for c in range(4):
            x = ref[0, pl.ds(c, SUPER // 4, stride=4), :]
            stage_ref[idx, c] = x
            put(x, 1, c, which)
        for r in range(16):
            put(stage_ref[idx, r % 4, pl.ds(r // 4, QBLK, stride=4), :], 2, r, which)

    merge_rows = 256
    for c in range(MERGE_DIL):
        for i in range(SUPER // MERGE_DIL // QBLK):
            tile(1, c, i)
        for a in range(dils[2] // MERGE_DIL):
            tile(2, a * MERGE_DIL + c, 0)
        for n0 in range(0, SUPER // MERGE_DIL, merge_rows):
            in_time = pl.ds(MERGE_DIL * n0 + c, merge_rows, stride=MERGE_DIL)
            in_class = pl.ds(c * (SUPER // MERGE_DIL) + n0, merge_rows)
            rows = (in_time, in_class, in_class)
            ms = [res_m[p, rows[p], :] for p in range(3)]
            m = jnp.maximum(jnp.maximum(ms[0], ms[1]), ms[2])
            ws = [jnp.exp2(x - m) for x in ms]
            den = ws[0] * res_l[0, rows[0], :] + ws[1] * res_l[1, rows[1], :] + ws[2] * res_l[2, rows[2], :]
            num = ws[0] * res_a[0, rows[0], :] + ws[1] * res_a[1, rows[1], :] + ws[2] * res_a[2, rows[2], :]
            o_ref[0, in_time, :] = num * (1.0 / den)


def _attn_prompt(q, k, v, to_bf16=()):
    b, s, _ = q.shape
    assert s % SUPER == 0
    n_hp, n_sb = D_ATTN // LANES, s // SUPER
    n_steps = b * n_hp * n_sb
    blk = pl.BlockSpec((1, SUPER, LANES), lambda bi, hp, sb: (bi, sb, hp))
    cast_specs = []
    for w in to_bf16:
        rows, cols = w.shape
        n_chunks = max(c for c in range(1, n_steps + 1)
                       if n_steps % c == 0 and rows % c == 0 and (rows // c) % (2 * SUBLANES) == 0)
        repeat = n_steps // n_chunks
        cast_specs.append(pl.BlockSpec(
            (rows // n_chunks, cols),
            lambda bi, hp, sb, repeat=repeat: (((bi * n_hp + hp) * n_sb + sb) // repeat, 0)))
    scratch = []
    for _, d in DILATED_PATTERNS:
        scratch.append(pltpu.VMEM((d, SUPER // d, LANES), BF16))
    for _, d in DILATED_PATTERNS:
        scratch.append(pltpu.VMEM((d, QBLK + SUPER // d, LANES), BF16))
    for _, d in DILATED_PATTERNS:
        scratch.append(pltpu.VMEM((d, QBLK + SUPER // d, 2 * LANES), BF16))
    scratch += [pltpu.VMEM((3, SUPER, LANES), F32)] * 3
    scratch += [pltpu.VMEM((2, QBLK, 2 * QBLK), F32), pltpu.VMEM((3, 4, SUPER // 4, LANES), F32)]
    out = pl.pallas_call(
        functools.partial(_attn_prompt_kernel, n_cast=len(to_bf16)),
        grid=(b, n_hp, n_sb),
        in_specs=[blk, blk, blk] + cast_specs,
        out_specs=[blk] + cast_specs,
        out_shape=[jax.ShapeDtypeStruct((b, s, D_ATTN), F32)]
        + [jax.ShapeDtypeStruct(w.shape, BF16) for w in to_bf16],
        scratch_shapes=scratch,
        compiler_params=pltpu.CompilerParams(
            dimension_semantics=("arbitrary", "arbitrary", "arbitrary"),
            vmem_limit_bytes=VMEM_LIMIT_BYTES),
        name="attn_prompt",
    )(q, k, v, *to_bf16)
    return out[0], out[1:]


def _attn_sample_kernel(q_ref, kn_ref, vn_ref, kt_ref, vt_ref, o_ref, *, t_len, w_buf):
    rows = N_HEADS * SUBLANES
    n_pat = len(DILATED_PATTERNS)
    pad = jnp.zeros((SUBLANES - t_len, D_ATTN), F32)
    q8 = jnp.concatenate([q_ref[0], pad], axis=0)
    kn8 = jnp.concatenate([kn_ref[0], pad], axis=0).astype(BF16)
    vn8 = jnp.concatenate([vn_ref[0], pad], axis=0).astype(BF16)
    head_lanes = lambda h: slice(h * HEAD_DIM, (h + 1) * HEAD_DIM)

    s_c = jnp.concatenate(
        [jnp.dot(q8[:, head_lanes(h)].astype(BF16), kt_ref[0, h].astype(BF16), preferred_element_type=F32)
         for h in range(N_HEADS)], axis=0)
    rid = lax.broadcasted_iota(jnp.int32, (rows, D_ATTN), 0)
    lid = lax.broadcasted_iota(jnp.int32, (rows, D_ATTN), 1)
    qbd = jnp.where((rid >> SUBLANE_SHIFT) == (lid >> HEAD_SHIFT), jnp.concatenate([q8] * N_HEADS, axis=0), 0.0)
    s_n = lax.dot_general(qbd.astype(BF16), kn8, (((1,), (1,)), ((), ())), preferred_element_type=F32)

    t_c = lax.broadcasted_iota(jnp.int32, (rows, w_buf), 0) & (SUBLANES - 1)
    d_c = w_buf + t_c - lax.broadcasted_iota(jnp.int32, (rows, w_buf), 1)
    t_n = lax.broadcasted_iota(jnp.int32, (rows, SUBLANES), 0) & (SUBLANES - 1)
    j_n = lax.broadcasted_iota(jnp.int32, (rows, SUBLANES), 1)
    d_n = t_n - j_n
    new_ok = (d_n >= 0) & (j_n < t_len)

    p_cs, p_ns, lses, ls = [], [], [], []
    for window, dil in DILATED_PATTERNS:
        ok_c = ((d_c & (dil - 1)) == 0) & (d_c <= window)
        ok_n = new_ok & ((d_n & (dil - 1)) == 0) & (d_n <= window)
        sc = jnp.where(ok_c, s_c, NEG_BIG)
        sn = jnp.where(ok_n, s_n, NEG_BIG)
        m = jnp.maximum(jnp.max(sc, axis=-1, keepdims=True), jnp.max(sn, axis=-1, keepdims=True))
        pc = jnp.exp2(sc - m)
        pn = jnp.exp2(sn - m)
        l = jnp.sum(pc, axis=-1, keepdims=True) + jnp.sum(pn, axis=-1, keepdims=True)
        p_cs.append(pc)
        p_ns.append(pn)
        ls.append(l)
        lses.append(m + jnp.log2(l))
    m = jnp.maximum(jnp.maximum(lses[0], lses[1]), lses[2])
    es = [jnp.exp2(x - m) for x in lses]
    inv_den = 1.0 / (es[0] + es[1] + es[2])
    wts = [es[p] * inv_den * (1.0 / ls[p]) for p in range(n_pat)]

    o_new = jnp.dot(jnp.concatenate(p_ns, axis=0).astype(BF16), vn8, preferred_element_type=F32)
    for h in range(N_HEADS):
        rs = slice(h * SUBLANES, (h + 1) * SUBLANES)
        p_h = jnp.concatenate([p_cs[p][rs] for p in range(n_pat)], axis=0).astype(BF16)
        o_h = lax.dot_general(p_h, vt_ref[0, h].astype(BF16), (((1,), (1,)), ((), ())),
                              preferred_element_type=F32)
        out = None
        for p in range(n_pat):
            new_part = o_new[p * rows + h * SUBLANES:p * rows + (h + 1) * SUBLANES, head_lanes(h)]
            term = wts[p][rs] * (o_h[p * SUBLANES:(p + 1) * SUBLANES] + new_part)
            out = term if out is None else out + term
        o_ref[0, :, head_lanes(h)] = out[0:t_len]


def _attn_sample(q, k_new, v_new, cache_k, cache_v):
    b, t_len, _ = q.shape
    w_buf = cache_k.shape[1]
    assert w_buf >= MAX_WINDOW and t_len <= SUBLANES
    new = pl.BlockSpec((1, t_len, D_ATTN), lambda bi: (bi, 0, 0))
    cache = pl.BlockSpec((1, N_HEADS, HEAD_DIM, w_buf), lambda bi: (bi, 0, 0, 0))
    row_minor = lambda c: jnp.transpose(c, (0, 2, 3, 1))
    return pl.pallas_call(
        functools.partial(_attn_sample_kernel, t_len=t_len, w_buf=w_buf),
        grid=(b,),
        in_specs=[new, new, new, cache, cache],
        out_specs=new,
        out_shape=jax.ShapeDtypeStruct((b, t_len, D_ATTN), F32),
        compiler_params=pltpu.CompilerParams(
            dimension_semantics=("arbitrary",), vmem_limit_bytes=VMEM_LIMIT_BYTES),
        name="attn_sample",
    )(q, k_new, v_new, row_minor(cache_k), row_minor(cache_v))


def _mix_ffn_kernel(x_ref, yl_ref, ya_ref, xs_ref, yls_ref, yas_ref, ga_ref, wo_ref, gf_ref, wg_ref, wu_ref,
                    wd_ref, gn_ref, y_ref, ys_ref, *, final_norm):
    def block(x, yl, ya):
        ya_n = (ya * _rms_scale(ya) * ga_ref[...]).astype(BF16)
        mix = (jnp.dot(yl, wo_ref[0:D_LRU, :], preferred_element_type=F32)
               + jnp.dot(ya_n, wo_ref[D_LRU:D_LRU + D_ATTN, :], preferred_element_type=F32))
        x1 = x + mix
        xn = (x1 * _rms_scale(x1) * gf_ref[...]).astype(BF16)
        x2 = x1
        for c0 in range(0, D_FF, FF_CHUNK):
            cols = slice(c0, c0 + FF_CHUNK)
            hg = jnp.dot(xn, wg_ref[:, cols], preferred_element_type=F32)
            hu = jnp.dot(xn, wu_ref[:, cols], preferred_element_type=F32)
            act = (hg * _sigmoid(hg) * hu).astype(BF16)
            x2 = x2 + jnp.dot(act, wd_ref[cols, :], preferred_element_type=F32)
        if final_norm:
            x2 = x2 * _rms_scale(x2) * gn_ref[...]
        return x2

    y_ref[...] = block(x_ref[...], yl_ref[...], ya_ref[...])

    @pl.when(pl.program_id(0) == pl.num_programs(0) - 1)
    def _():
        ys_ref[...] = block(xs_ref[...], yls_ref[...], yas_ref[...])


def _mix_ffn(x2d, yl, ya, xs2d, yl_s, ya_s, pw, g_final, tm, final_norm):
    n, ns = x2d.shape[0], xs2d.shape[0]
    assert n % tm == 0 and ns % (2 * SUBLANES) == 0
    row = lambda i: (i, 0)
    const = lambda i: (0, 0)
    once = pl.Buffered(1)
    few = lambda width: pl.BlockSpec((ns, width), const)
    return pl.pallas_call(
        functools.partial(_mix_ffn_kernel, final_norm=final_norm),
        grid=(n // tm,),
        in_specs=[
            pl.BlockSpec((tm, D_MODEL), row),
            pl.BlockSpec((tm, D_LRU), row),
            pl.BlockSpec((tm, D_ATTN), row),
            few(D_MODEL), few(D_LRU), few(D_ATTN),
            pl.BlockSpec((1, D_ATTN), const),
            pl.BlockSpec((D_LRU + D_ATTN, D_MODEL), const, pipeline_mode=once),
            pl.BlockSpec((1, D_MODEL), const),
            pl.BlockSpec((D_MODEL, D_FF), const, pipeline_mode=once),
            pl.BlockSpec((D_MODEL, D_FF), const, pipeline_mode=once),
            pl.BlockSpec((D_FF, D_MODEL), const, pipeline_mode=once),
            pl.BlockSpec((1, D_MODEL), const),
        ],
        out_specs=[pl.BlockSpec((tm, D_MODEL), row), few(D_MODEL)],
        out_shape=[jax.ShapeDtypeStruct((n, D_MODEL), F32), jax.ShapeDtypeStruct((ns, D_MODEL), F32)],
        compiler_params=pltpu.CompilerParams(
            dimension_semantics=("arbitrary",), vmem_limit_bytes=VMEM_LIMIT_BYTES),
        name="mix_ffn",
    )(x2d, yl, ya, xs2d, yl_s, ya_s, pw["g_attn"], pw["w_out"], pw["g_ffn"], pw["w_gate"], pw["w_up"],
      pw["w_down"], g_final)


def _block_diag_gate_weights(w_a, w_x):
    half_blocks = N_LRU_BLOCKS // 2
    half = D_LRU // 2

    def bd(w):
        eye = jnp.eye(half_blocks, dtype=w.dtype)
        return jnp.einsum("nij,nm->nimj", w, eye).reshape(half, half)

    halves = [jnp.concatenate([bd(w_a[c * half_blocks:(c + 1) * half_blocks]),
                               bd(w_x[c * half_blocks:(c + 1) * half_blocks])], axis=1)
              for c in range(2)]
    return jnp.stack(halves, axis=0).astype(BF16)


def _row(v):
    return v.reshape(1, -1).astype(F32)


def kernel(x_prompt, x_sample, state_conv, state_lru, cache_k, cache_v, norm_mix, w_in, conv_w, conv_b,
           lru_w_a, lru_b_a, lru_w_x, lru_b_x, lru_lambda, out_norm_lru, out_norm_attn, w_out, norm_ffn,
           w_gate, w_up, w_down, norm_final):
    depth = w_in.shape[0]
    bp, sp, _ = x_prompt.shape
    bs, ts, _ = x_sample.shape
    w_p = min(MAX_WINDOW, sp)
    assert ts >= CONV_WIDTH - 1
    xp = x_prompt.reshape(bp * sp, D_MODEL)
    xs = x_sample.reshape(bs * ts, D_MODEL)
    g_final = _row(norm_final)
    conv_p, lru_p, kw_p, vw_p = [], [], [], []
    conv_s, lru_s, kn_s, vn_s = [], [], [], []
    for l in range(depth):
        last = l == depth - 1
        w_in_bf = w_in[l].astype(BF16)
        g_mix = _row(norm_mix[l])
        lw = dict(
            conv_w=conv_w[l].astype(F32), conv_b=_row(conv_b[l]),
            w_gates=_block_diag_gate_weights(lru_w_a[l].astype(F32) * -LOG2_E, lru_w_x[l].astype(F32) * -LOG2_E),
            b_a=_row(lru_b_a[l]) * -LOG2_E, b_x=_row(lru_b_x[l]) * -LOG2_E,
            neg_c_softplus=_row(-LRU_C * jax.nn.softplus(-lru_lambda[l].astype(F32))),
            g_lru=_row(out_norm_lru[l]))

        u, gate, q, k, v, k_win, v_win, u_s, gate_s, q_s, k_s, v_s = _inproj(
            xp, xs, g_mix, w_in_bf, tm=1024, seq_len=sp, window=w_p)
        shp = lambda a: a.reshape(bp, sp, -1)
        yl, c_new, h_last = _lru_prompt(
            shp(u), shp(gate), jnp.zeros((bp, CONV_WIDTH - 1, D_LRU), F32),
            jnp.zeros((bp, 1, D_LRU), F32), lw, tt=2048)
        late_weights = tuple(w[l].astype(F32) for w in (w_out, w_gate, w_up, w_down))
        ya, (w_out_bf, w_gate_bf, w_up_bf, w_down_bf) = _attn_prompt(shp(q), shp(k), shp(v), late_weights)
        pw = dict(g_attn=_row(out_norm_attn[l]), w_out=w_out_bf, g_ffn=_row(norm_ffn[l]),
                  w_gate=w_gate_bf, w_up=w_up_bf, w_down=w_down_bf)
        conv_p.append(c_new)
        lru_p.append(h_last.reshape(bp, D_LRU))
        win = lambda a: a.reshape(bp, N_HEADS, HEAD_DIM, w_p).transpose(0, 3, 1, 2)
        kw_p.append(win(k_win))
        vw_p.append(win(v_win))

        tmaj = lambda a: a.reshape(bs, ts, -1).transpose(1, 0, 2)
        yl_t, c_new_t, h_last = _lru_sample(
            tmaj(u_s), tmaj(gate_s), state_conv[l].astype(F32).transpose(1, 0, 2),
            state_lru[l].astype(F32), lw)
        seq = lambda a: a.reshape(bs, ts, D_ATTN)
        ya_s = _attn_sample(seq(q_s), seq(k_s), seq(v_s), cache_k[l].astype(F32), cache_v[l].astype(F32))

        xp, xs = _mix_ffn(xp, yl.reshape(bp * sp, D_LRU), ya.reshape(bp * sp, D_ATTN),
                          xs, yl_t.transpose(1, 0, 2).reshape(bs * ts, D_LRU), ya_s.reshape(bs * ts, D_ATTN),
                          pw, g_final, tm=1024, final_norm=last)
        conv_s.append(c_new_t.transpose(1, 0, 2))
        lru_s.append(h_last)
        kn_s.append(k_s.reshape(bs, ts, N_HEADS, HEAD_DIM))
        vn_s.append(v_s.reshape(bs, ts, N_HEADS, HEAD_DIM))
    return (xp.reshape(bp, sp, D_MODEL), xs.reshape(bs, ts, D_MODEL),
            jnp.stack(conv_p, 0), jnp.stack(lru_p, 0), jnp.stack(kw_p, 0), jnp.stack(vw_p, 0),
            jnp.stack(conv_s, 0), jnp.stack(lru_s, 0), jnp.stack(kn_s, 0), jnp.stack(vn_s, 0))
```

```python
import functools
import math

import jax
import jax.numpy as jnp
from jax import lax
from jax.experimental import pallas as pl
from jax.experimental.pallas import tpu as pltpu

F32 = jnp.float32
BF16 = jnp.bfloat16

D_MODEL = 1024
D_LRU = 512
D_ATTN = 512
N_HEADS = 8
HEAD_DIM = 64
N_LRU_BLOCKS = 8
LRU_BLOCK = D_LRU // N_LRU_BLOCKS
CONV_WIDTH = 4
LRU_C = 8.0
DILATED_PATTERNS = ((128, 1), (512, 4), (2048, 16))
MAX_WINDOW = 2048
D_FF = 2816
D_IN = 2 * D_LRU + 3 * D_ATTN
RMS_EPS = 1e-6
LOG2_E = math.log2(math.e)
Q_SCALE = HEAD_DIM ** -0.5 * LOG2_E

LANES = 128
SUBLANES = 8
VMEM_LIMIT_BYTES = 56 * 1024 * 1024

KEYS_PER_PATTERN = 129
QBLK = 128
SUPER = 2048
HEADS_PER_STEP = LANES // HEAD_DIM
NEG_BIG = -1e30
MERGE_DIL = 4
LRU_CHUNK = 2048
SEG_LEN = LRU_CHUNK // SUBLANES
FF_CHUNK = 256
assert D_FF % FF_CHUNK == 0
HEAD_SHIFT = HEAD_DIM.bit_length() - 1
SUBLANE_SHIFT = SUBLANES.bit_length() - 1
assert all(d & (d - 1) == 0 for _, d in DILATED_PATTERNS) and 1 << HEAD_SHIFT == HEAD_DIM
assert all(w // d + 1 == KEYS_PER_PATTERN for w, d in DILATED_PATTERNS)


def _rms_scale(x):
    return lax.rsqrt(jnp.mean(x * x, axis=-1, keepdims=True) + RMS_EPS)


def _normed(x_ref, g_ref):
    x = x_ref[...]
    return (x * _rms_scale(x) * g_ref[...]).astype(BF16)


def _project(xn, w_ref, first_col, width=D_LRU):
    return jnp.dot(xn, w_ref[:, first_col:first_col + width], preferred_element_type=F32)


def _project_all(x_ref, g_ref, w_ref, u_ref, gate_ref, q_ref, k_ref, v_ref):
    xn = _normed(x_ref, g_ref)
    u_ref[...] = _project(xn, w_ref, 0)
    gate_ref[...] = _project(xn, w_ref, D_LRU)
    o = 2 * D_LRU
    q_ref[...] = _project(xn, w_ref, o, D_ATTN) * Q_SCALE
    k = _project(xn, w_ref, o + D_ATTN, D_ATTN)
    v = _project(xn, w_ref, o + 2 * D_ATTN, D_ATTN)
    k_ref[...] = k
    v_ref[...] = v
    return k, v


X_RING = 3


def _inproj_kernel(x_hbm, xs_ref, g_ref, w_ref, u_ref, gate_ref, q_ref, k_ref, v_ref, kw_ref, vw_ref,
                   us_ref, gates_ref, qs_ref, ks_ref, vs_ref, xbuf, xsem, *, n_tiles):
    i = pl.program_id(0)
    tm = xbuf.shape[1]

    def x_copy(tile, slot):
        rows = pl.ds(pl.multiple_of(tile * tm, tm), tm)
        return pltpu.make_async_copy(x_hbm.at[rows, :], xbuf.at[slot], xsem.at[slot])

    @pl.when(i == 0)
    def _():
        for t in range(min(X_RING - 1, n_tiles)):
            x_copy(t, t).start()

    ahead = i + (X_RING - 1)

    @pl.when(ahead < n_tiles)
    def _():
        x_copy(ahead, ahead % X_RING).start()

    slot = i % X_RING
    x_copy(i, slot).wait()
    k, v = _project_all(xbuf.at[slot], g_ref, w_ref, u_ref, gate_ref, q_ref, k_ref, v_ref)
    kw_ref[0] = k.T
    vw_ref[0] = v.T

    @pl.when(pl.program_id(0) == pl.num_programs(0) - 1)
    def _():
        _project_all(xs_ref, g_ref, w_ref, us_ref, gates_ref, qs_ref, ks_ref, vs_ref)


def _inproj(x2d, xs2d, g, w_bf, tm, seq_len, window):
    n, ns = x2d.shape[0], xs2d.shape[0]
    assert n % seq_len == 0 and seq_len % tm == 0 and window % tm == 0 and ns % SUBLANES == 0
    per_seq, per_win, n_seq = seq_len // tm, window // tm, n // seq_len
    const = lambda i: (0, 0)
    row = lambda i: (i, 0)
    win_block = lambda i: (i // per_seq, 0, jnp.maximum(i % per_seq - (per_seq - per_win), 0))
    return pl.pallas_call(
        functools.partial(_inproj_kernel, n_tiles=n // tm),
        grid=(n // tm,),
        scratch_shapes=[pltpu.VMEM((X_RING, tm, D_MODEL), F32), pltpu.SemaphoreType.DMA((X_RING,))],
        in_specs=[
            pl.BlockSpec(memory_space=pl.ANY),
            pl.BlockSpec((ns, D_MODEL), const),
            pl.BlockSpec((1, D_MODEL), const),
            pl.BlockSpec((D_MODEL, D_IN), const, pipeline_mode=pl.Buffered(1)),
        ],
        out_specs=[pl.BlockSpec((tm, D_LRU), row)] * 5 + [pl.BlockSpec((1, D_ATTN, tm), win_block)] * 2
        + [pl.BlockSpec((ns, D_LRU), const)] * 5,
        out_shape=[jax.ShapeDtypeStruct((n, D_LRU), F32)] * 5
        + [jax.ShapeDtypeStruct((n_seq, D_ATTN, window), F32)] * 2
        + [jax.ShapeDtypeStruct((ns, D_LRU), F32)] * 5,
        compiler_params=pltpu.CompilerParams(
            dimension_semantics=("arbitrary",), vmem_limit_bytes=VMEM_LIMIT_BYTES),
        name="inproj",
    )(x2d, xs2d, g, w_bf)


def _gelu_tanh(x):
    c = math.sqrt(2.0 / math.pi)
    t = jnp.tanh(x * (c + (0.044715 * c) * (x * x)))
    half = 0.5 * x
    return half + half * t


def _sigmoid(x):
    return jax.nn.sigmoid(x)


def _lru_gates(uc, wg_ref, ba, bx_bias, neg_c_softplus):
    half = D_LRU // 2
    ucb = uc.astype(BF16)
    pre0 = jnp.dot(ucb[:, 0:half], wg_ref[0], preferred_element_type=F32)
    pre1 = jnp.dot(ucb[:, half:D_LRU], wg_ref[1], preferred_element_type=F32)
    r_pre = jnp.concatenate([pre0[:, 0:half], pre1[:, 0:half]], axis=1) + ba
    i_pre = jnp.concatenate([pre0[:, half:], pre1[:, half:]], axis=1) + bx_bias
    r = 1.0 / (1.0 + jnp.exp2(r_pre))
    i = 1.0 / (1.0 + jnp.exp2(i_pre))
    log_a = r * neg_c_softplus
    a = jnp.exp(log_a)
    z = -jnp.tanh(log_a) * (a * a + 1.0)
    root = jnp.where(z > 0.0, z * lax.rsqrt(z), 0.0)
    b = root * (i * uc)
    return a, b


def _shift_rows(x, s, fill):
    t = x.shape[0]
    if s % SUBLANES == 0:
        return jnp.concatenate([jnp.full((s, x.shape[1]), fill, x.dtype), x[0:t - s]], axis=0)
    rolled = pltpu.roll(x, s, axis=0)
    row = lax.broadcasted_iota(jnp.int32, x.shape, 0)
    return jnp.where(row < s, fill, rolled)


def _scan_rows(a, b):
    t = a.shape[0]
    s = 1
    while s < t:
        b = a * _shift_rows(b, s, 0.0) + b
        if 2 * s < t:
            a = a * _shift_rows(a, s, 1.0)
        s *= 2
    return b


def _lru_prompt_kernel(u_ref, gate_ref, cprev_ref, h0_ref, cw_ref, cb_ref, wg_ref, ba_ref, bx_ref,
                       ncs_ref, gl_ref, y_ref, conv_ref, hlast_ref, tail_ref, h_ref, ubuf_ref, gbuf_ref,
                       ybuf_ref, *, n_chunks):
    ti = pl.program_id(1)
    lane_chunks = D_LRU // LANES

    def to_segment_major(src_ref, r0, buf_ref):
        for s in range(SUBLANES):
            rows = pl.ds(pl.multiple_of(r0 + s * SEG_LEN, SEG_LEN), SEG_LEN)
            for c in range(lane_chunks):
                buf_ref[c, pl.ds(s, SEG_LEN, stride=SUBLANES), :] = src_ref[0, rows, c * LANES:(c + 1) * LANES]
        return jnp.concatenate([buf_ref[c] for c in range(lane_chunks)], axis=1)
    n_t = pl.num_programs(1)

    @pl.when(ti == 0)
    def _():
        tail_ref[...] = jnp.zeros_like(tail_ref)
        tail_ref[SUBLANES - (CONV_WIDTH - 1):SUBLANES, :] = cprev_ref[0]
        h_ref[...] = jnp.broadcast_to(h0_ref[0], h_ref.shape)

    cw = cw_ref[...]
    cb = cb_ref[...]
    ba = ba_ref[...]
    bxb = bx_ref[...]
    ncs = ncs_ref[...]
    gl = gl_ref[...]
    row8 = lax.broadcasted_iota(jnp.int32, (SUBLANES, D_LRU), 0)
    seg0 = row8 == 0
    n_prev = CONV_WIDTH - 1

    def vrow(x, tau):
        return x[SUBLANES * tau:SUBLANES * (tau + 1)]

    def chunk(ci, carry):
        r0 = pl.multiple_of(ci * LRU_CHUNK, LRU_CHUNK)
        u = to_segment_major(u_ref, r0, ubuf_ref)
        tail = tail_ref[...]
        pre = []
        for k in range(n_prev, 0, -1):
            from_prev_seg = pltpu.roll(vrow(u, SEG_LEN - k), 1, axis=0)
            pre.append(jnp.where(seg0, tail[SUBLANES - k:SUBLANES - k + 1], from_prev_seg))
            tail_ref[SUBLANES - k:SUBLANES - k + 1, :] = u[SUBLANES * (SEG_LEN - k + 1) - 1:
                                                           SUBLANES * (SEG_LEN - k + 1)]
        ext = jnp.concatenate(pre + [u], axis=0)
        uc = cb
        for j in range(CONV_WIDTH):
            uc = uc + cw[j:j + 1, :] * ext[SUBLANES * j:SUBLANES * j + LRU_CHUNK]
        a, b = _lru_gates(uc, wg_ref, ba, bxb, ncs)

        h = vrow(b, 0)
        p = vrow(a, 0)
        for tau in range(1, SEG_LEN):
            a_t = vrow(a, tau)
            h = a_t * h + vrow(b, tau)
            p = a_t * p
        h_in = h_ref[0:1, :]
        seg_end = _scan_rows(p, jnp.where(seg0, h + p * h_in, h))
        h = jnp.where(seg0, h_in, pltpu.roll(seg_end, 1, axis=0))
        h_ref[...] = jnp.broadcast_to(seg_end[SUBLANES - 1:SUBLANES], h_ref.shape)
        hs = []
        for tau in range(SEG_LEN):
            h = vrow(a, tau) * h + vrow(b, tau)
            hs.append(h)
        h_all = jnp.concatenate(hs, axis=0)

        y = h_all * _gelu_tanh(to_segment_major(gate_ref, r0, gbuf_ref))
        yn = y * _rms_scale(y) * gl
        for c in range(lane_chunks):
            ybuf_ref[c] = yn[:, c * LANES:(c + 1) * LANES]
        for s in range(SUBLANES):
            rows = jnp.concatenate([ybuf_ref[c, pl.ds(s, SEG_LEN, stride=SUBLANES), :]
                                    for c in range(lane_chunks)], axis=1)
            y_ref[0, pl.ds(pl.multiple_of(r0 + s * SEG_LEN, SEG_LEN), SEG_LEN), :] = rows.astype(BF16)
        return carry

    lax.fori_loop(0, n_chunks, chunk, 0)

    @pl.when(ti == n_t - 1)
    def _():
        conv_ref[0] = tail_ref[SUBLANES - (CONV_WIDTH - 1):SUBLANES, :]
        hlast_ref[0] = h_ref[0:1, :]


def _lru_prompt(u, gate, conv_prev, h0, lw, tt):
    b, s, _ = u.shape
    assert s % tt == 0 and tt % LRU_CHUNK == 0
    seq = lambda bi, ti: (bi, ti, 0)
    per_b = lambda bi, ti: (bi, 0, 0)
    c2 = lambda bi, ti: (0, 0)
    c3 = lambda bi, ti: (0, 0, 0)
    vec = pl.BlockSpec((1, D_LRU), c2)
    return pl.pallas_call(
        functools.partial(_lru_prompt_kernel, n_chunks=tt // LRU_CHUNK),
        grid=(b, s // tt),
        in_specs=[
            pl.BlockSpec((1, tt, D_LRU), seq),
            pl.BlockSpec((1, tt, D_LRU), seq),
            pl.BlockSpec((1, CONV_WIDTH - 1, D_LRU), per_b),
            pl.BlockSpec((1, 1, D_LRU), per_b),
            pl.BlockSpec((CONV_WIDTH, D_LRU), c2),
            vec,
            pl.BlockSpec((2, D_LRU // 2, D_LRU), c3),
            vec, vec, vec, vec,
        ],
        out_specs=[
            pl.BlockSpec((1, tt, D_LRU), seq),
            pl.BlockSpec((1, CONV_WIDTH - 1, D_LRU), per_b),
            pl.BlockSpec((1, 1, D_LRU), per_b),
        ],
        out_shape=[
            jax.ShapeDtypeStruct((b, s, D_LRU), BF16),
            jax.ShapeDtypeStruct((b, CONV_WIDTH - 1, D_LRU), F32),
            jax.ShapeDtypeStruct((b, 1, D_LRU), F32),
        ],
        scratch_shapes=[pltpu.VMEM((SUBLANES, D_LRU), F32), pltpu.VMEM((SUBLANES, D_LRU), F32)]
        + [pltpu.VMEM((D_LRU // LANES, LRU_CHUNK, LANES), F32)] * 3,
        compiler_params=pltpu.CompilerParams(
            dimension_semantics=("arbitrary", "arbitrary"), vmem_limit_bytes=VMEM_LIMIT_BYTES),
        name="lru_prompt",
    )(u, gate, conv_prev, h0, lw["conv_w"], lw["conv_b"], lw["w_gates"], lw["b_a"], lw["b_x"],
      lw["neg_c_softplus"], lw["g_lru"])


def _lru_sample_kernel(u_ref, gate_ref, cprev_ref, h0_ref, cw_ref, cb_ref, wg_ref, ba_ref, bx_ref,
                       ncs_ref, gl_ref, y_ref, conv_ref, hlast_ref):
    t_len, nb, _ = u_ref.shape
    cw = cw_ref[...]
    hist = [cprev_ref[j] for j in range(CONV_WIDTH - 1)] + [u_ref[t] for t in range(t_len)]
    ucs = []
    for t in range(t_len):
        uc = cb_ref[...]
        for j in range(CONV_WIDTH):
            uc = uc + cw[j:j + 1, :] * hist[t + j]
        ucs.append(uc)
    uc_all = jnp.concatenate(ucs, axis=0)
    a, b = _lru_gates(uc_all, wg_ref, ba_ref[...], bx_ref[...], ncs_ref[...])
    h = h0_ref[...]
    gl = gl_ref[...]
    for t in range(t_len):
        h = a[t * nb:(t + 1) * nb] * h + b[t * nb:(t + 1) * nb]
        y = h * _gelu_tanh(gate_ref[t])
        y_ref[t] = (y * _rms_scale(y) * gl).astype(BF16)
    hlast_ref[...] = h
    for j in range(CONV_WIDTH - 1):
        conv_ref[j] = hist[t_len + j]


def _lru_sample(u_t, gate_t, conv_prev_t, h0, lw):
    t_len, nb, _ = u_t.shape
    return pl.pallas_call(
        _lru_sample_kernel,
        out_shape=[
            jax.ShapeDtypeStruct((t_len, nb, D_LRU), BF16),
            jax.ShapeDtypeStruct((CONV_WIDTH - 1, nb, D_LRU), F32),
            jax.ShapeDtypeStruct((nb, D_LRU), F32),
        ],
        name="lru_sample",
    )(u_t, gate_t, conv_prev_t, h0, lw["conv_w"], lw["conv_b"], lw["w_gates"], lw["b_a"], lw["b_x"],
      lw["neg_c_softplus"], lw["g_lru"])


def _attn_prompt_kernel(q_ref, k_ref, v_ref, o_ref,
                        qp1, qp4, qp16, kb1, kb4, kb16, vb1, vb4, vb16, res_a, res_l, res_m, bias_ref,
                        stage_ref):
    sb = pl.program_id(2)
    qps = (qp1, qp4, qp16)
    kbs = (kb1, kb4, kb16)
    vbs = (vb1, vb4, vb16)
    dils = tuple(d for _, d in DILATED_PATTERNS)
    assert dils == (1, 4, 16)
    lane = lax.broadcasted_iota(jnp.int32, (QBLK, LANES), 1)
    head0 = lane < HEAD_DIM

    qi = lax.broadcasted_iota(jnp.int32, (QBLK, 2 * QBLK), 0)
    ki = lax.broadcasted_iota(jnp.int32, (QBLK, 2 * QBLK), 1)
    delta = qi + QBLK - ki
    band = (delta >= 0) & (delta <= KEYS_PER_PATTERN - 1)
    bias_ref[0] = jnp.where(band, 0.0, NEG_BIG)
    bias_ref[1] = jnp.where(band & (ki >= QBLK), 0.0, NEG_BIG)

    for kb, vb, d in zip(kbs, vbs, dils):
        cls_len = SUPER // d

        @pl.when(sb == 0)
        def _():
            kb[:, 0:QBLK, :] = jnp.zeros((d, QBLK, LANES), BF16)
            vb[:, 0:QBLK, 0:LANES] = jnp.zeros((d, QBLK, LANES), BF16)
            vb[:, :, LANES:2 * LANES] = jnp.ones((d, QBLK + cls_len, LANES), BF16)

        @pl.when(sb > 0)
        def _():
            kb[:, 0:QBLK, :] = kb[:, cls_len:cls_len + QBLK, :]
            vb[:, 0:QBLK, 0:LANES] = vb[:, cls_len:cls_len + QBLK, 0:LANES]

    def put(x, p, c, which):
        n_rows = x.shape[0]
        if which == "q":
            qps[p][c] = x.astype(BF16)
        elif which == "k":
            kbs[p][c, QBLK:QBLK + n_rows, :] = x.astype(BF16)
        else:
            vbs[p][c, QBLK:QBLK + n_rows, 0:LANES] = x.astype(BF16)

    def tile(p, c, i):
        d = dils[p]
        r0 = i * QBLK
        q_t = qps[p][c, pl.ds(r0, QBLK), :]
        k_t = kbs[p][c, pl.ds(r0, 2 * QBLK), :]
        bias = bias_ref[(sb == 0).astype(jnp.int32)] if i == 0 else bias_ref[0]
        probs, maxes = [], []
        for h in range(HEADS_PER_STEP):
            hm = head0 if h == 0 else jnp.logical_not(head0)
            qh = jnp.where(hm, q_t, jnp.zeros_like(q_t))
            s = lax.dot_general(qh, k_t, (((1,), (1,)), ((), ())), preferred_element_type=F32)
            s = s + bias
            m = jnp.max(s, axis=-1, keepdims=True)
            probs.append(jnp.exp2(s - m).astype(BF16))
            maxes.append(jnp.broadcast_to(m, (QBLK, LANES)))
        v_t = vbs[p][c, pl.ds(r0, 2 * QBLK), :]
        pv = jnp.dot(jnp.concatenate(probs, axis=0), v_t, preferred_element_type=F32)
        pv0, pv1 = pv[0:QBLK], pv[QBLK:2 * QBLK]
        if d == 1:
            dst = pl.ds(r0, QBLK)
        elif d == MERGE_DIL:
            dst = pl.ds(c * (SUPER // MERGE_DIL) + r0, QBLK)
        else:
            dst = pl.ds((c % MERGE_DIL) * (SUPER // MERGE_DIL) + c // MERGE_DIL, QBLK, stride=MERGE_DIL)
        res_a[p, dst, :] = jnp.where(head0, pv0[:, 0:LANES], pv1[:, 0:LANES])
        res_l[p, dst, :] = jnp.where(head0, pv0[:, LANES:2 * LANES], pv1[:, LANES:2 * LANES])
        res_m[p, dst, :] = jnp.where(head0, maxes[0], maxes[1])

    sources = (("q", q_ref), ("k", k_ref), ("v", v_ref))
    for which, ref in sources:
        put(ref[0], 0, 0, which)
    for i in range(SUPER // QBLK):
        tile(0, 0, i)
    for idx, (which, ref) in enumerate(sources):
        for c in range(4):
            x = ref[0, pl.ds(c, SUPER // 4, stride=4), :]
            stage_ref[idx, c] = x
            put(x, 1, c, which)
        for r in range(16):
            put(stage_ref[idx, r % 4, pl.ds(r // 4, QBLK, stride=4), :], 2, r, which)

    merge_rows = 256
    for c in range(MERGE_DIL):
        for i in range(SUPER // MERGE_DIL // QBLK):
            tile(1, c, i)
        for a in range(dils[2] // MERGE_DIL):
            tile(2, a * MERGE_DIL + c, 0)
        for n0 in range(0, SUPER // MERGE_DIL, merge_rows):
            in_time = pl.ds(MERGE_DIL * n0 + c, merge_rows, stride=MERGE_DIL)
            in_class = pl.ds(c * (SUPER // MERGE_DIL) + n0, merge_rows)
            rows = (in_time, in_class, in_class)
            ms = [res_m[p, rows[p], :] for p in range(3)]
            m = jnp.maximum(jnp.maximum(ms[0], ms[1]), ms[2])
            ws = [jnp.exp2(x - m) for x in ms]
            den = ws[0] * res_l[0, rows[0], :] + ws[1] * res_l[1, rows[1], :] + ws[2] * res_l[2, rows[2], :]
            num = ws[0] * res_a[0, rows[0], :] + ws[1] * res_a[1, rows[1], :] + ws[2] * res_a[2, rows[2], :]
            o_ref[0, in_time, :] = num * (1.0 / den)


def _attn_prompt(q, k, v):
    b, s, _ = q.shape
    assert s % SUPER == 0
    blk = pl.BlockSpec((1, SUPER, LANES), lambda bi, hp, sb: (bi, sb, hp))
    scratch = []
    for _, d in DILATED_PATTERNS:
        scratch.append(pltpu.VMEM((d, SUPER // d, LANES), BF16))
    for _, d in DILATED_PATTERNS:
        scratch.append(pltpu.VMEM((d, QBLK + SUPER // d, LANES), BF16))
    for _, d in DILATED_PATTERNS:
        scratch.append(pltpu.VMEM((d, QBLK + SUPER // d, 2 * LANES), BF16))
    scratch += [pltpu.VMEM((3, SUPER, LANES), F32)] * 3
    scratch += [pltpu.VMEM((2, QBLK, 2 * QBLK), F32), pltpu.VMEM((3, 4, SUPER // 4, LANES), F32)]
    return pl.pallas_call(
        _attn_prompt_kernel,
        grid=(b, D_ATTN // LANES, s // SUPER),
        in_specs=[blk, blk, blk],
        out_specs=blk,
        out_shape=jax.ShapeDtypeStruct((b, s, D_ATTN), F32),
        scratch_shapes=scratch,
        compiler_params=pltpu.CompilerParams(
            dimension_semantics=("arbitrary", "arbitrary", "arbitrary"),
            vmem_limit_bytes=VMEM_LIMIT_BYTES),
        name="attn_prompt",
    )(q, k, v)


def _attn_sample_kernel(q_ref, kn_ref, vn_ref, kt_ref, vt_ref, o_ref, *, t_len, w_buf):
    rows = N_HEADS * SUBLANES
    n_pat = len(DILATED_PATTERNS)
    pad = jnp.zeros((SUBLANES - t_len, D_ATTN), F32)
    q8 = jnp.concatenate([q_ref[0], pad], axis=0)
    kn8 = jnp.concatenate([kn_ref[0], pad], axis=0).astype(BF16)
    vn8 = jnp.concatenate([vn_ref[0], pad], axis=0).astype(BF16)
    head_lanes = lambda h: slice(h * HEAD_DIM, (h + 1) * HEAD_DIM)

    s_c = jnp.concatenate(
        [jnp.dot(q8[:, head_lanes(h)].astype(BF16), kt_ref[0, h].astype(BF16), preferred_element_type=F32)
         for h in range(N_HEADS)], axis=0)
    rid = lax.broadcasted_iota(jnp.int32, (rows, D_ATTN), 0)
    lid = lax.broadcasted_iota(jnp.int32, (rows, D_ATTN), 1)
    qbd = jnp.where((rid >> SUBLANE_SHIFT) == (lid >> HEAD_SHIFT), jnp.concatenate([q8] * N_HEADS, axis=0), 0.0)
    s_n = lax.dot_general(qbd.astype(BF16), kn8, (((1,), (1,)), ((), ())), preferred_element_type=F32)

    t_c = lax.broadcasted_iota(jnp.int32, (rows, w_buf), 0) & (SUBLANES - 1)
    d_c = w_buf + t_c - lax.broadcasted_iota(jnp.int32, (rows, w_buf), 1)
    t_n = lax.broadcasted_iota(jnp.int32, (rows, SUBLANES), 0) & (SUBLANES - 1)
    j_n = lax.broadcasted_iota(jnp.int32, (rows, SUBLANES), 1)
    d_n = t_n - j_n
    new_ok = (d_n >= 0) & (j_n < t_len)

    p_cs, p_ns, lses, ls = [], [], [], []
    for window, dil in DILATED_PATTERNS:
        ok_c = ((d_c & (dil - 1)) == 0) & (d_c <= window)
        ok_n = new_ok & ((d_n & (dil - 1)) == 0) & (d_n <= window)
        sc = jnp.where(ok_c, s_c, NEG_BIG)
        sn = jnp.where(ok_n, s_n, NEG_BIG)
        m = jnp.maximum(jnp.max(sc, axis=-1, keepdims=True), jnp.max(sn, axis=-1, keepdims=True))
        pc = jnp.exp2(sc - m)
        pn = jnp.exp2(sn - m)
        l = jnp.sum(pc, axis=-1, keepdims=True) + jnp.sum(pn, axis=-1, keepdims=True)
        p_cs.append(pc)
        p_ns.append(pn)
        ls.append(l)
        lses.append(m + jnp.log2(l))
    m = jnp.maximum(jnp.maximum(lses[0], lses[1]), lses[2])
    es = [jnp.exp2(x - m) for x in lses]
    inv_den = 1.0 / (es[0] + es[1] + es[2])
    wts = [es[p] * inv_den * (1.0 / ls[p]) for p in range(n_pat)]

    o_new = jnp.dot(jnp.concatenate(p_ns, axis=0).astype(BF16), vn8, preferred_element_type=F32)
    for h in range(N_HEADS):
        rs = slice(h * SUBLANES, (h + 1) * SUBLANES)
        p_h = jnp.concatenate([p_cs[p][rs] for p in range(n_pat)], axis=0).astype(BF16)
        o_h = lax.dot_general(p_h, vt_ref[0, h].astype(BF16), (((1,), (1,)), ((), ())),
                              preferred_element_type=F32)
        out = None
        for p in range(n_pat):
            new_part = o_new[p * rows + h * SUBLANES:p * rows + (h + 1) * SUBLANES, head_lanes(h)]
            term = wts[p][rs] * (o_h[p * SUBLANES:(p + 1) * SUBLANES] + new_part)
            out = term if out is None else out + term
        o_ref[0, :, head_lanes(h)] = out[0:t_len]


def _attn_sample(q, k_new, v_new, cache_k, cache_v):
    b, t_len, _ = q.shape
    w_buf = cache_k.shape[1]
    assert w_buf >= MAX_WINDOW and t_len <= SUBLANES
    new = pl.BlockSpec((1, t_len, D_ATTN), lambda bi: (bi, 0, 0))
    cache = pl.BlockSpec((1, N_HEADS, HEAD_DIM, w_buf), lambda bi: (bi, 0, 0, 0))
    row_minor = lambda c: jnp.transpose(c, (0, 2, 3, 1))
    return pl.pallas_call(
        functools.partial(_attn_sample_kernel, t_len=t_len, w_buf=w_buf),
        grid=(b,),
        in_specs=[new, new, new, cache, cache],
        out_specs=new,
        out_shape=jax.ShapeDtypeStruct((b, t_len, D_ATTN), F32),
        compiler_params=pltpu.CompilerParams(
            dimension_semantics=("arbitrary",), vmem_limit_bytes=VMEM_LIMIT_BYTES),
        name="attn_sample",
    )(q, k_new, v_new, row_minor(cache_k), row_minor(cache_v))


def _mix_ffn_kernel(x_ref, yl_ref, ya_ref, xs_ref, yls_ref, yas_ref, ga_ref, wo_ref, gf_ref, wg_ref, wu_ref,
                    wd_ref, gn_ref, y_ref, ys_ref, *, final_norm):
    def block(x, yl, ya):
        ya_n = (ya * _rms_scale(ya) * ga_ref[...]).astype(BF16)
        mix = (jnp.dot(yl, wo_ref[0:D_LRU, :], preferred_element_type=F32)
               + jnp.dot(ya_n, wo_ref[D_LRU:D_LRU + D_ATTN, :], preferred_element_type=F32))
        x1 = x + mix
        xn = (x1 * _rms_scale(x1) * gf_ref[...]).astype(BF16)
        x2 = x1
        for c0 in range(0, D_FF, FF_CHUNK):
            cols = slice(c0, c0 + FF_CHUNK)
            hg = jnp.dot(xn, wg_ref[:, cols], preferred_element_type=F32)
            hu = jnp.dot(xn, wu_ref[:, cols], preferred_element_type=F32)
            act = (hg * _sigmoid(hg) * hu).astype(BF16)
            x2 = x2 + jnp.dot(act, wd_ref[cols, :], preferred_element_type=F32)
        if final_norm:
            x2 = x2 * _rms_scale(x2) * gn_ref[...]
        return x2

    y_ref[...] = block(x_ref[...], yl_ref[...], ya_ref[...])

    @pl.when(pl.program_id(0) == pl.num_programs(0) - 1)
    def _():
        ys_ref[...] = block(xs_ref[...], yls_ref[...], yas_ref[...])


def _mix_ffn(x2d, yl, ya, xs2d, yl_s, ya_s, pw, g_final, tm, final_norm):
    n, ns = x2d.shape[0], xs2d.shape[0]
    assert n % tm == 0 and ns % (2 * SUBLANES) == 0
    row = lambda i: (i, 0)
    const = lambda i: (0, 0)
    once = pl.Buffered(1)
    few = lambda width: pl.BlockSpec((ns, width), const)
    return pl.pallas_call(
        functools.partial(_mix_ffn_kernel, final_norm=final_norm),
        grid=(n // tm,),
        in_specs=[
            pl.BlockSpec((tm, D_MODEL), row),
            pl.BlockSpec((tm, D_LRU), row),
            pl.BlockSpec((tm, D_ATTN), row),
            few(D_MODEL), few(D_LRU), few(D_ATTN),
            pl.BlockSpec((1, D_ATTN), const),
            pl.BlockSpec((D_LRU + D_ATTN, D_MODEL), const, pipeline_mode=once),
            pl.BlockSpec((1, D_MODEL), const),
            pl.BlockSpec((D_MODEL, D_FF), const, pipeline_mode=once),
            pl.BlockSpec((D_MODEL, D_FF), const, pipeline_mode=once),
            pl.BlockSpec((D_FF, D_MODEL), const, pipeline_mode=once),
            pl.BlockSpec((1, D_MODEL), const),
        ],
        out_specs=[pl.BlockSpec((tm, D_MODEL), row), few(D_MODEL)],
        out_shape=[jax.ShapeDtypeStruct((n, D_MODEL), F32), jax.ShapeDtypeStruct((ns, D_MODEL), F32)],
        compiler_params=pltpu.CompilerParams(
            dimension_semantics=("arbitrary",), vmem_limit_bytes=VMEM_LIMIT_BYTES),
        name="mix_ffn",
    )(x2d, yl, ya, xs2d, yl_s, ya_s, pw["g_attn"], pw["w_out"], pw["g_ffn"], pw["w_gate"], pw["w_up"],
      pw["w_down"], g_final)


def _block_diag_gate_weights(w_a, w_x):
    half_blocks = N_LRU_BLOCKS // 2
    half = D_LRU // 2

    def bd(w):
        eye = jnp.eye(half_blocks, dtype=w.dtype)
        return jnp.einsum("nij,nm->nimj", w, eye).reshape(half, half)

    halves = [jnp.concatenate([bd(w_a[c * half_blocks:(c + 1) * half_blocks]),
                               bd(w_x[c * half_blocks:(c + 1) * half_blocks])], axis=1)
              for c in range(2)]
    return jnp.stack(halves, axis=0).astype(BF16)


def _row(v):
    return v.reshape(1, -1).astype(F32)


def kernel(x_prompt, x_sample, state_conv, state_lru, cache_k, cache_v, norm_mix, w_in, conv_w, conv_b,
           lru_w_a, lru_b_a, lru_w_x, lru_b_x, lru_lambda, out_norm_lru, out_norm_attn, w_out, norm_ffn,
           w_gate, w_up, w_down, norm_final):
    depth = w_in.shape[0]
    bp, sp, _ = x_prompt.shape
    bs, ts, _ = x_sample.shape
    w_p = min(MAX_WINDOW, sp)
    assert ts >= CONV_WIDTH - 1
    xp = x_prompt.reshape(bp * sp, D_MODEL)
    xs = x_sample.reshape(bs * ts, D_MODEL)
    g_final = _row(norm_final)
    conv_p, lru_p, kw_p, vw_p = [], [], [], []
    conv_s, lru_s, kn_s, vn_s = [], [], [], []
    for l in range(depth):
        last = l == depth - 1
        w_in_bf = w_in[l].astype(BF16)
        g_mix = _row(norm_mix[l])
        lw = dict(
            conv_w=conv_w[l].astype(F32), conv_b=_row(conv_b[l]),
            w_gates=_block_diag_gate_weights(lru_w_a[l].astype(F32) * -LOG2_E, lru_w_x[l].astype(F32) * -LOG2_E),
            b_a=_row(lru_b_a[l]) * -LOG2_E, b_x=_row(lru_b_x[l]) * -LOG2_E,
            neg_c_softplus=_row(-LRU_C * jax.nn.softplus(-lru_lambda[l].astype(F32))),
            g_lru=_row(out_norm_lru[l]))
        pw = dict(
            g_attn=_row(out_norm_attn[l]), w_out=w_out[l].astype(BF16), g_ffn=_row(norm_ffn[l]),
            w_gate=w_gate[l].astype(BF16), w_up=w_up[l].astype(BF16), w_down=w_down[l].astype(BF16))

        u, gate, q, k, v, k_win, v_win, u_s, gate_s, q_s, k_s, v_s = _inproj(
            xp, xs, g_mix, w_in_bf, tm=1024, seq_len=sp, window=w_p)
        shp = lambda a: a.reshape(bp, sp, -1)
        yl, c_new, h_last = _lru_prompt(
            shp(u), shp(gate), jnp.zeros((bp, CONV_WIDTH - 1, D_LRU), F32),
            jnp.zeros((bp, 1, D_LRU), F32), lw, tt=2048)
        ya = _attn_prompt(shp(q), shp(k), shp(v))
        conv_p.append(c_new)
        lru_p.append(h_last.reshape(bp, D_LRU))
        win = lambda a: a.reshape(bp, N_HEADS, HEAD_DIM, w_p).transpose(0, 3, 1, 2)
        kw_p.append(win(k_win))
        vw_p.append(win(v_win))

        tmaj = lambda a: a.reshape(bs, ts, -1).transpose(1, 0, 2)
        yl_t, c_new_t, h_last = _lru_sample(
            tmaj(u_s), tmaj(gate_s), state_conv[l].astype(F32).transpose(1, 0, 2),
            state_lru[l].astype(F32), lw)
        seq = lambda a: a.reshape(bs, ts, D_ATTN)
        ya_s = _attn_sample(seq(q_s), seq(k_s), seq(v_s), cache_k[l].astype(F32), cache_v[l].astype(F32))

        xp, xs = _mix_ffn(xp, yl.reshape(bp * sp, D_LRU), ya.reshape(bp * sp, D_ATTN),
                          xs, yl_t.transpose(1, 0, 2).reshape(bs * ts, D_LRU), ya_s.reshape(bs * ts, D_ATTN),
                          pw, g_final, tm=1024, final_norm=last)
        conv_s.append(c_new_t.transpose(1, 0, 2))
        lru_s.append(h_last)
        kn_s.append(k_s.reshape(bs, ts, N_HEADS, HEAD_DIM))
        vn_s.append(v_s.reshape(bs, ts, N_HEADS, HEAD_DIM))
    return (xp.reshape(bp, sp, D_MODEL), xs.reshape(bs, ts, D_MODEL),
            jnp.stack(conv_p, 0), jnp.stack(lru_p, 0), jnp.stack(kw_p, 0), jnp.stack(vw_p, 0),
            jnp.stack(conv_s, 0), jnp.stack(lru_s, 0), jnp.stack(kn_s, 0), jnp.stack(vn_s, 0))
```
